```python
import jax, jax.numpy as jnp
from jax import lax
import numpy as np

D_MODEL = 1024
BATCH = 4
SEQ = 8192
DEPTH = 2
DEC_BATCH = 16
DEC_SEQ = 32
PAST_LEN = 4096

CHUNK = 64
D_MIX = D_MODEL
LRU_WIDTH = D_MIX // 2
LRU_BLOCKS = 8
LRU_BLOCK_DIM = LRU_WIDTH // LRU_BLOCKS
LRU_C = 8.0
CONV_W = 4
HG_WIDTH = D_MIX - LRU_WIDTH
HG_HEADS = 4
HG_DK = HG_WIDTH // HG_HEADS
HG_DV = HG_DK
N_GROUPS = 4
EXPERTS_PER_GROUP = 8
N_EXPERTS = N_GROUPS * EXPERTS_PER_GROUP
EXPERT_HIDDEN = D_MODEL // 4
TOP_K = 2
MOE_BLOCK = 128
IN_COLS = 2 * LRU_WIDTH + 4 * HG_WIDTH
EPS = 1e-6

kernel_name = "hymba_rglru_hgrn2_hiermoe_stream_step"


def _rms(x):
    x32 = x.astype(jnp.float32)
    return (x32 * lax.rsqrt(jnp.mean(x32 * x32, axis=-1, keepdims=True) + EPS)).astype(x.dtype)


def _causal_conv(xb, buf, w, b):
    T = xb.shape[1]
    xpad = jnp.concatenate([buf.astype(xb.dtype), xb], axis=1)
    y = b + sum(w[j] * xpad[:, j:j + T] for j in range(CONV_W))
    return y, xpad[:, T:]


def _rg_lru(xc, h0, wa, ba, wx, bx, lam):
    B, T, _ = xc.shape
    xh = xc.reshape(B, T, LRU_BLOCKS, LRU_BLOCK_DIM)
    r = jax.nn.sigmoid((jnp.einsum('btni,nij->btnj', xh, wa).reshape(B, T, LRU_WIDTH) + ba).astype(jnp.float32))
    ig = jax.nn.sigmoid((jnp.einsum('btni,nij->btnj', xh, wx).reshape(B, T, LRU_WIDTH) + bx).astype(jnp.float32))
    log_a = -LRU_C * r * jax.nn.softplus(-lam.astype(jnp.float32))
    a = jnp.exp(log_a)
    bterm = jnp.sqrt(-jnp.expm1(2.0 * log_a)) * ig * xc.astype(jnp.float32)
    bterm = bterm.at[:, 0].add(a[:, 0] * h0.astype(jnp.float32))

    def combine(left, right):
        a1, b1 = left
        a2, b2 = right
        return a1 * a2, a2 * b1 + b2

    _, h = lax.associative_scan(combine, (a, bterm), axis=1)
    return h.astype(xc.dtype), h[:, -1].astype(xc.dtype)


def _hgrn2(q, k, logf, v, S0):
    B, T, H, _ = q.shape
    C = min(CHUNK, T)
    n = T // C

    def to_chunks(z):
        return jnp.moveaxis(z.reshape(B, n, C, H, z.shape[-1]), 1, 0)

    mask = jnp.tril(jnp.ones((C, C), dtype=bool))[None, :, :, None, None]

    def step(S, inp):
        qc, kc, gc, vc = inp
        bcum = jnp.cumsum(gc, axis=1)
        o_inter = jnp.einsum('bthk,bhkv->bthv', qc * jnp.exp(bcum), S)
        decay = jnp.exp(jnp.where(mask, bcum[:, :, None] - bcum[:, None, :], -jnp.inf))
        A = jnp.einsum('bthk,bshk,btshk->bhts', qc, kc, decay)
        o_intra = jnp.einsum('bhts,bshv->bthv', A, vc)
        b_last = bcum[:, -1]
        S_new = jnp.exp(b_last)[..., None] * S + jnp.einsum('bshk,bshv->bhkv', kc * jnp.exp(b_last[:, None] - bcum), vc)
        return S_new, o_inter + o_intra

    S_fin, o = lax.scan(step, S0, (to_chunks(q), to_chunks(k), to_chunks(logf), to_chunks(v)))
    o = jnp.moveaxis(o, 0, 1).reshape(B, T, H, v.shape[-1])
    return o, S_fin


def _hier_moe(h, w_grp, b_grp, w_rt, b_rt, w_gate, w_up, w_down):
    glog = (h @ w_grp + b_grp).astype(jnp.float32)
    gprob = jax.nn.softmax(glog, axis=-1)
    _, gsel = lax.top_k(glog, 1)
    p_grp = jnp.take_along_axis(gprob, gsel, axis=-1)
    elog = (h @ w_rt + b_rt).astype(jnp.float32).reshape(-1, N_GROUPS, EXPERTS_PER_GROUP)
    elog_sel = jnp.take_along_axis(elog, gsel[:, :, None], axis=1)[:, 0]
    tv, ti = lax.top_k(elog_sel, TOP_K)
    wts = p_grp * jax.nn.softmax(tv, axis=-1)
    eid = gsel * EXPERTS_PER_GROUP + ti
    combine = jnp.einsum('nk,nke->ne', wts, jax.nn.one_hot(eid, N_EXPERTS, dtype=jnp.float32)).astype(h.dtype)
    hid = jax.nn.silu(jnp.einsum('nd,edh->neh', h, w_gate)) * jnp.einsum('nd,edh->neh', h, w_up)
    return jnp.einsum('neh,ehd->nd', hid * combine[..., None], w_down)


def _moe_tokens(h, w_grp, b_grp, w_rt, b_rt, w_gate, w_up, w_down):
    B, T, D = h.shape
    n = B * T
    flat = h.reshape(n, D)
    if n % MOE_BLOCK == 0:
        out = lax.map(lambda blk: _hier_moe(blk, w_grp, b_grp, w_rt, b_rt, w_gate, w_up, w_down),
                      flat.reshape(n // MOE_BLOCK, MOE_BLOCK, D))
    else:
        out = _hier_moe(flat, w_grp, b_grp, w_rt, b_rt, w_gate, w_up, w_down)
    return out.reshape(B, T, D)


def _trunk(x, c, conv_st, lru_st, hg_st, w_ada, b_ada, w_in, conv_w, conv_b, lru_wa, lru_ba, lru_wx, lru_bx,
           lru_lambda, hg_lower, hg_norm_w, w_out, w_grp, b_grp, w_rt, b_rt, w_gate, w_up, w_down, final_norm_w):
    B, T, _ = x.shape
    p_low = jax.nn.softmax(hg_lower.astype(jnp.float32), axis=0)
    lbs = jnp.cumsum(p_low, axis=0) - p_low[0]
    L, Hw = LRU_WIDTH, HG_WIDTH
    new_conv, new_lru, new_hg = [], [], []
    for l in range(DEPTH):
        mod = jax.nn.silu(c) @ w_ada[l] + b_ada[l]
        sh1, sc1, g1, sh2, sc2, g2 = jnp.split(mod[:, None, :], 6, axis=-1)
        h = _rms(x) * (1 + sc1) + sh1
        u = h @ w_in[l]
        xb, gb, qh, fh, ih, gh = jnp.split(u, [L, 2 * L, 2 * L + Hw, 2 * L + 2 * Hw, 2 * L + 3 * Hw], axis=-1)
        xc, cbuf = _causal_conv(xb, conv_st[l], conv_w[l], conv_b[l])
        hl, hlast = _rg_lru(xc, lru_st[l], lru_wa[l], lru_ba[l], lru_wx[l], lru_bx[l], lru_lambda[l])
        y_lru = hl * jax.nn.gelu(gb)
        q = jax.nn.silu(qh.astype(jnp.float32)).reshape(B, T, HG_HEADS, HG_DK) * (HG_DK ** -0.5)
        f = (lbs[l] + (1.0 - lbs[l]) * jax.nn.sigmoid(fh.astype(jnp.float32))).reshape(B, T, HG_HEADS, HG_DK)
        o, S = _hgrn2(q, 1.0 - f, jnp.log(f), ih.astype(jnp.float32).reshape(B, T, HG_HEADS, HG_DV),
                      hg_st[l].astype(jnp.float32))
        y_hg = (_rms(o) * hg_norm_w[l]).reshape(B, T, HG_WIDTH).astype(x.dtype) * jax.nn.silu(gh)
        mix = jnp.concatenate([y_lru, y_hg], axis=-1) @ w_out[l]
        x = x + g1 * mix
        h2 = _rms(x) * (1 + sc2) + sh2
        x = x + g2 * _moe_tokens(h2, w_grp[l], b_grp[l], w_rt[l], b_rt[l], w_gate[l], w_up[l], w_down[l])
        new_conv.append(cbuf)
        new_lru.append(hlast)
        new_hg.append(S.astype(x.dtype))
    y = _rms(x) * final_norm_w
    return y, jnp.stack(new_conv), jnp.stack(new_lru), jnp.stack(new_hg)


def setup_inputs(seed: int = 0) -> dict:
    key = jax.random.key(seed)
    ks = jax.random.split(key, 32)
    f32 = jnp.float32

    def nrm(k, shape, s):
        return jax.random.normal(k, shape, f32) * s

    u = jax.random.uniform(ks[13], (DEPTH, LRU_WIDTH), f32, 0.9, 0.999)
    a0 = u ** (1.0 / LRU_C)
    return {
        "x_prompt": nrm(ks[0], (BATCH, SEQ, D_MODEL), 1.0),
        "x_sample": nrm(ks[1], (DEC_BATCH, DEC_SEQ, D_MODEL), 1.0),
        "state_conv": nrm(ks[2], (DEPTH, DEC_BATCH, CONV_W - 1, LRU_WIDTH), 1.0),
        "state_lru": nrm(ks[3], (DEPTH, DEC_BATCH, LRU_WIDTH), 0.5),
        "state_hgrn": nrm(ks[4], (DEPTH, DEC_BATCH, HG_HEADS, HG_DK, HG_DV), 0.3),
        "c_prompt": nrm(ks[5], (BATCH, D_MODEL), 1.0),
        "c_sample": nrm(ks[6], (DEC_BATCH, D_MODEL), 1.0),
        "w_ada": nrm(ks[7], (DEPTH, D_MODEL, 6 * D_MODEL), 0.5 * D_MODEL ** -0.5),
        "b_ada": nrm(ks[8], (DEPTH, 6 * D_MODEL), 0.02),
        "w_in": nrm(ks[9], (DEPTH, D_MODEL, IN_COLS), D_MODEL ** -0.5),
        "conv_w": nrm(ks[10], (DEPTH, CONV_W, LRU_WIDTH), CONV_W ** -0.5),
        "conv_b": nrm(ks[11], (DEPTH, LRU_WIDTH), 0.02),
        "lru_wa": nrm(ks[12], (DEPTH, LRU_BLOCKS, LRU_BLOCK_DIM, LRU_BLOCK_DIM), LRU_BLOCK_DIM ** -0.5),
        "lru_ba": nrm(ks[14], (DEPTH, LRU_WIDTH), 0.1),
        "lru_wx": nrm(ks[15], (DEPTH, LRU_BLOCKS, LRU_BLOCK_DIM, LRU_BLOCK_DIM), LRU_BLOCK_DIM ** -0.5),
        "lru_bx": nrm(ks[16], (DEPTH, LRU_WIDTH), 0.1),
        "lru_lambda": jnp.log(a0) - jnp.log1p(-a0),
        "hg_lower": nrm(ks[17], (DEPTH, HG_WIDTH), 0.5),
        "hg_norm_w": 1.0 + nrm(ks[18], (DEPTH, HG_DV), 0.02),
        "w_out": nrm(ks[19], (DEPTH, D_MIX, D_MODEL), D_MIX ** -0.5),
        "w_grp": nrm(ks[20], (DEPTH, D_MODEL, N_GROUPS), D_MODEL ** -0.5),
        "b_grp": nrm(ks[21], (DEPTH, N_GROUPS), 0.01),
        "w_rt": nrm(ks[22], (DEPTH, D_MODEL, N_EXPERTS), D_MODEL ** -0.5),
        "b_rt": nrm(ks[23], (DEPTH, N_EXPERTS), 0.01),
        "w_gate": nrm(ks[24], (DEPTH, N_EXPERTS, D_MODEL, EXPERT_HIDDEN), D_MODEL ** -0.5),
        "w_up": nrm(ks[25], (DEPTH, N_EXPERTS, D_MODEL, EXPERT_HIDDEN), D_MODEL ** -0.5),
        "w_down": nrm(ks[26], (DEPTH, N_EXPERTS, EXPERT_HIDDEN, D_MODEL), EXPERT_HIDDEN ** -0.5),
        "final_norm_w": 1.0 + nrm(ks[27], (D_MODEL,), 0.02),
    }


def reference(x_prompt, x_sample, state_conv, state_lru, state_hgrn, c_prompt, c_sample, w_ada, b_ada, w_in,
              conv_w, conv_b, lru_wa, lru_ba, lru_wx, lru_bx, lru_lambda, hg_lower, hg_norm_w, w_out, w_grp,
              b_grp, w_rt, b_rt, w_gate, w_up, w_down, final_norm_w):
    weights = (w_ada, b_ada, w_in, conv_w, conv_b, lru_wa, lru_ba, lru_wx, lru_bx, lru_lambda, hg_lower,
               hg_norm_w, w_out, w_grp, b_grp, w_rt, b_rt, w_gate, w_up, w_down, final_norm_w)
    Bp = x_prompt.shape[0]
    dt = x_prompt.dtype
    zero_conv = jnp.zeros((DEPTH, Bp, CONV_W - 1, LRU_WIDTH), dt)
    zero_lru = jnp.zeros((DEPTH, Bp, LRU_WIDTH), dt)
    zero_hg = jnp.zeros((DEPTH, Bp, HG_HEADS, HG_DK, HG_DV), dt)
    y_prompt, conv_p, lru_p, hg_p = _trunk(x_prompt, c_prompt, zero_conv, zero_lru, zero_hg, *weights)
    y_sample, conv_s, lru_s, hg_s = _trunk(x_sample, c_sample, state_conv, state_lru, state_hgrn, *weights)
    return (y_prompt, y_sample, conv_p, lru_p, hg_p, conv_s, lru_s, hg_s)
```

```python
import functools

import jax
import jax.numpy as jnp
from jax import lax
from jax.experimental import pallas as pl
from jax.experimental.pallas import tpu as pltpu

D_MODEL = 1024
LRU_WIDTH = 512
LRU_BLOCKS = 8
LRU_C = 8.0
CONV_W = 4
HG_WIDTH = 512
HG_HEADS = 4
HG_DK = 128
N_GROUPS = 4
EXPERTS_PER_GROUP = 8
N_EXPERTS = 32
EXPERT_HIDDEN = 256
EPS = 1e-6

SUBLANES = 8
LANES = 128
HG_SUB = 16
EXP_CLAMP = 80.0
ROUTE_LANES = LANES
VMEM_LIMIT = 56 * 1024 * 1024

_NT = (((1,), (1,)), ((), ()))
_TN = (((0,), (0,)), ((), ()))


def _bf(x):
    return x.astype(jnp.bfloat16)


def _dot(a, b):
    return jnp.dot(a, b, preferred_element_type=jnp.float32)


def _sigmoid(x):
    return 1.0 / (1.0 + jnp.exp(-x))


def _silu(x):
    return x * _sigmoid(x)


def _gelu_tanh(x):
    return 0.5 * x * (1.0 + jnp.tanh(0.7978845608028654 * (x + 0.044715 * (x * x * x))))


def _softplus(x):
    return jnp.maximum(x, 0.0) + jnp.log(1.0 + jnp.exp(-jnp.abs(x)))


def _rms_rows(x):
    return x * lax.rsqrt(jnp.mean(x * x, axis=-1, keepdims=True) + EPS)


def _scan8(a, b):
    rows = lax.broadcasted_iota(jnp.int32, a.shape, 0)
    for d in (1, 2, 4):
        m = rows >= d
        a_sh = pltpu.roll(a, d, 0)
        b_sh = pltpu.roll(b, d, 0)
        b = jnp.where(m, a * b_sh + b, b)
        a = jnp.where(m, a * a_sh, a)
    return a, b


def _cumsum8(x):
    rows = lax.broadcasted_iota(jnp.int32, x.shape, 0)
    for d in (1, 2, 4):
        x = x + jnp.where(rows >= d, pltpu.roll(x, d, 0), 0.0)
    return x


def _block_refs(b, m):
    n, w = b.shape
    starts, ends = [], []
    for i in range(n // m):
        if i == 0:
            starts.append(jnp.zeros((m, w), jnp.float32))
        else:
            starts.append(jnp.broadcast_to(b[i * m - 1:i * m, :], (m, w)))
        ends.append(jnp.broadcast_to(b[(i + 1) * m - 1:(i + 1) * m, :], (m, w)))
    if len(starts) == 1:
        return starts[0], ends[0]
    return jnp.concatenate(starts, axis=0), jnp.concatenate(ends, axis=0)


def _mixer_kernel(*refs, tt, chunk, has_prev):
    it = iter(refs)
    x_ref = next(it)
    if has_prev:
        moe_ref = next(it)
        modp_ref = next(it)
    (mod_ref, conv0_ref, lru0_ref, hg0_ref, w_in_ref, conv_w_ref, conv_b_ref, wa_ref, wx_ref, ba_ref, bx_ref,
     lam_ref, lbs_ref, hgn_ref, w_out_ref, wr_hi_ref, wr_lo_ref, br_ref) = (next(it) for _ in range(18))
    xmid_ref, h2_ref, route_ref, conv_out_ref, lru_out_ref, hg_out_ref = (next(it) for _ in range(6))
    conv_scr, lru_scr, st_scr = (next(it) for _ in range(3))

    j = pl.program_id(1)
    last_j = pl.num_programs(1) - 1
    pad = SUBLANES - (CONV_W - 1)

    @pl.when(j == 0)
    def _():
        conv_scr[pad:SUBLANES, :] = conv0_ref[0]
        lru_scr[...] = lru0_ref[0]
        for hd in range(HG_HEADS):
            st_scr[hd] = hg0_ref[0, hd].T

    x = x_ref[0]
    if has_prev:
        x = x + modp_ref[0, 5:6, :] * moe_ref[0]
    mod = mod_ref[0]
    sh1, sc1, g1, sh2, sc2 = (mod[i:i + 1, :] for i in range(5))

    h = _rms_rows(x) * (1.0 + sc1) + sh1
    u = _dot(_bf(h), w_in_ref[...])
    lw, hw = LRU_WIDTH, HG_WIDTH
    xb, gb = u[:, :lw], u[:, lw:2 * lw]
    qh, fh = u[:, 2 * lw:2 * lw + hw], u[:, 2 * lw + hw:2 * lw + 2 * hw]
    ih, gh = u[:, 2 * lw + 2 * hw:2 * lw + 3 * hw], u[:, 2 * lw + 3 * hw:]

    conv_scr[SUBLANES:SUBLANES + tt, :] = xb
    xc = conv_b_ref[...]
    for k in range(CONV_W):
        xc = xc + conv_w_ref[k:k + 1, :] * conv_scr[pl.ds(pad + k, tt), :]
    new_hist = conv_scr[pl.ds(tt + pad, CONV_W - 1), :]
    conv_scr[pad:SUBLANES, :] = new_hist

    xc_bf = _bf(xc)
    half = lw // 2
    r_pre = jnp.concatenate([_dot(xc_bf[:, i * half:(i + 1) * half], wa_ref[i]) for i in range(2)], axis=1)
    i_pre = jnp.concatenate([_dot(xc_bf[:, i * half:(i + 1) * half], wx_ref[i]) for i in range(2)], axis=1)
    r = _sigmoid(r_pre + ba_ref[...])
    ig = _sigmoid(i_pre + bx_ref[...])
    log_a = (-LRU_C) * r * _softplus(-lam_ref[...])
    a = jnp.exp(log_a)
    th = jnp.tanh(log_a)
    one_minus_a2 = (-2.0) * th / (1.0 - th)
    bt = jnp.sqrt(one_minus_a2) * ig * xc

    carry = lru_scr[...]
    hs = []
    for g in range(tt // SUBLANES):
        sl = slice(g * SUBLANES, (g + 1) * SUBLANES)
        acum, hloc = _scan8(a[sl], bt[sl])
        hg = acum * carry + hloc
        hs.append(hg)
        carry = hg[SUBLANES - 1:SUBLANES, :]
    lru_scr[...] = carry
    hl = jnp.concatenate(hs, axis=0)
    y_lru = hl * _gelu_tanh(gb)

    lbs = lbs_ref[...]
    q = _silu(qh) * (HG_DK ** -0.5)
    f = lbs + (1.0 - lbs) * _sigmoid(fh)
    kk = 1.0 - f
    glog = jnp.log(f)

    ti = lax.broadcasted_iota(jnp.int32, (chunk, chunk), 0)
    si = lax.broadcasted_iota(jnp.int32, (chunk, chunk), 1)
    sub_shift = HG_SUB.bit_length() - 1
    mask_diag = ((ti >> sub_shift) == (si >> sub_shift)) & (ti >= si)
    levels = []
    m = HG_SUB
    while m < chunk:
        sh = m.bit_length() - 1
        levels.append((m, ((ti >> (sh + 1)) == (si >> (sh + 1))) & (((ti >> sh) & 1) == 1) & (((si >> sh) & 1) == 0)))
        m *= 2

    o_chunks = []
    for c in range(tt // chunk):
        rs = slice(c * chunk, (c + 1) * chunk)
        gc = glog[rs]
        bs, bcarry = [], None
        for g in range(chunk // SUBLANES):
            cs = _cumsum8(gc[g * SUBLANES:(g + 1) * SUBLANES])
            if bcarry is not None:
                cs = cs + bcarry
            bs.append(cs)
            bcarry = cs[SUBLANES - 1:SUBLANES, :]
        b = jnp.concatenate(bs, axis=0)
        b_last = bcarry
        qc, kc, vc = q[rs], kk[rs], ih[rs]

        st16, en16 = _block_refs(b, HG_SUB)
        q_lv = {HG_SUB: _bf(qc * jnp.exp(b - st16))}
        k_lv = {HG_SUB: _bf(kc * jnp.exp(en16 - b))}
        k_diag = _bf(kc * jnp.exp(jnp.minimum(st16 - b, EXP_CLAMP)))
        for m, _ in levels:
            if m == HG_SUB:
                continue
            st, en = _block_refs(b, m)
            q_lv[m] = _bf(qc * jnp.exp(b - st))
            k_lv[m] = _bf(kc * jnp.exp(en - b))
        q_all = _bf(qc * jnp.exp(b))
        k_all = _bf(kc * jnp.exp(b_last - b))
        v_bf = _bf(vc)
        eb_last = jnp.exp(b_last)

        o_heads = []
        for hd in range(HG_HEADS):
            hs_ = slice(hd * HG_DK, (hd + 1) * HG_DK)
            amat = jnp.where(mask_diag,
                             lax.dot_general(q_lv[HG_SUB][:, hs_], k_diag[:, hs_], _NT,
                                             preferred_element_type=jnp.float32), 0.0)
            for m, msk in levels:
                amat = amat + jnp.where(msk, lax.dot_general(q_lv[m][:, hs_], k_lv[m][:, hs_], _NT,
                                                             preferred_element_type=jnp.float32), 0.0)
            st_t = st_scr[hd]
            o_h = lax.dot_general(q_all[:, hs_], _bf(st_t), _NT, preferred_element_type=jnp.float32)
            o_h = o_h + _dot(_bf(amat), v_bf[:, hs_])
            st_scr[hd] = st_t * eb_last[:, hs_] + lax.dot_general(v_bf[:, hs_], k_all[:, hs_], _TN,
                                                                  preferred_element_type=jnp.float32)
            o_heads.append(_rms_rows(o_h) * hgn_ref[...])
        o_chunks.append(jnp.concatenate(o_heads, axis=1))
    o_all = o_chunks[0] if len(o_chunks) == 1 else jnp.concatenate(o_chunks, axis=0)
    y_hg = o_all * _silu(gh)

    mix = _dot(_bf(jnp.concatenate([y_lru, y_hg], axis=1)), w_out_ref[...])
    x_mid = x + g1 * mix
    xmid_ref[0] = x_mid

    h2 = _rms_rows(x_mid) * (1.0 + sc2) + sh2
    h2_hi = _bf(h2)
    h2_lo = _bf(h2 - h2_hi.astype(jnp.float32))
    h2_ref[0] = h2_hi
    logits = (_dot(h2_hi, wr_hi_ref[...]) + _dot(h2_hi, wr_lo_ref[...]) + _dot(h2_lo, wr_hi_ref[...])
              + br_ref[...])

    lane = lax.broadcasted_iota(jnp.int32, (tt, ROUTE_LANES), 1)
    lane_f = lane.astype(jnp.float32)
    neg, big = -1e30, 1e6
    is_grp = lane < N_GROUPS
    gl = jnp.where(is_grp, logits, neg)
    gmax = jnp.max(gl, axis=-1, keepdims=True)
    gsel = jnp.min(jnp.where(gl == gmax, lane_f, big), axis=-1, keepdims=True)
    gsum = jnp.sum(jnp.where(is_grp, jnp.exp(gl - gmax), 0.0), axis=-1, keepdims=True)
    p_grp = 1.0 / gsum
    lo = N_GROUPS + gsel * EXPERTS_PER_GROUP
    emask = (lane_f >= lo) & (lane_f < lo + EXPERTS_PER_GROUP)
    el = jnp.where(emask, logits, neg)
    tv1 = jnp.max(el, axis=-1, keepdims=True)
    ti1 = jnp.min(jnp.where(emask & (el == tv1), lane_f, big), axis=-1, keepdims=True)
    emask2 = emask & (lane_f != ti1)
    el2 = jnp.where(emask2, logits, neg)
    tv2 = jnp.max(el2, axis=-1, keepdims=True)
    ti2 = jnp.min(jnp.where(emask2 & (el2 == tv2), lane_f, big), axis=-1, keepdims=True)
    e21 = jnp.exp(tv2 - tv1)
    w1 = p_grp / (1.0 + e21)
    w2 = w1 * e21
    route = jnp.where(lane == 0, ti1 - N_GROUPS,
                      jnp.where(lane == 1, ti2 - N_GROUPS,
                                jnp.where(lane == 2, w1, jnp.where(lane == 3, w2, 0.0))))
    route_ref[0] = route

    @pl.when(j == last_j)
    def _():
        conv_out_ref[0] = new_hist
        lru_out_ref[0] = carry
        for hd in range(HG_HEADS):
            hg_out_ref[0, hd] = st_scr[hd].T


def _mixer_call(x, moe_prev, mod_prev, mod, conv0, lru0, hg0, wts, *, tt, chunk):
    bsz, t, d = x.shape
    has_prev = moe_prev is not None
    grid = (bsz, t // tt)
    tile = lambda last: pl.BlockSpec((1, tt, last), lambda b, j: (b, j, 0))
    per_b = lambda *shape: pl.BlockSpec((1,) + shape, lambda b, j: (b,) + (0,) * len(shape))
    full = lambda a: pl.BlockSpec(a.shape, lambda b, j: (0,) * a.ndim)

    args, in_specs = [x], [tile(d)]
    if has_prev:
        args += [moe_prev, mod_prev]
        in_specs += [tile(d), per_b(6, d)]
    args += [mod, conv0, lru0.reshape(bsz, 1, LRU_WIDTH), hg0]
    in_specs += [per_b(6, d), per_b(CONV_W - 1, LRU_WIDTH), per_b(1, LRU_WIDTH), per_b(HG_HEADS, HG_DK, HG_DK)]
    args += list(wts)
    in_specs += [full(w) for w in wts]

    out_shape = (
        jax.ShapeDtypeStruct((bsz, t, d), jnp.float32),
        jax.ShapeDtypeStruct((bsz, t, d), jnp.bfloat16),
        jax.ShapeDtypeStruct((bsz, t, ROUTE_LANES), jnp.float32),
        jax.ShapeDtypeStruct((bsz, CONV_W - 1, LRU_WIDTH), jnp.float32),
        jax.ShapeDtypeStruct((bsz, 1, LRU_WIDTH), jnp.float32),
        jax.ShapeDtypeStruct((bsz, HG_HEADS, HG_DK, HG_DK), jnp.float32),
    )
    out_specs = (tile(d), tile(d), tile(ROUTE_LANES), per_b(CONV_W - 1, LRU_WIDTH), per_b(1, LRU_WIDTH),
                 per_b(HG_HEADS, HG_DK, HG_DK))
    scratch = [
        pltpu.VMEM((tt + SUBLANES, LRU_WIDTH), jnp.float32),
        pltpu.VMEM((1, LRU_WIDTH), jnp.float32),
        pltpu.VMEM((HG_HEADS, HG_DK, HG_DK), jnp.float32),
    ]
    return pl.pallas_call(
        functools.partial(_mixer_kernel, tt=tt, chunk=chunk, has_prev=has_prev),
        grid=grid, in_specs=in_specs, out_specs=out_specs, out_shape=out_shape, scratch_shapes=scratch,
        compiler_params=pltpu.CompilerParams(dimension_semantics=("arbitrary", "arbitrary"),
                                             vmem_limit_bytes=VMEM_LIMIT),
        name="mixer",
    )(*args)


def _mod_kernel(c_ref, w_ref, b_ref, o_ref):
    s = _silu(c_ref[...])
    o_ref[0] = jnp.dot(s, w_ref[0], preferred_element_type=jnp.float32,
                       precision=lax.Precision.HIGHEST) + b_ref[0]


def _mod_call(c_all, w_ada, b_ada):
    depth, d, n = w_ada.shape
    rows = c_all.shape[0]
    tn = d
    return pl.pallas_call(
        _mod_kernel,
        grid=(depth, n // tn),
        in_specs=[pl.BlockSpec((rows, d), lambda l, i: (0, 0)),
                  pl.BlockSpec((1, d, tn), lambda l, i: (l, 0, i)),
                  pl.BlockSpec((1, 1, tn), lambda l, i: (l, 0, i))],
        out_specs=pl.BlockSpec((1, rows, tn), lambda l, i: (l, 0, i)),
        out_shape=jax.ShapeDtypeStruct((depth, rows, n), jnp.float32),
        compiler_params=pltpu.CompilerParams(dimension_semantics=("arbitrary", "arbitrary"),
                                             vmem_limit_bytes=VMEM_LIMIT),
        name="modulation",
    )(c_all, w_ada, b_ada.reshape(depth, 1, n))


def _moe_dense_kernel(h_ref, route_ref, wgu_ref, wd_ref, o_ref):
    e = pl.program_id(1)

    @pl.when(e == 0)
    def _():
        o_ref[...] = jnp.zeros_like(o_ref)

    ef = e.astype(jnp.float32)
    rt = route_ref[...]
    cw = jnp.where(rt[:, 0:1] == ef, rt[:, 2:3], 0.0) + jnp.where(rt[:, 1:2] == ef, rt[:, 3:4], 0.0)
    gu = _dot(h_ref[...], wgu_ref[0])
    hid = _silu(gu[:, :EXPERT_HIDDEN]) * gu[:, EXPERT_HIDDEN:] * cw
    o_ref[...] += _dot(_bf(hid), wd_ref[0])


def _moe_dense_call(h2, route, wgu, wd, *, tm):
    n, d = h2.shape
    return pl.pallas_call(
        _moe_dense_kernel,
        grid=(n // tm, N_EXPERTS),
        in_specs=[pl.BlockSpec((tm, d), lambda i, e: (i, 0)),
                  pl.BlockSpec((tm, ROUTE_LANES), lambda i, e: (i, 0)),
                  pl.BlockSpec((1, d, 2 * EXPERT_HIDDEN), lambda i, e: (e, 0, 0)),
                  pl.BlockSpec((1, EXPERT_HIDDEN, d), lambda i, e: (e, 0, 0))],
        out_specs=pl.BlockSpec((tm, d), lambda i, e: (i, 0)),
        out_shape=jax.ShapeDtypeStruct((n, d), jnp.float32),
        compiler_params=pltpu.CompilerParams(dimension_semantics=("arbitrary", "arbitrary"),
                                             vmem_limit_bytes=VMEM_LIMIT),
        name="experts",
    )(h2, route, wgu, wd)


def _final_kernel(x_ref, moe_ref, mod_ref, w_ref, o_ref):
    x = x_ref[0] + mod_ref[0, 5:6, :] * moe_ref[0]
    o_ref[0] = _rms_rows(x) * w_ref[...]


def _final_call(x_mid, moe, mod, fw, *, tt):
    bsz, t, d = x_mid.shape
    tile = pl.BlockSpec((1, tt, d), lambda b, j: (b, j, 0))
    return pl.pallas_call(
        _final_kernel,
        grid=(bsz, t // tt),
        in_specs=[tile, tile, pl.BlockSpec((1, 6, d), lambda b, j: (b, 0, 0)),
                  pl.BlockSpec((1, d), lambda b, j: (0, 0))],
        out_specs=tile,
        out_shape=jax.ShapeDtypeStruct((bsz, t, d), jnp.float32),
        compiler_params=pltpu.CompilerParams(dimension_semantics=("arbitrary", "arbitrary"),
                                             vmem_limit_bytes=VMEM_LIMIT),
        name="final_norm",
    )(x_mid, moe, mod, fw.reshape(1, d))


def _block_diag_halves(w):
    nb, bd, _ = w.shape
    per = nb // 2
    eye = jnp.eye(per, dtype=w.dtype)
    halves = [jnp.einsum('nij,nm->nimj', w[h * per:(h + 1) * per], eye).reshape(per * bd, per * bd) for h in range(2)]
    return jnp.stack(halves)


def _tile_rows(t):
    for cand in (256, 128, 64, 32, 16, 8):
        if t % cand == 0:
            return cand
    raise ValueError(f"sequence length {t} is not a multiple of 8")


def kernel(x_prompt, x_sample, state_conv, state_lru, state_hgrn, c_prompt, c_sample, w_ada, b_ada, w_in, conv_w,
           conv_b, lru_wa, lru_ba, lru_wx, lru_bx, lru_lambda, hg_lower, hg_norm_w, w_out, w_grp, b_grp, w_rt, b_rt,
           w_gate, w_up, w_down, final_norm_w):
    depth = w_in.shape[0]
    bp, tp, d = x_prompt.shape
    bs, ts, _ = x_sample.shape
    f32 = jnp.float32

    c_all = jnp.concatenate([c_prompt, c_sample], axis=0)
    rows = -(-c_all.shape[0] // SUBLANES) * SUBLANES
    c_all = jnp.pad(c_all, ((0, rows - c_all.shape[0]), (0, 0)))
    mod_all = _mod_call(c_all, w_ada, b_ada).reshape(depth, rows, 6, d)

    p_low = jax.nn.softmax(hg_lower.astype(f32), axis=0)
    lbs = jnp.cumsum(p_low, axis=0) - p_low[0]
    w_in_bf, w_out_bf = _bf(w_in), _bf(w_out)
    wr = jnp.concatenate([w_grp, w_rt], axis=-1)
    wr = jnp.pad(wr, ((0, 0), (0, 0), (0, ROUTE_LANES - wr.shape[-1])))
    wr_hi = _bf(wr)
    wr_lo = _bf(wr - wr_hi.astype(f32))
    br = jnp.concatenate([b_grp, b_rt], axis=-1)
    br = jnp.pad(br, ((0, 0), (0, ROUTE_LANES - br.shape[-1])))
    wgu = _bf(jnp.concatenate([w_gate, w_up], axis=-1))
    wd = _bf(w_down)

    trunks = [
        dict(x=x_prompt, b0=0, nb=bp, t=tp,
             conv=jnp.zeros((depth, bp, CONV_W - 1, LRU_WIDTH), f32), lru=jnp.zeros((depth, bp, LRU_WIDTH), f32),
             hg=jnp.zeros((depth, bp, HG_HEADS, HG_DK, HG_DK), f32)),
        dict(x=x_sample, b0=bp, nb=bs, t=ts, conv=state_conv, lru=state_lru, hg=state_hgrn),
    ]
    results = []
    for tr in trunks:
        t = tr["t"]
        tt = _tile_rows(t)
        chunk = min(64, tt)
        n = tr["nb"] * t
        tm = 512 if n % 512 == 0 else _tile_rows(n)
        x, moe, mod_prev = tr["x"], None, None
        convs, lrus, hgs = [], [], []
        for l in range(depth):
            mod = mod_all[l, tr["b0"]:tr["b0"] + tr["nb"]]
            wts = (w_in_bf[l], conv_w[l], conv_b[l].reshape(1, -1), _bf(_block_diag_halves(lru_wa[l])),
                   _bf(_block_diag_halves(lru_wx[l])), lru_ba[l].reshape(1, -1), lru_bx[l].reshape(1, -1),
                   lru_lambda[l].reshape(1, -1), lbs[l].reshape(1, -1), hg_norm_w[l].reshape(1, -1), w_out_bf[l],
                   wr_hi[l], wr_lo[l], br[l].reshape(1, -1))
            x, h2, route, conv_n, lru_n, hg_n = _mixer_call(
                x, moe, mod_prev, mod, tr["conv"][l], tr["lru"][l], tr["hg"][l], wts, tt=tt, chunk=chunk)
            moe = _moe_dense_call(h2.reshape(n, d), route.reshape(n, ROUTE_LANES), wgu[l], wd[l],
                                  tm=tm).reshape(tr["nb"], t, d)
            mod_prev = mod
            convs.append(conv_n)
            lrus.append(lru_n.reshape(tr["nb"], LRU_WIDTH))
            hgs.append(hg_n)
        y = _final_call(x, moe, mod_prev, final_norm_w, tt=tt)
        results.append((y, jnp.stack(convs), jnp.stack(lrus), jnp.stack(hgs)))
    (yp, cp, lp, hp), (ys, cs, ls, hs) = results
    return (yp, ys, cp, lp, hp, cs, ls, hs)
```

```python
import functools

import jax
import jax.numpy as jnp
from jax import lax
from jax.experimental import pallas as pl
from jax.experimental.pallas import tpu as pltpu

D_MODEL = 1024
LRU_WIDTH = 512
LRU_BLOCKS = 8
LRU_C = 8.0
CONV_W = 4
HG_WIDTH = 512
HG_HEADS = 4
HG_DK = 128
N_GROUPS = 4
EXPERTS_PER_GROUP = 8
N_EXPERTS = 32
EXPERT_HIDDEN = 256
EPS = 1e-6

SUBLANES = 8
LANES = 128
HG_SUB = 16
EXP_CLAMP = 80.0
ROUTE_LANES = LANES
EXPERT_TILE = 256
VMEM_LIMIT = 56 * 1024 * 1024

_NT = (((1,), (1,)), ((), ()))
_TN = (((0,), (0,)), ((), ()))


def _bf(x):
    return x.astype(jnp.bfloat16)


def _dot(a, b):
    return jnp.dot(a, b, preferred_element_type=jnp.float32)


def _sigmoid(x):
    return 1.0 / (1.0 + jnp.exp(-x))


def _silu(x):
    return x * _sigmoid(x)


def _gelu_tanh(x):
    return 0.5 * x * (1.0 + jnp.tanh(0.7978845608028654 * (x + 0.044715 * (x * x * x))))


def _softplus(x):
    return jnp.maximum(x, 0.0) + jnp.log(1.0 + jnp.exp(-jnp.abs(x)))


def _rms_rows(x):
    return x * lax.rsqrt(jnp.mean(x * x, axis=-1, keepdims=True) + EPS)


def _scan8(a, b):
    rows = lax.broadcasted_iota(jnp.int32, a.shape, 0)
    for d in (1, 2, 4):
        m = rows >= d
        a_sh = pltpu.roll(a, d, 0)
        b_sh = pltpu.roll(b, d, 0)
        b = jnp.where(m, a * b_sh + b, b)
        a = jnp.where(m, a * a_sh, a)
    return a, b


def _cumsum8(x):
    rows = lax.broadcasted_iota(jnp.int32, x.shape, 0)
    for d in (1, 2, 4):
        x = x + jnp.where(rows >= d, pltpu.roll(x, d, 0), 0.0)
    return x


def _block_refs(b, m):
    n, w = b.shape
    starts, ends = [], []
    for i in range(n // m):
        if i == 0:
            starts.append(jnp.zeros((m, w), jnp.float32))
        else:
            starts.append(jnp.broadcast_to(b[i * m - 1:i * m, :], (m, w)))
        ends.append(jnp.broadcast_to(b[(i + 1) * m - 1:(i + 1) * m, :], (m, w)))
    if len(starts) == 1:
        return starts[0], ends[0]
    return jnp.concatenate(starts, axis=0), jnp.concatenate(ends, axis=0)


def _mixer_kernel(*refs, tt, chunk, has_prev):
    it = iter(refs)
    x_ref = next(it)
    if has_prev:
        moe_ref = next(it)
        modp_ref = next(it)
    (mod_ref, conv0_ref, lru0_ref, hg0_ref, w_in_ref, conv_w_ref, conv_b_ref, wa_ref, wx_ref, ba_ref, bx_ref,
     lam_ref, lbs_ref, hgn_ref, w_out_ref, wr_hi_ref, wr_lo_ref, br_ref, tri_ref, cnt_in_ref) = (
        next(it) for _ in range(20))
    (xmid_ref, hpk_ref, route_ref, idx_ref, cnt_out_ref, conv_out_ref, lru_out_ref,
     hg_out_ref) = (next(it) for _ in range(8))
    conv_scr, lru_scr, st_scr, cnt_scr = (next(it) for _ in range(4))

    bi = pl.program_id(0)
    j = pl.program_id(1)
    last_b = pl.num_programs(0) - 1
    last_j = pl.num_programs(1) - 1
    pad = SUBLANES - (CONV_W - 1)
    d_half = D_MODEL // 2

    @pl.when((bi == 0) & (j == 0))
    def _():
        cnt_scr[...] = cnt_in_ref[...]

    @pl.when(j == 0)
    def _():
        conv_scr[pad:SUBLANES, :] = conv0_ref[0]
        lru_scr[...] = lru0_ref[0]
        for hd in range(HG_HEADS):
            st_scr[hd] = hg0_ref[0, hd].T

    x = x_ref[0]
    if has_prev:
        x = x + modp_ref[0, 5:6, :] * moe_ref[0]
    mod = mod_ref[0]
    sh1, sc1, g1, sh2, sc2 = (mod[i:i + 1, :] for i in range(5))

    h = _rms_rows(x) * (1.0 + sc1) + sh1
    u = _dot(_bf(h), w_in_ref[...])
    lw, hw = LRU_WIDTH, HG_WIDTH
    xb, gb = u[:, :lw], u[:, lw:2 * lw]
    qh, fh = u[:, 2 * lw:2 * lw + hw], u[:, 2 * lw + hw:2 * lw + 2 * hw]
    ih, gh = u[:, 2 * lw + 2 * hw:2 * lw + 3 * hw], u[:, 2 * lw + 3 * hw:]

    conv_scr[SUBLANES:SUBLANES + tt, :] = xb
    xc = conv_b_ref[...]
    for k in range(CONV_W):
        xc = xc + conv_w_ref[k:k + 1, :] * conv_scr[pl.ds(pad + k, tt), :]
    new_hist = conv_scr[pl.ds(tt + pad, CONV_W - 1), :]
    conv_scr[pad:SUBLANES, :] = new_hist

    xc_bf = _bf(xc)
    half = lw // 2
    r_pre = jnp.concatenate([_dot(xc_bf[:, i * half:(i + 1) * half], wa_ref[i]) for i in range(2)], axis=1)
    i_pre = jnp.concatenate([_dot(xc_bf[:, i * half:(i + 1) * half], wx_ref[i]) for i in range(2)], axis=1)
    r = _sigmoid(r_pre + ba_ref[...])
    ig = _sigmoid(i_pre + bx_ref[...])
    log_a = (-LRU_C) * r * _softplus(-lam_ref[...])
    a = jnp.exp(log_a)
    th = jnp.tanh(log_a)
    one_minus_a2 = (-2.0) * th / (1.0 - th)
    bt = jnp.sqrt(one_minus_a2) * ig * xc

    carry = lru_scr[...]
    hs = []
    for g in range(tt // SUBLANES):
        sl = slice(g * SUBLANES, (g + 1) * SUBLANES)
        acum, hloc = _scan8(a[sl], bt[sl])
        hg = acum * carry + hloc
        hs.append(hg)
        carry = hg[SUBLANES - 1:SUBLANES, :]
    lru_scr[...] = carry
    hl = jnp.concatenate(hs, axis=0)
    y_lru = hl * _gelu_tanh(gb)

    lbs = lbs_ref[...]
    q = _silu(qh) * (HG_DK ** -0.5)
    f = lbs + (1.0 - lbs) * _sigmoid(fh)
    kk = 1.0 - f
    glog = jnp.log(f)

    ti = lax.broadcasted_iota(jnp.int32, (chunk, chunk), 0)
    si = lax.broadcasted_iota(jnp.int32, (chunk, chunk), 1)
    sub_shift = HG_SUB.bit_length() - 1
    mask_diag = ((ti >> sub_shift) == (si >> sub_shift)) & (ti >= si)
    levels = []
    m = HG_SUB
    while m < chunk:
        sh = m.bit_length() - 1
        levels.append((m, ((ti >> (sh + 1)) == (si >> (sh + 1))) & (((ti >> sh) & 1) == 1) & (((si >> sh) & 1) == 0)))
        m *= 2

    o_chunks = []
    for c in range(tt // chunk):
        rs = slice(c * chunk, (c + 1) * chunk)
        gc = glog[rs]
        bs, bcarry = [], None
        for g in range(chunk // SUBLANES):
            cs = _cumsum8(gc[g * SUBLANES:(g + 1) * SUBLANES])
            if bcarry is not None:
                cs = cs + bcarry
            bs.append(cs)
            bcarry = cs[SUBLANES - 1:SUBLANES, :]
        b = jnp.concatenate(bs, axis=0)
        b_last = bcarry
        qc, kc, vc = q[rs], kk[rs], ih[rs]

        st16, en16 = _block_refs(b, HG_SUB)
        q_lv = {HG_SUB: _bf(qc * jnp.exp(b - st16))}
        k_lv = {HG_SUB: _bf(kc * jnp.exp(en16 - b))}
        k_diag = _bf(kc * jnp.exp(jnp.minimum(st16 - b, EXP_CLAMP)))
        for m, _ in levels:
            if m == HG_SUB:
                continue
            st, en = _block_refs(b, m)
            q_lv[m] = _bf(qc * jnp.exp(b - st))
            k_lv[m] = _bf(kc * jnp.exp(en - b))
        q_all = _bf(qc * jnp.exp(b))
        k_all = _bf(kc * jnp.exp(b_last - b))
        v_bf = _bf(vc)
        eb_last = jnp.exp(b_last)

        o_heads = []
        for hd in range(HG_HEADS):
            hs_ = slice(hd * HG_DK, (hd + 1) * HG_DK)
            amat = jnp.where(mask_diag,
                             lax.dot_general(q_lv[HG_SUB][:, hs_], k_diag[:, hs_], _NT,
                                             preferred_element_type=jnp.float32), 0.0)
            for m, msk in levels:
                amat = amat + jnp.where(msk, lax.dot_general(q_lv[m][:, hs_], k_lv[m][:, hs_], _NT,
                                                             preferred_element_type=jnp.float32), 0.0)
            st_t = st_scr[hd]
            o_h = lax.dot_general(q_all[:, hs_], _bf(st_t), _NT, preferred_element_type=jnp.float32)
            o_h = o_h + _dot(_bf(amat), v_bf[:, hs_])
            st_scr[hd] = st_t * eb_last[:, hs_] + lax.dot_general(v_bf[:, hs_], k_all[:, hs_], _TN,
                                                                  preferred_element_type=jnp.float32)
            o_heads.append(_rms_rows(o_h) * hgn_ref[...])
        o_chunks.append(jnp.concatenate(o_heads, axis=1))
    o_all = o_chunks[0] if len(o_chunks) == 1 else jnp.concatenate(o_chunks, axis=0)
    y_hg = o_all * _silu(gh)

    mix = _dot(_bf(jnp.concatenate([y_lru, y_hg], axis=1)), w_out_ref[...])
    x_mid = x + g1 * mix
    xmid_ref[0] = x_mid

    h2 = _rms_rows(x_mid) * (1.0 + sc2) + sh2
    h2_hi = _bf(h2)
    h2_lo = _bf(h2 - h2_hi.astype(jnp.float32))
    hbits = lax.bitcast_convert_type(h2_hi.astype(jnp.float32), jnp.uint32)
    hpk_ref[0] = hbits[:, :d_half] | (hbits[:, d_half:] >> 16)
    logits = (_dot(h2_hi, wr_hi_ref[...]) + _dot(h2_hi, wr_lo_ref[...]) + _dot(h2_lo, wr_hi_ref[...])
              + br_ref[...])

    lane = lax.broadcasted_iota(jnp.int32, (tt, ROUTE_LANES), 1)
    lane_f = lane.astype(jnp.float32)
    neg, big = -1e30, 1e6
    is_grp = lane < N_GROUPS
    gl = jnp.where(is_grp, logits, neg)
    gmax = jnp.max(gl, axis=-1, keepdims=True)
    gsel = jnp.min(jnp.where(gl == gmax, lane_f, big), axis=-1, keepdims=True)
    gsum = jnp.sum(jnp.where(is_grp, jnp.exp(gl - gmax), 0.0), axis=-1, keepdims=True)
    p_grp = 1.0 / gsum
    lo = N_GROUPS + gsel * EXPERTS_PER_GROUP
    emask = (lane_f >= lo) & (lane_f < lo + EXPERTS_PER_GROUP)
    el = jnp.where(emask, logits, neg)
    tv1 = jnp.max(el, axis=-1, keepdims=True)
    ti1 = jnp.min(jnp.where(emask & (el == tv1), lane_f, big), axis=-1, keepdims=True)
    emask2 = emask & (lane_f != ti1)
    el2 = jnp.where(emask2, logits, neg)
    tv2 = jnp.max(el2, axis=-1, keepdims=True)
    ti2 = jnp.min(jnp.where(emask2 & (el2 == tv2), lane_f, big), axis=-1, keepdims=True)
    e21 = jnp.exp(tv2 - tv1)
    w1 = p_grp / (1.0 + e21)
    w2 = w1 * e21
    hot1 = lane_f == ti1
    hot2 = lane_f == ti2
    hot12 = jnp.where(hot1 | hot2, 1.0, 0.0)
    before = _dot(tri_ref[...], _bf(hot12)) + cnt_scr[...]
    rank1 = jnp.sum(jnp.where(hot1, before, 0.0), axis=-1, keepdims=True)
    rank2 = jnp.sum(jnp.where(hot2, before, 0.0), axis=-1, keepdims=True)
    cnt_scr[...] = before[tt - 1:tt, :] + hot12[tt - 1:tt, :]

    eid1 = ti1 - N_GROUPS
    eid2 = ti2 - N_GROUPS
    route_ref[0] = jnp.where(lane == 0, w1, jnp.where(lane == 1, w2, 0.0))
    rec = jnp.where(lane == 0, eid1, jnp.where(lane == 1, eid2, jnp.where(lane == 2, rank1,
                                                                          jnp.where(lane == 3, rank2, 0.0))))
    if tt < LANES:
        rec = jnp.concatenate([rec, jnp.zeros((LANES - tt, ROUTE_LANES), jnp.float32)], axis=0)
    idx_ref[0] = rec.T[:SUBLANES, :tt].astype(jnp.int32)

    @pl.when(j == last_j)
    def _():
        conv_out_ref[0] = new_hist
        lru_out_ref[0] = carry
        for hd in range(HG_HEADS):
            hg_out_ref[0, hd] = st_scr[hd].T

    @pl.when((bi == last_b) & (j == last_j))
    def _():
        cnt_out_ref[...] = cnt_scr[...]


def _mixer_call(x, moe_prev, mod_prev, mod, conv0, lru0, hg0, wts, cnt_in, *, tt, chunk):
    bsz, t, d = x.shape
    has_prev = moe_prev is not None
    grid = (bsz, t // tt)
    tile = lambda last: pl.BlockSpec((1, tt, last), lambda b, j: (b, j, 0))
    per_b = lambda *shape: pl.BlockSpec((1,) + shape, lambda b, j: (b,) + (0,) * len(shape))
    full = lambda a: pl.BlockSpec(a.shape, lambda b, j: (0,) * a.ndim)
    tri = _bf(jnp.tril(jnp.ones((tt, tt), jnp.float32), -1))

    args, in_specs = [x], [tile(d)]
    if has_prev:
        args += [moe_prev, mod_prev]
        in_specs += [tile(d), per_b(6, d)]
    args += [mod, conv0, lru0.reshape(bsz, 1, LRU_WIDTH), hg0]
    in_specs += [per_b(6, d), per_b(CONV_W - 1, LRU_WIDTH), per_b(1, LRU_WIDTH), per_b(HG_HEADS, HG_DK, HG_DK)]
    args += list(wts) + [tri, cnt_in]
    in_specs += [full(w) for w in wts] + [full(tri), full(cnt_in)]

    out_shape = (
        jax.ShapeDtypeStruct((bsz, t, d), jnp.float32),
        jax.ShapeDtypeStruct((bsz, t, d // 2), jnp.uint32),
        jax.ShapeDtypeStruct((bsz, t, ROUTE_LANES), jnp.float32),
        jax.ShapeDtypeStruct((bsz, SUBLANES, t), jnp.int32),
        jax.ShapeDtypeStruct((1, ROUTE_LANES), jnp.float32),
        jax.ShapeDtypeStruct((bsz, CONV_W - 1, LRU_WIDTH), jnp.float32),
        jax.ShapeDtypeStruct((bsz, 1, LRU_WIDTH), jnp.float32),
        jax.ShapeDtypeStruct((bsz, HG_HEADS, HG_DK, HG_DK), jnp.float32),
    )
    out_specs = (tile(d), tile(d // 2), tile(ROUTE_LANES),
                 pl.BlockSpec((1, SUBLANES, tt), lambda b, j: (b, 0, j)),
                 pl.BlockSpec((1, ROUTE_LANES), lambda b, j: (0, 0)),
                 per_b(CONV_W - 1, LRU_WIDTH), per_b(1, LRU_WIDTH), per_b(HG_HEADS, HG_DK, HG_DK))
    scratch = [
        pltpu.VMEM((tt + SUBLANES, LRU_WIDTH), jnp.float32),
        pltpu.VMEM((1, LRU_WIDTH), jnp.float32),
        pltpu.VMEM((HG_HEADS, HG_DK, HG_DK), jnp.float32),
        pltpu.VMEM((1, ROUTE_LANES), jnp.float32),
    ]
    return pl.pallas_call(
        functools.partial(_mixer_kernel, tt=tt, chunk=chunk, has_prev=has_prev),
        grid=grid, in_specs=in_specs, out_specs=out_specs, out_shape=out_shape, scratch_shapes=scratch,
        compiler_params=pltpu.CompilerParams(dimension_semantics=("arbitrary", "arbitrary"),
                                             vmem_limit_bytes=VMEM_LIMIT),
        name="mixer",
    )(*args)


def _mod_kernel(c_ref, w_ref, b_ref, o_ref):
    s = _silu(c_ref[...])
    o_ref[0] = jnp.dot(s, w_ref[0], preferred_element_type=jnp.float32,
                       precision=lax.Precision.HIGHEST) + b_ref[0]


def _mod_call(c_all, w_ada, b_ada):
    depth, d, n = w_ada.shape
    rows = c_all.shape[0]
    tn = d
    return pl.pallas_call(
        _mod_kernel,
        grid=(depth, n // tn),
        in_specs=[pl.BlockSpec((rows, d), lambda l, i: (0, 0)),
                  pl.BlockSpec((1, d, tn), lambda l, i: (l, 0, i)),
                  pl.BlockSpec((1, 1, tn), lambda l, i: (l, 0, i))],
        out_specs=pl.BlockSpec((1, rows, tn), lambda l, i: (l, 0, i)),
        out_shape=jax.ShapeDtypeStruct((depth, rows, n), jnp.float32),
        compiler_params=pltpu.CompilerParams(dimension_semantics=("arbitrary", "arbitrary"),
                                             vmem_limit_bytes=VMEM_LIMIT),
        name="modulation",
    )(c_all, w_ada, b_ada.reshape(depth, 1, n))


def _sorted_row(off_ref, idx_ref, k, t):
    return off_ref[idx_ref[0, k, t]] + idx_ref[0, 2 + k, t]


def _dispatch_kernel(off_ref, h_ref, idx_ref, hs_in_ref, hs_ref, sem, *, tt):
    del hs_in_ref

    def row_copy(t, p):
        return pltpu.make_async_copy(h_ref.at[0, pl.ds(t, 1), :], hs_ref.at[pl.ds(p, 1), :], sem)

    def issue(t, c):
        for k in range(2):
            row_copy(t, _sorted_row(off_ref, idx_ref, k, t)).start()
        return c

    def drain(t, c):
        for k in range(2):
            row_copy(0, 0).wait()
        return c

    lax.fori_loop(0, tt, issue, 0, unroll=8)
    lax.fori_loop(0, tt, drain, 0, unroll=8)


def _dispatch_call(off, hpk, idx, hs, *, tt):
    bsz, t, dh = hpk.shape
    grid_spec = pltpu.PrefetchScalarGridSpec(
        num_scalar_prefetch=1, grid=(bsz, t // tt),
        in_specs=[pl.BlockSpec((1, tt, dh), lambda b, j, off: (b, j, 0)),
                  pl.BlockSpec((1, SUBLANES, tt), lambda b, j, off: (b, 0, j), memory_space=pltpu.SMEM),
                  pl.BlockSpec(memory_space=pl.ANY)],
        out_specs=pl.BlockSpec(memory_space=pl.ANY),
        scratch_shapes=[pltpu.SemaphoreType.DMA(())])
    return pl.pallas_call(
        functools.partial(_dispatch_kernel, tt=tt),
        grid_spec=grid_spec,
        out_shape=jax.ShapeDtypeStruct(hs.shape, hs.dtype),
        input_output_aliases={3: 0},
        compiler_params=pltpu.CompilerParams(dimension_semantics=("arbitrary", "arbitrary"),
                                             vmem_limit_bytes=VMEM_LIMIT),
        name="dispatch",
    )(off, hpk, idx, hs)


def _expert_kernel(te_ref, nt_ref, hs_ref, wgu_ref, wd_ref, ys_ref):
    del te_ref
    i = pl.program_id(0)

    @pl.when(i < nt_ref[0])
    def _():
        w = hs_ref[...]
        dh = w.shape[1]
        a = _bf(lax.bitcast_convert_type(w & jnp.uint32(0xFFFF0000), jnp.float32))
        b = _bf(lax.bitcast_convert_type(w << 16, jnp.float32))
        gu = _dot(a, wgu_ref[0, :dh, :]) + _dot(b, wgu_ref[0, dh:, :])
        hid = _silu(gu[:, :EXPERT_HIDDEN]) * gu[:, EXPERT_HIDDEN:]
        ys_ref[...] = _dot(_bf(hid), wd_ref[0])

    @pl.when(i >= nt_ref[0])
    def _():
        ys_ref[...] = jnp.zeros_like(ys_ref)


def _expert_call(tile_expert, n_tiles, hs, wgu, wd, *, te_rows):
    npad, dh = hs.shape
    d = wd.shape[-1]
    grid_spec = pltpu.PrefetchScalarGridSpec(
        num_scalar_prefetch=2, grid=(npad // te_rows,),
        in_specs=[pl.BlockSpec((te_rows, dh), lambda i, te, nt: (i, 0)),
                  pl.BlockSpec((1, 2 * dh, 2 * EXPERT_HIDDEN), lambda i, te, nt: (te[i], 0, 0)),
                  pl.BlockSpec((1, EXPERT_HIDDEN, d), lambda i, te, nt: (te[i], 0, 0))],
        out_specs=pl.BlockSpec((te_rows, d), lambda i, te, nt: (i, 0)))
    return pl.pallas_call(
        _expert_kernel,
        grid_spec=grid_spec,
        out_shape=jax.ShapeDtypeStruct((npad, d), jnp.float32),
        compiler_params=pltpu.CompilerParams(dimension_semantics=("arbitrary",), vmem_limit_bytes=VMEM_LIMIT),
        name="experts",
    )(tile_expert, n_tiles, hs, wgu, wd)


def _combine_kernel(off_ref, idx_ref, route_ref, ys_ref, o_ref, y1, y2, sem, *, tt):
    def row_copy(dst, t, p):
        return pltpu.make_async_copy(ys_ref.at[pl.ds(p, 1), :], dst.at[pl.ds(t, 1), :], sem)

    def issue(t, c):
        for k, dst in enumerate((y1, y2)):
            row_copy(dst, t, _sorted_row(off_ref, idx_ref, k, t)).start()
        return c

    def drain(t, c):
        for dst in (y1, y2):
            row_copy(dst, 0, 0).wait()
        return c

    lax.fori_loop(0, tt, issue, 0, unroll=8)
    lax.fori_loop(0, tt, drain, 0, unroll=8)
    rt = route_ref[0]
    o_ref[0] = rt[:, 0:1] * y1[...] + rt[:, 1:2] * y2[...]


def _combine_call(off, idx, route, ys, *, tt):
    bsz, _, t = idx.shape
    d = ys.shape[-1]
    grid_spec = pltpu.PrefetchScalarGridSpec(
        num_scalar_prefetch=1, grid=(bsz, t // tt),
        in_specs=[pl.BlockSpec((1, SUBLANES, tt), lambda b, j, off: (b, 0, j), memory_space=pltpu.SMEM),
                  pl.BlockSpec((1, tt, ROUTE_LANES), lambda b, j, off: (b, j, 0)),
                  pl.BlockSpec(memory_space=pl.ANY)],
        out_specs=pl.BlockSpec((1, tt, d), lambda b, j, off: (b, j, 0)),
        scratch_shapes=[pltpu.VMEM((tt, d), jnp.float32), pltpu.VMEM((tt, d), jnp.float32),
                        pltpu.SemaphoreType.DMA(())])
    return pl.pallas_call(
        functools.partial(_combine_kernel, tt=tt),
        grid_spec=grid_spec,
        out_shape=jax.ShapeDtypeStruct((bsz, t, d), jnp.float32),
        compiler_params=pltpu.CompilerParams(dimension_semantics=("arbitrary", "arbitrary"),
                                             vmem_limit_bytes=VMEM_LIMIT),
        name="combine",
    )(off, idx, route, ys)


def _final_kernel(x_ref, moe_ref, mod_ref, w_ref, o_ref):
    x = x_ref[0] + mod_ref[0, 5:6, :] * moe_ref[0]
    o_ref[0] = _rms_rows(x) * w_ref[...]


def _final_call(x_mid, moe, mod, fw, *, tt):
    bsz, t, d = x_mid.shape
    tile = pl.BlockSpec((1, tt, d), lambda b, j: (b, j, 0))
    return pl.pallas_call(
        _final_kernel,
        grid=(bsz, t // tt),
        in_specs=[tile, tile, pl.BlockSpec((1, 6, d), lambda b, j: (b, 0, 0)),
                  pl.BlockSpec((1, d), lambda b, j: (0, 0))],
        out_specs=tile,
        out_shape=jax.ShapeDtypeStruct((bsz, t, d), jnp.float32),
        compiler_params=pltpu.CompilerParams(dimension_semantics=("arbitrary", "arbitrary"),
                                             vmem_limit_bytes=VMEM_LIMIT),
        name="final_norm",
    )(x_mid, moe, mod, fw.reshape(1, d))


def _block_diag_halves(w):
    nb, bd, _ = w.shape
    per = nb // 2
    eye = jnp.eye(per, dtype=w.dtype)
    halves = [jnp.einsum('nij,nm->nimj', w[h * per:(h + 1) * per], eye).reshape(per * bd, per * bd) for h in range(2)]
    return jnp.stack(halves)


def _tile_rows(t):
    for cand in (256, 128, 64, 32, 16, 8):
        if t % cand == 0:
            return cand
    raise ValueError(f"sequence length {t} is not a multiple of 8")


def kernel(x_prompt, x_sample, state_conv, state_lru, state_hgrn, c_prompt, c_sample, w_ada, b_ada, w_in, conv_w,
           conv_b, lru_wa, lru_ba, lru_wx, lru_bx, lru_lambda, hg_lower, hg_norm_w, w_out, w_grp, b_grp, w_rt, b_rt,
           w_gate, w_up, w_down, final_norm_w):
    depth = w_in.shape[0]
    bp, tp, d = x_prompt.shape
    bs, ts, _ = x_sample.shape
    f32 = jnp.float32

    c_all = jnp.concatenate([c_prompt, c_sample], axis=0)
    rows = -(-c_all.shape[0] // SUBLANES) * SUBLANES
    c_all = jnp.pad(c_all, ((0, rows - c_all.shape[0]), (0, 0)))
    mod_all = _mod_call(c_all, w_ada, b_ada).reshape(depth, rows, 6, d)

    p_low = jax.nn.softmax(hg_lower.astype(f32), axis=0)
    lbs = jnp.cumsum(p_low, axis=0) - p_low[0]
    w_in_bf, w_out_bf = _bf(w_in), _bf(w_out)
    wr = jnp.concatenate([w_grp, w_rt], axis=-1)
    wr = jnp.pad(wr, ((0, 0), (0, 0), (0, ROUTE_LANES - wr.shape[-1])))
    wr_hi = _bf(wr)
    wr_lo = _bf(wr - wr_hi.astype(f32))
    br = jnp.concatenate([b_grp, b_rt], axis=-1)
    br = jnp.pad(br, ((0, 0), (0, ROUTE_LANES - br.shape[-1])))
    wgu = _bf(jnp.concatenate([w_gate, w_up], axis=-1))
    wd = _bf(w_down)

    trunks = [
        dict(x=x_prompt, b0=0, nb=bp, t=tp,
             conv=jnp.zeros((depth, bp, CONV_W - 1, LRU_WIDTH), f32), lru=jnp.zeros((depth, bp, LRU_WIDTH), f32),
             hg=jnp.zeros((depth, bp, HG_HEADS, HG_DK, HG_DK), f32)),
        dict(x=x_sample, b0=bp, nb=bs, t=ts, conv=state_conv, lru=state_lru, hg=state_hgrn),
    ]
    for tr in trunks:
        tr["tt"] = _tile_rows(tr["t"])
        tr["chunk"] = min(64, tr["tt"])
        tr.update(moe=None, mod_prev=None, convs=[], lrus=[], hgs=[])

    n_rows = 2 * sum(tr["nb"] * tr["t"] for tr in trunks)
    max_tiles = (n_rows + N_EXPERTS * (EXPERT_TILE - 1)) // EXPERT_TILE
    hs_zero = jnp.zeros((max_tiles * EXPERT_TILE, d // 2), jnp.uint32)
    tile_ids = jnp.arange(max_tiles, dtype=jnp.int32)

    for l in range(depth):
        wts = (w_in_bf[l], conv_w[l], conv_b[l].reshape(1, -1), _bf(_block_diag_halves(lru_wa[l])),
               _bf(_block_diag_halves(lru_wx[l])), lru_ba[l].reshape(1, -1), lru_bx[l].reshape(1, -1),
               lru_lambda[l].reshape(1, -1), lbs[l].reshape(1, -1), hg_norm_w[l].reshape(1, -1), w_out_bf[l],
               wr_hi[l], wr_lo[l], br[l].reshape(1, -1))
        cnt = jnp.zeros((1, ROUTE_LANES), f32)
        for tr in trunks:
            mod = mod_all[l, tr["b0"]:tr["b0"] + tr["nb"]]
            tr["x"], tr["hpk"], tr["route"], tr["idx"], cnt, conv_n, lru_n, hg_n = _mixer_call(
                tr["x"], tr["moe"], tr["mod_prev"], mod, tr["conv"][l], tr["lru"][l], tr["hg"][l], wts, cnt,
                tt=tr["tt"], chunk=tr["chunk"])
            tr["mod_prev"] = mod
            tr["convs"].append(conv_n)
            tr["lrus"].append(lru_n.reshape(tr["nb"], LRU_WIDTH))
            tr["hgs"].append(hg_n)

        counts = cnt[0, N_GROUPS:N_GROUPS + N_EXPERTS].astype(jnp.int32)
        tiles_e = (counts + (EXPERT_TILE - 1)) // EXPERT_TILE
        ends = jnp.cumsum(tiles_e)
        off = (ends - tiles_e) * EXPERT_TILE
        n_tiles = ends[-1:]
        tile_expert = jnp.sum(jnp.minimum(tile_ids, n_tiles - 1)[:, None] >= ends[None, :], axis=1).astype(jnp.int32)

        hs = hs_zero
        for tr in trunks:
            hs = _dispatch_call(off, tr["hpk"], tr["idx"], hs, tt=tr["tt"])
        ys = _expert_call(tile_expert, n_tiles, hs, wgu[l], wd[l], te_rows=EXPERT_TILE)
        for tr in trunks:
            tr["moe"] = _combine_call(off, tr["idx"], tr["route"], ys, tt=tr["tt"])

    results = []
    for tr in trunks:
        y = _final_call(tr["x"], tr["moe"], tr["mod_prev"], final_norm_w, tt=tr["tt"])
        results.append((y, jnp.stack(tr["convs"]), jnp.stack(tr["lrus"]), jnp.stack(tr["hgs"])))
    (yp, cp, lp, hp), (ys_, cs, ls, hs_) = results
    return (yp, ys_, cp, lp, hp, cs, ls, hs_)
```

```python
import functools

import jax
import jax.numpy as jnp
from jax import lax
from jax.experimental import pallas as pl
from jax.experimental.pallas import tpu as pltpu

D_MODEL = 1024
LRU_WIDTH = 512
LRU_BLOCKS = 8
LRU_C = 8.0
CONV_W = 4
HG_WIDTH = 512
HG_HEADS = 4
HG_DK = 128
N_GROUPS = 4
EXPERTS_PER_GROUP = 8
N_EXPERTS = 32
EXPERT_HIDDEN = 256
EPS = 1e-6

SUBLANES = 8
LANES = 128
HG_SUB = 16
EXP_CLAMP = 80.0
ROUTE_LANES = LANES
EXPERT_TILE = 256
VMEM_LIMIT = 56 * 1024 * 1024

_NT = (((1,), (1,)), ((), ()))
_TN = (((0,), (0,)), ((), ()))


def _bf(x):
    return x.astype(jnp.bfloat16)


def _dot(a, b):
    return jnp.dot(a, b, preferred_element_type=jnp.float32)


def _sigmoid(x):
    return 1.0 / (1.0 + jnp.exp(-x))


def _silu(x):
    return x * _sigmoid(x)


def _gelu_tanh(x):
    return 0.5 * x * (1.0 + jnp.tanh(0.7978845608028654 * (x + 0.044715 * (x * x * x))))


def _softplus(x):
    return jnp.maximum(x, 0.0) + jnp.log(1.0 + jnp.exp(-jnp.abs(x)))


def _rms_rows(x):
    return x * lax.rsqrt(jnp.mean(x * x, axis=-1, keepdims=True) + EPS)


def _scan8(a, b):
    rows = lax.broadcasted_iota(jnp.int32, a.shape, 0)
    for d in (1, 2, 4):
        m = rows >= d
        a_sh = pltpu.roll(a, d, 0)
        b_sh = pltpu.roll(b, d, 0)
        b = jnp.where(m, a * b_sh + b, b)
        a = jnp.where(m, a * a_sh, a)
    return a, b


def _cumsum8(x):
    rows = lax.broadcasted_iota(jnp.int32, x.shape, 0)
    for d in (1, 2, 4):
        x = x + jnp.where(rows >= d, pltpu.roll(x, d, 0), 0.0)
    return x


def _block_refs(b, m):
    n, w = b.shape
    starts, ends = [], []
    for i in range(n // m):
        if i == 0:
            starts.append(jnp.zeros((m, w), jnp.float32))
        else:
            starts.append(jnp.broadcast_to(b[i * m - 1:i * m, :], (m, w)))
        ends.append(jnp.broadcast_to(b[(i + 1) * m - 1:(i + 1) * m, :], (m, w)))
    if len(starts) == 1:
        return starts[0], ends[0]
    return jnp.concatenate(starts, axis=0), jnp.concatenate(ends, axis=0)


def _mixer_kernel(*refs, tt, chunk, has_prev):
    it = iter(refs)
    x_ref = next(it)
    if has_prev:
        moe_ref = next(it)
        modp_ref = next(it)
    (mod_ref, conv0_ref, lru0_ref, hg0_ref, w_in_ref, conv_w_ref, conv_b_ref, wa_ref, wx_ref, ba_ref, bx_ref,
     lam_ref, lbs_ref, hgn_ref, w_out_ref, wr_hi_ref, wr_lo_ref, br_ref, tri_ref, upper_ref, cnt_in_ref) = (
        next(it) for _ in range(21))
    (xmid_ref, hloc_ref, route_ref, tab_ref, cnt_out_ref, conv_out_ref, lru_out_ref,
     hg_out_ref) = (next(it) for _ in range(8))
    conv_scr, lru_scr, st_scr, cnt_scr = (next(it) for _ in range(4))

    bi = pl.program_id(0)
    j = pl.program_id(1)
    last_b = pl.num_programs(0) - 1
    last_j = pl.num_programs(1) - 1
    pad = SUBLANES - (CONV_W - 1)
    d_half = D_MODEL // 2

    @pl.when((bi == 0) & (j == 0))
    def _():
        cnt_scr[...] = cnt_in_ref[...]

    @pl.when(j == 0)
    def _():
        conv_scr[pad:SUBLANES, :] = conv0_ref[0]
        lru_scr[...] = lru0_ref[0]
        for hd in range(HG_HEADS):
            st_scr[hd] = hg0_ref[0, hd].T

    x = x_ref[0]
    if has_prev:
        x = x + modp_ref[0, 5:6, :] * moe_ref[0]
    mod = mod_ref[0]
    sh1, sc1, g1, sh2, sc2 = (mod[i:i + 1, :] for i in range(5))

    h = _rms_rows(x) * (1.0 + sc1) + sh1
    u = _dot(_bf(h), w_in_ref[...])
    lw, hw = LRU_WIDTH, HG_WIDTH
    xb, gb = u[:, :lw], u[:, lw:2 * lw]
    qh, fh = u[:, 2 * lw:2 * lw + hw], u[:, 2 * lw + hw:2 * lw + 2 * hw]
    ih, gh = u[:, 2 * lw + 2 * hw:2 * lw + 3 * hw], u[:, 2 * lw + 3 * hw:]

    conv_scr[SUBLANES:SUBLANES + tt, :] = xb
    xc = conv_b_ref[...]
    for k in range(CONV_W):
        xc = xc + conv_w_ref[k:k + 1, :] * conv_scr[pl.ds(pad + k, tt), :]
    new_hist = conv_scr[pl.ds(tt + pad, CONV_W - 1), :]
    conv_scr[pad:SUBLANES, :] = new_hist

    xc_bf = _bf(xc)
    half = lw // 2
    r_pre = jnp.concatenate([_dot(xc_bf[:, i * half:(i + 1) * half], wa_ref[i]) for i in range(2)], axis=1)
    i_pre = jnp.concatenate([_dot(xc_bf[:, i * half:(i + 1) * half], wx_ref[i]) for i in range(2)], axis=1)
    r = _sigmoid(r_pre + ba_ref[...])
    ig = _sigmoid(i_pre + bx_ref[...])
    log_a = (-LRU_C) * r * _softplus(-lam_ref[...])
    a = jnp.exp(log_a)
    th = jnp.tanh(log_a)
    one_minus_a2 = (-2.0) * th / (1.0 - th)
    bt = jnp.sqrt(one_minus_a2) * ig * xc

    carry = lru_scr[...]
    hs = []
    for g in range(tt // SUBLANES):
        sl = slice(g * SUBLANES, (g + 1) * SUBLANES)
        acum, hloc = _scan8(a[sl], bt[sl])
        hg = acum * carry + hloc
        hs.append(hg)
        carry = hg[SUBLANES - 1:SUBLANES, :]
    lru_scr[...] = carry
    hl = jnp.concatenate(hs, axis=0)
    y_lru = hl * _gelu_tanh(gb)

    lbs = lbs_ref[...]
    q = _silu(qh) * (HG_DK ** -0.5)
    f = lbs + (1.0 - lbs) * _sigmoid(fh)
    kk = 1.0 - f
    glog = jnp.log(f)

    ti = lax.broadcasted_iota(jnp.int32, (chunk, chunk), 0)
    si = lax.broadcasted_iota(jnp.int32, (chunk, chunk), 1)
    sub_shift = HG_SUB.bit_length() - 1
    mask_diag = ((ti >> sub_shift) == (si >> sub_shift)) & (ti >= si)
    levels = []
    m = HG_SUB
    while m < chunk:
        sh = m.bit_length() - 1
        levels.append((m, ((ti >> (sh + 1)) == (si >> (sh + 1))) & (((ti >> sh) & 1) == 1) & (((si >> sh) & 1) == 0)))
        m *= 2

    o_chunks = []
    for c in range(tt // chunk):
        rs = slice(c * chunk, (c + 1) * chunk)
        gc = glog[rs]
        bs, bcarry = [], None
        for g in range(chunk // SUBLANES):
            cs = _cumsum8(gc[g * SUBLANES:(g + 1) * SUBLANES])
            if bcarry is not None:
                cs = cs + bcarry
            bs.append(cs)
            bcarry = cs[SUBLANES - 1:SUBLANES, :]
        b = jnp.concatenate(bs, axis=0)
        b_last = bcarry
        qc, kc, vc = q[rs], kk[rs], ih[rs]

        st16, en16 = _block_refs(b, HG_SUB)
        q_lv = {HG_SUB: _bf(qc * jnp.exp(b - st16))}
        k_lv = {HG_SUB: _bf(kc * jnp.exp(en16 - b))}
        k_diag = _bf(kc * jnp.exp(jnp.minimum(st16 - b, EXP_CLAMP)))
        for m, _ in levels:
            if m == HG_SUB:
                continue
            st, en = _block_refs(b, m)
            q_lv[m] = _bf(qc * jnp.exp(b - st))
            k_lv[m] = _bf(kc * jnp.exp(en - b))
        q_all = _bf(qc * jnp.exp(b))
        k_all = _bf(kc * jnp.exp(b_last - b))
        v_bf = _bf(vc)
        eb_last = jnp.exp(b_last)

        o_heads = []
        for hd in range(HG_HEADS):
            hs_ = slice(hd * HG_DK, (hd + 1) * HG_DK)
            amat = jnp.where(mask_diag,
                             lax.dot_general(q_lv[HG_SUB][:, hs_], k_diag[:, hs_], _NT,
                                             preferred_element_type=jnp.float32), 0.0)
            for m, msk in levels:
                amat = amat + jnp.where(msk, lax.dot_general(q_lv[m][:, hs_], k_lv[m][:, hs_], _NT,
                                                             preferred_element_type=jnp.float32), 0.0)
            st_t = st_scr[hd]
            o_h = lax.dot_general(q_all[:, hs_], _bf(st_t), _NT, preferred_element_type=jnp.float32)
            o_h = o_h + _dot(_bf(amat), v_bf[:, hs_])
            st_scr[hd] = st_t * eb_last[:, hs_] + lax.dot_general(v_bf[:, hs_], k_all[:, hs_], _TN,
                                                                  preferred_element_type=jnp.float32)
            o_heads.append(_rms_rows(o_h) * hgn_ref[...])
        o_chunks.append(jnp.concatenate(o_heads, axis=1))
    o_all = o_chunks[0] if len(o_chunks) == 1 else jnp.concatenate(o_chunks, axis=0)
    y_hg = o_all * _silu(gh)

    mix = _dot(_bf(jnp.concatenate([y_lru, y_hg], axis=1)), w_out_ref[...])
    x_mid = x + g1 * mix
    xmid_ref[0] = x_mid

    h2 = _rms_rows(x_mid) * (1.0 + sc2) + sh2
    h2_hi = _bf(h2)
    h2_lo = _bf(h2 - h2_hi.astype(jnp.float32))
    logits = (_dot(h2_hi, wr_hi_ref[...]) + _dot(h2_hi, wr_lo_ref[...]) + _dot(h2_lo, wr_hi_ref[...])
              + br_ref[...])

    lane = lax.broadcasted_iota(jnp.int32, (tt, ROUTE_LANES), 1)
    lane_f = lane.astype(jnp.float32)
    neg, big = -1e30, 1e6
    is_grp = lane < N_GROUPS
    gl = jnp.where(is_grp, logits, neg)
    gmax = jnp.max(gl, axis=-1, keepdims=True)
    gsel = jnp.min(jnp.where(gl == gmax, lane_f, big), axis=-1, keepdims=True)
    gsum = jnp.sum(jnp.where(is_grp, jnp.exp(gl - gmax), 0.0), axis=-1, keepdims=True)
    p_grp = 1.0 / gsum
    lo = N_GROUPS + gsel * EXPERTS_PER_GROUP
    emask = (lane_f >= lo) & (lane_f < lo + EXPERTS_PER_GROUP)
    el = jnp.where(emask, logits, neg)
    tv1 = jnp.max(el, axis=-1, keepdims=True)
    ti1 = jnp.min(jnp.where(emask & (el == tv1), lane_f, big), axis=-1, keepdims=True)
    emask2 = emask & (lane_f != ti1)
    el2 = jnp.where(emask2, logits, neg)
    tv2 = jnp.max(el2, axis=-1, keepdims=True)
    ti2 = jnp.min(jnp.where(emask2 & (el2 == tv2), lane_f, big), axis=-1, keepdims=True)
    e21 = jnp.exp(tv2 - tv1)
    w1 = p_grp / (1.0 + e21)
    w2 = w1 * e21
    hot1 = lane_f == ti1
    hot2 = lane_f == ti2
    hot12 = jnp.where(hot1 | hot2, 1.0, 0.0)
    before = _dot(tri_ref[...], _bf(hot12))
    c_tile = before[tt - 1:tt, :] + hot12[tt - 1:tt, :]
    c_pad = jnp.floor((c_tile + (SUBLANES - 1.0)) * (1.0 / SUBLANES)) * SUBLANES
    lstart = _dot(_bf(jnp.broadcast_to(c_pad, (SUBLANES, ROUTE_LANES))), upper_ref[...])[0:1, :]
    pos = before + lstart
    slot1 = jnp.sum(jnp.where(hot1, pos, 0.0), axis=-1, keepdims=True)
    slot2 = jnp.sum(jnp.where(hot2, pos, 0.0), axis=-1, keepdims=True)
    gbase = cnt_scr[...]
    cnt_scr[...] = gbase + c_pad
    tab_ref[0, 0] = jnp.concatenate(
        [lstart, c_pad, gbase, jnp.zeros((SUBLANES - 3, ROUTE_LANES), jnp.float32)], axis=0).astype(jnp.int32)
    route_ref[0] = jnp.where(lane == 0, w1, jnp.where(lane == 1, w2, jnp.where(lane == 2, slot1,
                                                                                jnp.where(lane == 3, slot2, 0.0))))

    rec = jnp.where(lane == 0, slot1, jnp.where(lane == 1, slot2, 0.0))
    if tt < LANES:
        rec = jnp.concatenate([rec, jnp.zeros((LANES - tt, ROUTE_LANES), jnp.float32)], axis=0)
    rec_t = rec.T
    s1_row, s2_row = rec_t[0:1, :tt], rec_t[1:2, :tt]
    n_slots = hloc_ref.shape[2]
    slot_iota = lax.broadcasted_iota(jnp.int32, (n_slots, tt), 0).astype(jnp.float32)
    perm = _bf(jnp.where((slot_iota == s1_row) | (slot_iota == s2_row), 1.0, 0.0))
    srt = _dot(perm, h2_hi)
    sbits = lax.bitcast_convert_type(srt, jnp.uint32)
    hloc_ref[0, 0] = sbits[:, :d_half] | (sbits[:, d_half:] >> 16)

    @pl.when(j == last_j)
    def _():
        conv_out_ref[0] = new_hist
        lru_out_ref[0] = carry
        for hd in range(HG_HEADS):
            hg_out_ref[0, hd] = st_scr[hd].T

    @pl.when((bi == last_b) & (j == last_j))
    def _():
        cnt_out_ref[...] = cnt_scr[...]


def _mixer_call(x, moe_prev, mod_prev, mod, conv0, lru0, hg0, wts, cnt_in, *, tt, chunk):
    bsz, t, d = x.shape
    has_prev = moe_prev is not None
    grid = (bsz, t // tt)
    tile = lambda last: pl.BlockSpec((1, tt, last), lambda b, j: (b, j, 0))
    per_b = lambda *shape: pl.BlockSpec((1,) + shape, lambda b, j: (b,) + (0,) * len(shape))
    full = lambda a: pl.BlockSpec(a.shape, lambda b, j: (0,) * a.ndim)
    tri = _bf(jnp.tril(jnp.ones((tt, tt), jnp.float32), -1))
    upper = _bf(jnp.triu(jnp.ones((ROUTE_LANES, ROUTE_LANES), jnp.float32), 1))
    n_slots = _tile_slots(tt)
    per_tile = lambda *shape: pl.BlockSpec((1, 1) + shape, lambda b, j: (b, j) + (0,) * len(shape))

    args, in_specs = [x], [tile(d)]
    if has_prev:
        args += [moe_prev, mod_prev]
        in_specs += [tile(d), per_b(6, d)]
    args += [mod, conv0, lru0.reshape(bsz, 1, LRU_WIDTH), hg0]
    in_specs += [per_b(6, d), per_b(CONV_W - 1, LRU_WIDTH), per_b(1, LRU_WIDTH), per_b(HG_HEADS, HG_DK, HG_DK)]
    args += list(wts) + [tri, upper, cnt_in]
    in_specs += [full(w) for w in wts] + [full(tri), full(upper), full(cnt_in)]

    out_shape = (
        jax.ShapeDtypeStruct((bsz, t, d), jnp.float32),
        jax.ShapeDtypeStruct((bsz, t // tt, n_slots, d // 2), jnp.uint32),
        jax.ShapeDtypeStruct((bsz, t, ROUTE_LANES), jnp.float32),
        jax.ShapeDtypeStruct((bsz, t // tt, SUBLANES, ROUTE_LANES), jnp.int32),
        jax.ShapeDtypeStruct((1, ROUTE_LANES), jnp.float32),
        jax.ShapeDtypeStruct((bsz, CONV_W - 1, LRU_WIDTH), jnp.float32),
        jax.ShapeDtypeStruct((bsz, 1, LRU_WIDTH), jnp.float32),
        jax.ShapeDtypeStruct((bsz, HG_HEADS, HG_DK, HG_DK), jnp.float32),
    )
    out_specs = (tile(d), per_tile(n_slots, d // 2), tile(ROUTE_LANES), per_tile(SUBLANES, ROUTE_LANES),
                 pl.BlockSpec((1, ROUTE_LANES), lambda b, j: (0, 0)),
                 per_b(CONV_W - 1, LRU_WIDTH), per_b(1, LRU_WIDTH), per_b(HG_HEADS, HG_DK, HG_DK))
    scratch = [
        pltpu.VMEM((tt + SUBLANES, LRU_WIDTH), jnp.float32),
        pltpu.VMEM((1, LRU_WIDTH), jnp.float32),
        pltpu.VMEM((HG_HEADS, HG_DK, HG_DK), jnp.float32),
        pltpu.VMEM((1, ROUTE_LANES), jnp.float32),
    ]
    return pl.pallas_call(
        functools.partial(_mixer_kernel, tt=tt, chunk=chunk, has_prev=has_prev),
        grid=grid, in_specs=in_specs, out_specs=out_specs, out_shape=out_shape, scratch_shapes=scratch,
        compiler_params=pltpu.CompilerParams(dimension_semantics=("arbitrary", "arbitrary"),
                                             vmem_limit_bytes=VMEM_LIMIT),
        name="mixer",
    )(*args)


def _mod_kernel(c_ref, w_ref, b_ref, o_ref):
    s = _silu(c_ref[...])
    o_ref[0] = jnp.dot(s, w_ref[0], preferred_element_type=jnp.float32,
                       precision=lax.Precision.HIGHEST) + b_ref[0]


def _mod_call(c_all, w_ada, b_ada):
    depth, d, n = w_ada.shape
    rows = c_all.shape[0]
    tn = d
    return pl.pallas_call(
        _mod_kernel,
        grid=(depth, n // tn),
        in_specs=[pl.BlockSpec((rows, d), lambda l, i: (0, 0)),
                  pl.BlockSpec((1, d, tn), lambda l, i: (l, 0, i)),
                  pl.BlockSpec((1, 1, tn), lambda l, i: (l, 0, i))],
        out_specs=pl.BlockSpec((1, rows, tn), lambda l, i: (l, 0, i)),
        out_shape=jax.ShapeDtypeStruct((depth, rows, n), jnp.float32),
        compiler_params=pltpu.CompilerParams(dimension_semantics=("arbitrary", "arbitrary"),
                                             vmem_limit_bytes=VMEM_LIMIT),
        name="modulation",
    )(c_all, w_ada, b_ada.reshape(depth, 1, n))


def _for_each_granule(off_ref, tab_ref, fn):
    def per_expert(e, total):
        lane = N_GROUPS + e
        n_gran = lax.shift_right_logical(tab_ref[0, 1, lane], SUBLANES.bit_length() - 1)
        src0 = tab_ref[0, 0, lane]
        dst0 = off_ref[e] + tab_ref[0, 2, lane]

        def per_granule(g, c):
            fn(pl.multiple_of(src0 + g * SUBLANES, SUBLANES), pl.multiple_of(dst0 + g * SUBLANES, SUBLANES))
            return c

        lax.fori_loop(0, n_gran, per_granule, 0)
        return total + n_gran

    return lax.fori_loop(0, N_EXPERTS, per_expert, 0)


def _dispatch_kernel(off_ref, tab_ref, hloc_ref, hs_in_ref, hs_ref, sem, pend):
    del hs_in_ref
    i = pl.program_id(0)
    cur = lax.rem(i, 2)

    def granule_copy(which, src, dst):
        return pltpu.make_async_copy(hloc_ref.at[i, pl.ds(src, SUBLANES), :],
                                     hs_ref.at[pl.ds(dst, SUBLANES), :], sem.at[which])

    def drain(which, n):
        lax.fori_loop(0, n, lambda g, c: (granule_copy(which, 0, 0).wait(), c)[1], 0)

    @pl.when(i == 0)
    def _():
        pend[0] = 0

    earlier = pend[0]
    pend[0] = _for_each_granule(off_ref, tab_ref, lambda s, d: granule_copy(cur, s, d).start())
    drain(1 - cur, earlier)

    @pl.when(i == pl.num_programs(0) - 1)
    def _():
        drain(cur, pend[0])


def _dispatch_call(off, tab, hloc, hs):
    n = tab.shape[0]
    grid_spec = pltpu.PrefetchScalarGridSpec(
        num_scalar_prefetch=1, grid=(n,),
        in_specs=[pl.BlockSpec((1, SUBLANES, ROUTE_LANES), lambda i, off: (i, 0, 0), memory_space=pltpu.SMEM),
                  pl.BlockSpec(memory_space=pl.ANY),
                  pl.BlockSpec(memory_space=pl.ANY)],
        out_specs=pl.BlockSpec(memory_space=pl.ANY),
        scratch_shapes=[pltpu.SemaphoreType.DMA((2,)), pltpu.SMEM((1,), jnp.int32)])
    return pl.pallas_call(
        _dispatch_kernel,
        grid_spec=grid_spec,
        out_shape=jax.ShapeDtypeStruct(hs.shape, hs.dtype),
        input_output_aliases={3: 0},
        compiler_params=pltpu.CompilerParams(dimension_semantics=("arbitrary",), vmem_limit_bytes=VMEM_LIMIT),
        name="dispatch",
    )(off, tab, hloc, hs)


def _expert_kernel(te_ref, nt_ref, hs_ref, wgu_ref, wd_ref, ys_ref):
    del te_ref
    i = pl.program_id(0)

    @pl.when(i < nt_ref[0])
    def _():
        w = hs_ref[...]
        dh = w.shape[1]
        a = _bf(lax.bitcast_convert_type(w & jnp.uint32(0xFFFF0000), jnp.float32))
        b = _bf(lax.bitcast_convert_type(w << 16, jnp.float32))
        gu = _dot(a, wgu_ref[0, :dh, :]) + _dot(b, wgu_ref[0, dh:, :])
        hid = _silu(gu[:, :EXPERT_HIDDEN]) * gu[:, EXPERT_HIDDEN:]
        y = _dot(_bf(hid), wd_ref[0])
        ybits = lax.bitcast_convert_type(_bf(y).astype(jnp.float32), jnp.uint32)
        ys_ref[...] = ybits[:, :dh] | (ybits[:, dh:] >> 16)

    @pl.when(i >= nt_ref[0])
    def _():
        ys_ref[...] = jnp.zeros_like(ys_ref)


def _expert_call(tile_expert, n_tiles, hs, wgu, wd, *, te_rows):
    npad, dh = hs.shape
    d = wd.shape[-1]
    grid_spec = pltpu.PrefetchScalarGridSpec(
        num_scalar_prefetch=2, grid=(npad // te_rows,),
        in_specs=[pl.BlockSpec((te_rows, dh), lambda i, te, nt: (i, 0)),
                  pl.BlockSpec((1, 2 * dh, 2 * EXPERT_HIDDEN), lambda i, te, nt: (te[i], 0, 0)),
                  pl.BlockSpec((1, EXPERT_HIDDEN, d), lambda i, te, nt: (te[i], 0, 0))],
        out_specs=pl.BlockSpec((te_rows, dh), lambda i, te, nt: (i, 0)))
    return pl.pallas_call(
        _expert_kernel,
        grid_spec=grid_spec,
        out_shape=jax.ShapeDtypeStruct((npad, dh), jnp.uint32),
        compiler_params=pltpu.CompilerParams(dimension_semantics=("arbitrary",), vmem_limit_bytes=VMEM_LIMIT),
        name="experts",
    )(tile_expert, n_tiles, hs, wgu, wd)


def _combine_kernel(off_ref, tab_ref, tab_next_ref, route_ref, ys_ref, o_ref, yloc, sem, pend):
    i = pl.program_id(0)
    last = pl.num_programs(0) - 1
    cur = lax.rem(i, 2)

    def granule_copy(buf, loc, srt):
        return pltpu.make_async_copy(ys_ref.at[pl.ds(srt, SUBLANES), :],
                                     yloc.at[buf, pl.ds(loc, SUBLANES), :], sem.at[buf])

    def gather(tab, buf):
        return _for_each_granule(off_ref, tab, lambda s, d: granule_copy(buf, s, d).start())

    @pl.when(i == 0)
    def _():
        yloc[...] = jnp.zeros_like(yloc)
        pend[0] = gather(tab_ref, 0)

    n_cur = pend[0]

    @pl.when(i < last)
    def _():
        pend[0] = gather(tab_next_ref, 1 - cur)

    lax.fori_loop(0, n_cur, lambda g, c: (granule_copy(cur, 0, 0).wait(), c)[1], 0)

    rt = route_ref[...]
    tt, n_slots = rt.shape[0], yloc.shape[1]
    slot = lax.broadcasted_iota(jnp.int32, (tt, n_slots), 1).astype(jnp.float32)
    wc = jnp.where(slot == rt[:, 2:3], rt[:, 0:1], 0.0) + jnp.where(slot == rt[:, 3:4], rt[:, 1:2], 0.0)
    wc_hi = _bf(wc)
    wc_lo = _bf(wc - wc_hi.astype(jnp.float32))
    yw = yloc[cur]
    dh = yw.shape[1]
    for half, bits in enumerate((yw & jnp.uint32(0xFFFF0000), yw << 16)):
        yh = _bf(lax.bitcast_convert_type(bits, jnp.float32))
        o_ref[:, half * dh:(half + 1) * dh] = _dot(wc_hi, yh) + _dot(wc_lo, yh)


def _combine_call(off, tab, route, ys, *, n_slots):
    n = tab.shape[0]
    tt = route.shape[0] // n
    dh = ys.shape[-1]
    tab_spec = lambda imap: pl.BlockSpec((1, SUBLANES, ROUTE_LANES), imap, memory_space=pltpu.SMEM)
    grid_spec = pltpu.PrefetchScalarGridSpec(
        num_scalar_prefetch=1, grid=(n,),
        in_specs=[tab_spec(lambda i, off: (i, 0, 0)),
                  tab_spec(lambda i, off: (jnp.minimum(i + 1, n - 1), 0, 0)),
                  pl.BlockSpec((tt, ROUTE_LANES), lambda i, off: (i, 0)),
                  pl.BlockSpec(memory_space=pl.ANY)],
        out_specs=pl.BlockSpec((tt, 2 * dh), lambda i, off: (i, 0)),
        scratch_shapes=[pltpu.VMEM((2, n_slots, dh), jnp.uint32), pltpu.SemaphoreType.DMA((2,)),
                        pltpu.SMEM((1,), jnp.int32)])
    return pl.pallas_call(
        _combine_kernel,
        grid_spec=grid_spec,
        out_shape=jax.ShapeDtypeStruct((n * tt, 2 * dh), jnp.float32),
        compiler_params=pltpu.CompilerParams(dimension_semantics=("arbitrary",), vmem_limit_bytes=VMEM_LIMIT),
        name="combine",
    )(off, tab, tab, route, ys)


def _final_kernel(x_ref, moe_ref, mod_ref, w_ref, o_ref):
    x = x_ref[0] + mod_ref[0, 5:6, :] * moe_ref[0]
    o_ref[0] = _rms_rows(x) * w_ref[...]


def _final_call(x_mid, moe, mod, fw, *, tt):
    bsz, t, d = x_mid.shape
    tile = pl.BlockSpec((1, tt, d), lambda b, j: (b, j, 0))
    return pl.pallas_call(
        _final_kernel,
        grid=(bsz, t // tt),
        in_specs=[tile, tile, pl.BlockSpec((1, 6, d), lambda b, j: (b, 0, 0)),
                  pl.BlockSpec((1, d), lambda b, j: (0, 0))],
        out_specs=tile,
        out_shape=jax.ShapeDtypeStruct((bsz, t, d), jnp.float32),
        compiler_params=pltpu.CompilerParams(dimension_semantics=("arbitrary", "arbitrary"),
                                             vmem_limit_bytes=VMEM_LIMIT),
        name="final_norm",
    )(x_mid, moe, mod, fw.reshape(1, d))


def _block_diag_halves(w):
    nb, bd, _ = w.shape
    per = nb // 2
    eye = jnp.eye(per, dtype=w.dtype)
    halves = [jnp.einsum('nij,nm->nimj', w[h * per:(h + 1) * per], eye).reshape(per * bd, per * bd) for h in range(2)]
    return jnp.stack(halves)


def _tile_slots(tt):
    worst = 2 * tt + (SUBLANES - 1) * min(N_EXPERTS, 2 * tt)
    return -(-worst // LANES) * LANES


def _tile_rows(t):
    for cand in (256, 128, 64, 32, 16, 8):
        if t % cand == 0:
            return cand
    raise ValueError(f"sequence length {t} is not a multiple of 8")


def kernel(x_prompt, x_sample, state_conv, state_lru, state_hgrn, c_prompt, c_sample, w_ada, b_ada, w_in, conv_w,
           conv_b, lru_wa, lru_ba, lru_wx, lru_bx, lru_lambda, hg_lower, hg_norm_w, w_out, w_grp, b_grp, w_rt, b_rt,
           w_gate, w_up, w_down, final_norm_w):
    depth = w_in.shape[0]
    bp, tp, d = x_prompt.shape
    bs, ts, _ = x_sample.shape
    f32 = jnp.float32

    c_all = jnp.concatenate([c_prompt, c_sample], axis=0)
    rows = -(-c_all.shape[0] // SUBLANES) * SUBLANES
    c_all = jnp.pad(c_all, ((0, rows - c_all.shape[0]), (0, 0)))
    mod_all = _mod_call(c_all, w_ada, b_ada).reshape(depth, rows, 6, d)

    p_low = jax.nn.softmax(hg_lower.astype(f32), axis=0)
    lbs = jnp.cumsum(p_low, axis=0) - p_low[0]
    w_in_bf, w_out_bf = _bf(w_in), _bf(w_out)
    wr = jnp.concatenate([w_grp, w_rt], axis=-1)
    wr = jnp.pad(wr, ((0, 0), (0, 0), (0, ROUTE_LANES - wr.shape[-1])))
    wr_hi = _bf(wr)
    wr_lo = _bf(wr - wr_hi.astype(f32))
    br = jnp.concatenate([b_grp, b_rt], axis=-1)
    br = jnp.pad(br, ((0, 0), (0, ROUTE_LANES - br.shape[-1])))
    wgu = _bf(jnp.concatenate([w_gate, w_up], axis=-1))
    wd = _bf(w_down)

    trunks = [
        dict(x=x_prompt, b0=0, nb=bp, t=tp,
             conv=jnp.zeros((depth, bp, CONV_W - 1, LRU_WIDTH), f32), lru=jnp.zeros((depth, bp, LRU_WIDTH), f32),
             hg=jnp.zeros((depth, bp, HG_HEADS, HG_DK, HG_DK), f32)),
        dict(x=x_sample, b0=bp, nb=bs, t=ts, conv=state_conv, lru=state_lru, hg=state_hgrn),
    ]
    for tr in trunks:
        tr["tt"] = _tile_rows(tr["t"])
        tr["chunk"] = min(64, tr["tt"])
        tr.update(moe=None, mod_prev=None, convs=[], lrus=[], hgs=[])

    n_rows = sum(tr["nb"] * (tr["t"] // tr["tt"]) * (2 * tr["tt"] + (SUBLANES - 1) * min(N_EXPERTS, 2 * tr["tt"]))
                 for tr in trunks)
    max_tiles = (n_rows + N_EXPERTS * (EXPERT_TILE - 1)) // EXPERT_TILE
    hs_zero = jnp.zeros((max_tiles * EXPERT_TILE, d // 2), jnp.uint32)
    tile_ids = jnp.arange(max_tiles, dtype=jnp.int32)

    for l in range(depth):
        wts = (w_in_bf[l], conv_w[l], conv_b[l].reshape(1, -1), _bf(_block_diag_halves(lru_wa[l])),
               _bf(_block_diag_halves(lru_wx[l])), lru_ba[l].reshape(1, -1), lru_bx[l].reshape(1, -1),
               lru_lambda[l].reshape(1, -1), lbs[l].reshape(1, -1), hg_norm_w[l].reshape(1, -1), w_out_bf[l],
               wr_hi[l], wr_lo[l], br[l].reshape(1, -1))
        cnt = jnp.zeros((1, ROUTE_LANES), f32)
        for tr in trunks:
            mod = mod_all[l, tr["b0"]:tr["b0"] + tr["nb"]]
            tr["x"], tr["hloc"], tr["route"], tr["tab"], cnt, conv_n, lru_n, hg_n = _mixer_call(
                tr["x"], tr["moe"], tr["mod_prev"], mod, tr["conv"][l], tr["lru"][l], tr["hg"][l], wts, cnt,
                tt=tr["tt"], chunk=tr["chunk"])
            tr["mod_prev"] = mod
            tr["convs"].append(conv_n)
            tr["lrus"].append(lru_n.reshape(tr["nb"], LRU_WIDTH))
            tr["hgs"].append(hg_n)

        counts = cnt[0, N_GROUPS:N_GROUPS + N_EXPERTS].astype(jnp.int32)
        tiles_e = (counts + (EXPERT_TILE - 1)) // EXPERT_TILE
        ends = jnp.cumsum(tiles_e)
        off = (ends - tiles_e) * EXPERT_TILE
        n_tiles = ends[-1:]
        tile_expert = jnp.sum(jnp.minimum(tile_ids, n_tiles - 1)[:, None] >= ends[None, :], axis=1).astype(jnp.int32)

        hs = hs_zero
        for tr in trunks:
            tr["tab"] = tr["tab"].reshape((-1,) + tr["tab"].shape[2:])
            hs = _dispatch_call(off, tr["tab"], tr["hloc"].reshape((-1,) + tr["hloc"].shape[2:]), hs)
        ys = _expert_call(tile_expert, n_tiles, hs, wgu[l], wd[l], te_rows=EXPERT_TILE)
        for tr in trunks:
            moe = _combine_call(off, tr["tab"], tr["route"].reshape(-1, ROUTE_LANES), ys,
                                n_slots=_tile_slots(tr["tt"]))
            tr["moe"] = moe.reshape(tr["nb"], tr["t"], d)

    results = []
    for tr in trunks:
        y = _final_call(tr["x"], tr["moe"], tr["mod_prev"], final_norm_w, tt=tr["tt"])
        results.append((y, jnp.stack(tr["convs"]), jnp.stack(tr["lrus"]), jnp.stack(tr["hgs"])))
    (yp, cp, lp, hp), (ys_, cs, ls, hs_) = results
    return (yp, ys_, cp, lp, hp, cs, ls, hs_)
```

```python
import functools

import jax
import jax.numpy as jnp
from jax import lax
from jax.experimental import pallas as pl
from jax.experimental.pallas import tpu as pltpu

D_MODEL = 1024
LRU_WIDTH = 512
LRU_BLOCKS = 8
LRU_C = 8.0
CONV_W = 4
HG_WIDTH = 512
HG_HEADS = 4
HG_DK = 128
N_GROUPS = 4
EXPERTS_PER_GROUP = 8
N_EXPERTS = 32
EXPERT_HIDDEN = 256
EPS = 1e-6

SUBLANES = 8
GRANULE = 16
LANES = 128
HG_SUB = 16
EXP_CLAMP = 80.0
ROUTE_LANES = LANES
EXPERT_TILE = 256
VMEM_LIMIT = 56 * 1024 * 1024

_NT = (((1,), (1,)), ((), ()))
_TN = (((0,), (0,)), ((), ()))


def _bf(x):
    return x.astype(jnp.bfloat16)


def _dot(a, b):
    return jnp.dot(a, b, preferred_element_type=jnp.float32)


def _sigmoid(x):
    return 1.0 / (1.0 + jnp.exp(-x))


def _silu(x):
    return x * _sigmoid(x)


def _gelu_tanh(x):
    return 0.5 * x * (1.0 + jnp.tanh(0.7978845608028654 * (x + 0.044715 * (x * x * x))))


def _softplus(x):
    return jnp.maximum(x, 0.0) + jnp.log(1.0 + jnp.exp(-jnp.abs(x)))


def _rms_rows(x):
    return x * lax.rsqrt(jnp.mean(x * x, axis=-1, keepdims=True) + EPS)


def _scan8(a, b):
    rows = lax.broadcasted_iota(jnp.int32, a.shape, 0)
    for d in (1, 2, 4):
        m = rows >= d
        a_sh = pltpu.roll(a, d, 0)
        b_sh = pltpu.roll(b, d, 0)
        b = jnp.where(m, a * b_sh + b, b)
        a = jnp.where(m, a * a_sh, a)
    return a, b


def _cumsum8(x):
    rows = lax.broadcasted_iota(jnp.int32, x.shape, 0)
    for d in (1, 2, 4):
        x = x + jnp.where(rows >= d, pltpu.roll(x, d, 0), 0.0)
    return x


def _block_refs(b, m):
    n, w = b.shape
    starts, ends = [], []
    for i in range(n // m):
        if i == 0:
            starts.append(jnp.zeros((m, w), jnp.float32))
        else:
            starts.append(jnp.broadcast_to(b[i * m - 1:i * m, :], (m, w)))
        ends.append(jnp.broadcast_to(b[(i + 1) * m - 1:(i + 1) * m, :], (m, w)))
    if len(starts) == 1:
        return starts[0], ends[0]
    return jnp.concatenate(starts, axis=0), jnp.concatenate(ends, axis=0)


def _mixer_kernel(*refs, tt, chunk, has_prev):
    it = iter(refs)
    x_ref = next(it)
    if has_prev:
        moe_ref = next(it)
        modp_ref = next(it)
    (mod_ref, conv0_ref, lru0_ref, hg0_ref, w_in_ref, conv_w_ref, conv_b_ref, wa_ref, wx_ref, ba_ref, bx_ref,
     lam_ref, lbs_ref, hgn_ref, w_out_ref, wr_hi_ref, wr_lo_ref, br_ref, tri_ref, upper_ref, cnt_in_ref) = (
        next(it) for _ in range(21))
    (xmid_ref, h2_ref, route_ref, tab_ref, cnt_out_ref, conv_out_ref, lru_out_ref,
     hg_out_ref) = (next(it) for _ in range(8))
    conv_scr, lru_scr, st_scr, cnt_scr = (next(it) for _ in range(4))

    bi = pl.program_id(0)
    j = pl.program_id(1)
    last_b = pl.num_programs(0) - 1
    last_j = pl.num_programs(1) - 1
    pad = SUBLANES - (CONV_W - 1)

    @pl.when((bi == 0) & (j == 0))
    def _():
        cnt_scr[...] = cnt_in_ref[...]

    @pl.when(j == 0)
    def _():
        conv_scr[pad:SUBLANES, :] = conv0_ref[0]
        lru_scr[...] = lru0_ref[0]
        for hd in range(HG_HEADS):
            st_scr[hd] = hg0_ref[0, hd].T

    x = x_ref[0]
    if has_prev:
        x = x + modp_ref[0, 5:6, :] * moe_ref[0]
    mod = mod_ref[0]
    sh1, sc1, g1, sh2, sc2 = (mod[i:i + 1, :] for i in range(5))

    h = _rms_rows(x) * (1.0 + sc1) + sh1
    u = _dot(_bf(h), w_in_ref[...])
    lw, hw = LRU_WIDTH, HG_WIDTH
    xb, gb = u[:, :lw], u[:, lw:2 * lw]
    qh, fh = u[:, 2 * lw:2 * lw + hw], u[:, 2 * lw + hw:2 * lw + 2 * hw]
    ih, gh = u[:, 2 * lw + 2 * hw:2 * lw + 3 * hw], u[:, 2 * lw + 3 * hw:]

    conv_scr[SUBLANES:SUBLANES + tt, :] = xb
    xc = conv_b_ref[...]
    for k in range(CONV_W):
        xc = xc + conv_w_ref[k:k + 1, :] * conv_scr[pl.ds(pad + k, tt), :]
    new_hist = conv_scr[pl.ds(tt + pad, CONV_W - 1), :]
    conv_scr[pad:SUBLANES, :] = new_hist

    xc_bf = _bf(xc)
    half = lw // 2
    r_pre = jnp.concatenate([_dot(xc_bf[:, i * half:(i + 1) * half], wa_ref[i]) for i in range(2)], axis=1)
    i_pre = jnp.concatenate([_dot(xc_bf[:, i * half:(i + 1) * half], wx_ref[i]) for i in range(2)], axis=1)
    r = _sigmoid(r_pre + ba_ref[...])
    ig = _sigmoid(i_pre + bx_ref[...])
    log_a = (-LRU_C) * r * _softplus(-lam_ref[...])
    a = jnp.exp(log_a)
    th = jnp.tanh(log_a)
    one_minus_a2 = (-2.0) * th / (1.0 - th)
    bt = jnp.sqrt(one_minus_a2) * ig * xc

    carry = lru_scr[...]
    hs = []
    for g in range(tt // SUBLANES):
        sl = slice(g * SUBLANES, (g + 1) * SUBLANES)
        acum, hloc = _scan8(a[sl], bt[sl])
        hg = acum * carry + hloc
        hs.append(hg)
        carry = hg[SUBLANES - 1:SUBLANES, :]
    lru_scr[...] = carry
    hl = jnp.concatenate(hs, axis=0)
    y_lru = hl * _gelu_tanh(gb)

    lbs = lbs_ref[...]
    q = _silu(qh) * (HG_DK ** -0.5)
    f = lbs + (1.0 - lbs) * _sigmoid(fh)
    kk = 1.0 - f
    glog = jnp.log(f)

    ti = lax.broadcasted_iota(jnp.int32, (chunk, chunk), 0)
    si = lax.broadcasted_iota(jnp.int32, (chunk, chunk), 1)
    sub_shift = HG_SUB.bit_length() - 1
    mask_diag = ((ti >> sub_shift) == (si >> sub_shift)) & (ti >= si)
    levels = []
    m = HG_SUB
    while m < chunk:
        sh = m.bit_length() - 1
        levels.append((m, ((ti >> (sh + 1)) == (si >> (sh + 1))) & (((ti >> sh) & 1) == 1) & (((si >> sh) & 1) == 0)))
        m *= 2

    o_chunks = []
    for c in range(tt // chunk):
        rs = slice(c * chunk, (c + 1) * chunk)
        gc = glog[rs]
        bs, bcarry = [], None
        for g in range(chunk // SUBLANES):
            cs = _cumsum8(gc[g * SUBLANES:(g + 1) * SUBLANES])
            if bcarry is not None:
                cs = cs + bcarry
            bs.append(cs)
            bcarry = cs[SUBLANES - 1:SUBLANES, :]
        b = jnp.concatenate(bs, axis=0)
        b_last = bcarry
        qc, kc, vc = q[rs], kk[rs], ih[rs]

        st16, en16 = _block_refs(b, HG_SUB)
        q_lv = {HG_SUB: _bf(qc * jnp.exp(b - st16))}
        k_lv = {HG_SUB: _bf(kc * jnp.exp(en16 - b))}
        k_diag = _bf(kc * jnp.exp(jnp.minimum(st16 - b, EXP_CLAMP)))
        for m, _ in levels:
            if m == HG_SUB:
                continue
            st, en = _block_refs(b, m)
            q_lv[m] = _bf(qc * jnp.exp(b - st))
            k_lv[m] = _bf(kc * jnp.exp(en - b))
        q_all = _bf(qc * jnp.exp(b))
        k_all = _bf(kc * jnp.exp(b_last - b))
        v_bf = _bf(vc)
        eb_last = jnp.exp(b_last)

        o_heads = []
        for hd in range(HG_HEADS):
            hs_ = slice(hd * HG_DK, (hd + 1) * HG_DK)
            amat = jnp.where(mask_diag,
                             lax.dot_general(q_lv[HG_SUB][:, hs_], k_diag[:, hs_], _NT,
                                             preferred_element_type=jnp.float32), 0.0)
            for m, msk in levels:
                amat = amat + jnp.where(msk, lax.dot_general(q_lv[m][:, hs_], k_lv[m][:, hs_], _NT,
                                                             preferred_element_type=jnp.float32), 0.0)
            st_t = st_scr[hd]
            o_h = lax.dot_general(q_all[:, hs_], _bf(st_t), _NT, preferred_element_type=jnp.float32)
            o_h = o_h + _dot(_bf(amat), v_bf[:, hs_])
            st_scr[hd] = st_t * eb_last[:, hs_] + lax.dot_general(v_bf[:, hs_], k_all[:, hs_], _TN,
                                                                  preferred_element_type=jnp.float32)
            o_heads.append(_rms_rows(o_h) * hgn_ref[...])
        o_chunks.append(jnp.concatenate(o_heads, axis=1))
    o_all = o_chunks[0] if len(o_chunks) == 1 else jnp.concatenate(o_chunks, axis=0)
    y_hg = o_all * _silu(gh)

    mix = _dot(_bf(jnp.concatenate([y_lru, y_hg], axis=1)), w_out_ref[...])
    x_mid = x + g1 * mix
    xmid_ref[0] = x_mid

    h2 = _rms_rows(x_mid) * (1.0 + sc2) + sh2
    h2_hi = _bf(h2)
    h2_lo = _bf(h2 - h2_hi.astype(jnp.float32))
    h2_ref[0] = h2_hi
    logits = (_dot(h2_hi, wr_hi_ref[...]) + _dot(h2_hi, wr_lo_ref[...]) + _dot(h2_lo, wr_hi_ref[...])
              + br_ref[...])

    lane = lax.broadcasted_iota(jnp.int32, (tt, ROUTE_LANES), 1)
    lane_f = lane.astype(jnp.float32)
    neg, big = -1e30, 1e6
    is_grp = lane < N_GROUPS
    gl = jnp.where(is_grp, logits, neg)
    gmax = jnp.max(gl, axis=-1, keepdims=True)
    gsel = jnp.min(jnp.where(gl == gmax, lane_f, big), axis=-1, keepdims=True)
    gsum = jnp.sum(jnp.where(is_grp, jnp.exp(gl - gmax), 0.0), axis=-1, keepdims=True)
    p_grp = 1.0 / gsum
    lo = N_GROUPS + gsel * EXPERTS_PER_GROUP
    emask = (lane_f >= lo) & (lane_f < lo + EXPERTS_PER_GROUP)
    el = jnp.where(emask, logits, neg)
    tv1 = jnp.max(el, axis=-1, keepdims=True)
    ti1 = jnp.min(jnp.where(emask & (el == tv1), lane_f, big), axis=-1, keepdims=True)
    emask2 = emask & (lane_f != ti1)
    el2 = jnp.where(emask2, logits, neg)
    tv2 = jnp.max(el2, axis=-1, keepdims=True)
    ti2 = jnp.min(jnp.where(emask2 & (el2 == tv2), lane_f, big), axis=-1, keepdims=True)
    e21 = jnp.exp(tv2 - tv1)
    w1 = p_grp / (1.0 + e21)
    w2 = w1 * e21
    hot1 = lane_f == ti1
    hot2 = lane_f == ti2
    hot12 = jnp.where(hot1 | hot2, 1.0, 0.0)
    before = _dot(tri_ref[...], _bf(hot12))
    c_tile = before[tt - 1:tt, :] + hot12[tt - 1:tt, :]
    c_pad = jnp.floor((c_tile + (GRANULE - 1.0)) * (1.0 / GRANULE)) * GRANULE
    lstart = _dot(_bf(jnp.broadcast_to(c_pad, (SUBLANES, ROUTE_LANES))), upper_ref[...])[0:1, :]
    pos = before + lstart
    slot1 = jnp.sum(jnp.where(hot1, pos, 0.0), axis=-1, keepdims=True)
    slot2 = jnp.sum(jnp.where(hot2, pos, 0.0), axis=-1, keepdims=True)
    gbase = cnt_scr[...]
    cnt_scr[...] = gbase + c_pad
    tab_ref[0, 0] = jnp.concatenate(
        [lstart, c_pad, gbase, jnp.zeros((SUBLANES - 3, ROUTE_LANES), jnp.float32)], axis=0).astype(jnp.int32)
    route_ref[0] = jnp.where(lane == 0, w1, jnp.where(lane == 1, w2, jnp.where(lane == 2, slot1,
                                                                                jnp.where(lane == 3, slot2, 0.0))))

    @pl.when(j == last_j)
    def _():
        conv_out_ref[0] = new_hist
        lru_out_ref[0] = carry
        for hd in range(HG_HEADS):
            hg_out_ref[0, hd] = st_scr[hd].T

    @pl.when((bi == last_b) & (j == last_j))
    def _():
        cnt_out_ref[...] = cnt_scr[...]


def _mixer_call(x, moe_prev, mod_prev, mod, conv0, lru0, hg0, wts, cnt_in, *, tt, chunk):
    bsz, t, d = x.shape
    has_prev = moe_prev is not None
    grid = (bsz, t // tt)
    tile = lambda last: pl.BlockSpec((1, tt, last), lambda b, j: (b, j, 0))
    per_b = lambda *shape: pl.BlockSpec((1,) + shape, lambda b, j: (b,) + (0,) * len(shape))
    full = lambda a: pl.BlockSpec(a.shape, lambda b, j: (0,) * a.ndim)
    tri = _bf(jnp.tril(jnp.ones((tt, tt), jnp.float32), -1))
    upper = _bf(jnp.triu(jnp.ones((ROUTE_LANES, ROUTE_LANES), jnp.float32), 1))
    per_tile = lambda *shape: pl.BlockSpec((1, 1) + shape, lambda b, j: (b, j) + (0,) * len(shape))

    args, in_specs = [x], [tile(d)]
    if has_prev:
        args += [moe_prev, mod_prev]
        in_specs += [tile(d), per_b(6, d)]
    args += [mod, conv0, lru0.reshape(bsz, 1, LRU_WIDTH), hg0]
    in_specs += [per_b(6, d), per_b(CONV_W - 1, LRU_WIDTH), per_b(1, LRU_WIDTH), per_b(HG_HEADS, HG_DK, HG_DK)]
    args += list(wts) + [tri, upper, cnt_in]
    in_specs += [full(w) for w in wts] + [full(tri), full(upper), full(cnt_in)]

    out_shape = (
        jax.ShapeDtypeStruct((bsz, t, d), jnp.float32),
        jax.ShapeDtypeStruct((bsz, t, d), jnp.bfloat16),
        jax.ShapeDtypeStruct((bsz, t, ROUTE_LANES), jnp.float32),
        jax.ShapeDtypeStruct((bsz, t // tt, SUBLANES, ROUTE_LANES), jnp.int32),
        jax.ShapeDtypeStruct((1, ROUTE_LANES), jnp.float32),
        jax.ShapeDtypeStruct((bsz, CONV_W - 1, LRU_WIDTH), jnp.float32),
        jax.ShapeDtypeStruct((bsz, 1, LRU_WIDTH), jnp.float32),
        jax.ShapeDtypeStruct((bsz, HG_HEADS, HG_DK, HG_DK), jnp.float32),
    )
    out_specs = (tile(d), tile(d), tile(ROUTE_LANES), per_tile(SUBLANES, ROUTE_LANES),
                 pl.BlockSpec((1, ROUTE_LANES), lambda b, j: (0, 0)),
                 per_b(CONV_W - 1, LRU_WIDTH), per_b(1, LRU_WIDTH), per_b(HG_HEADS, HG_DK, HG_DK))
    scratch = [
        pltpu.VMEM((tt + SUBLANES, LRU_WIDTH), jnp.float32),
        pltpu.VMEM((1, LRU_WIDTH), jnp.float32),
        pltpu.VMEM((HG_HEADS, HG_DK, HG_DK), jnp.float32),
        pltpu.VMEM((1, ROUTE_LANES), jnp.float32),
    ]
    return pl.pallas_call(
        functools.partial(_mixer_kernel, tt=tt, chunk=chunk, has_prev=has_prev),
        grid=grid, in_specs=in_specs, out_specs=out_specs, out_shape=out_shape, scratch_shapes=scratch,
        compiler_params=pltpu.CompilerParams(dimension_semantics=("arbitrary", "arbitrary"),
                                             vmem_limit_bytes=VMEM_LIMIT),
        name="mixer",
    )(*args)


def _mod_kernel(c_ref, w_ref, b_ref, o_ref):
    s = _silu(c_ref[...])
    o_ref[0] = jnp.dot(s, w_ref[0], preferred_element_type=jnp.float32,
                       precision=lax.Precision.HIGHEST) + b_ref[0]


def _mod_call(c_all, w_ada, b_ada):
    depth, d, n = w_ada.shape
    rows = c_all.shape[0]
    tn = d
    return pl.pallas_call(
        _mod_kernel,
        grid=(depth, n // tn),
        in_specs=[pl.BlockSpec((rows, d), lambda l, i: (0, 0)),
                  pl.BlockSpec((1, d, tn), lambda l, i: (l, 0, i)),
                  pl.BlockSpec((1, 1, tn), lambda l, i: (l, 0, i))],
        out_specs=pl.BlockSpec((1, rows, tn), lambda l, i: (l, 0, i)),
        out_shape=jax.ShapeDtypeStruct((depth, rows, n), jnp.float32),
        compiler_params=pltpu.CompilerParams(dimension_semantics=("arbitrary", "arbitrary"),
                                             vmem_limit_bytes=VMEM_LIMIT),
        name="modulation",
    )(c_all, w_ada, b_ada.reshape(depth, 1, n))


def _for_each_granule(off_ref, tab_ref, fn):
    def per_expert(e, total):
        lane = N_GROUPS + e
        n_gran = lax.shift_right_logical(tab_ref[0, 1, lane], GRANULE.bit_length() - 1)
        src0 = tab_ref[0, 0, lane]
        dst0 = off_ref[e] + tab_ref[0, 2, lane]

        def per_granule(g, c):
            fn(pl.multiple_of(src0 + g * GRANULE, GRANULE), pl.multiple_of(dst0 + g * GRANULE, GRANULE))
            return c

        lax.fori_loop(0, n_gran, per_granule, 0)
        return total + n_gran

    return lax.fori_loop(0, N_EXPERTS, per_expert, 0)


def _dispatch_kernel(off_ref, tab_ref, h_ref, route_ref, hs_in_ref, hs_ref, srt, sem, pend):
    del hs_in_ref
    i = pl.program_id(0)
    cur = lax.rem(i, 2)
    tt, n_slots = h_ref.shape[0], srt.shape[1]

    def granule_copy(buf, src, dst):
        return pltpu.make_async_copy(srt.at[buf, pl.ds(src, GRANULE), :],
                                     hs_ref.at[pl.ds(dst, GRANULE), :], sem.at[buf])

    def drain(buf, n):
        lax.fori_loop(0, n, lambda g, c: (granule_copy(buf, 0, 0).wait(), c)[1], 0)

    @pl.when(i == 0)
    def _():
        pend[0] = 0

    rec = route_ref[...]
    if tt < LANES:
        rec = jnp.concatenate([rec, jnp.zeros((LANES - tt, ROUTE_LANES), jnp.float32)], axis=0)
    rec_t = rec.T
    s1_row, s2_row = rec_t[2:3, :tt], rec_t[3:4, :tt]
    slot_iota = lax.broadcasted_iota(jnp.int32, (n_slots, tt), 0).astype(jnp.float32)
    perm = _bf(jnp.where((slot_iota == s1_row) | (slot_iota == s2_row), 1.0, 0.0))
    srt[cur] = _bf(_dot(perm, h_ref[...]))

    earlier = pend[0]
    pend[0] = _for_each_granule(off_ref, tab_ref, lambda s, d: granule_copy(cur, s, d).start())
    drain(1 - cur, earlier)

    @pl.when(i == pl.num_programs(0) - 1)
    def _():
        drain(cur, pend[0])


def _dispatch_call(off, tab, h2, route, hs, *, n_slots):
    n = tab.shape[0]
    tt = h2.shape[0] // n
    d = h2.shape[1]
    grid_spec = pltpu.PrefetchScalarGridSpec(
        num_scalar_prefetch=1, grid=(n,),
        in_specs=[pl.BlockSpec((1, SUBLANES, ROUTE_LANES), lambda i, off: (i, 0, 0), memory_space=pltpu.SMEM),
                  pl.BlockSpec((tt, d), lambda i, off: (i, 0)),
                  pl.BlockSpec((tt, ROUTE_LANES), lambda i, off: (i, 0)),
                  pl.BlockSpec(memory_space=pl.ANY)],
        out_specs=pl.BlockSpec(memory_space=pl.ANY),
        scratch_shapes=[pltpu.VMEM((2, n_slots, d), jnp.bfloat16), pltpu.SemaphoreType.DMA((2,)),
                        pltpu.SMEM((1,), jnp.int32)])
    return pl.pallas_call(
        _dispatch_kernel,
        grid_spec=grid_spec,
        out_shape=jax.ShapeDtypeStruct(hs.shape, hs.dtype),
        input_output_aliases={4: 0},
        compiler_params=pltpu.CompilerParams(dimension_semantics=("arbitrary",), vmem_limit_bytes=VMEM_LIMIT),
        name="dispatch",
    )(off, tab, h2, route, hs)


def _expert_kernel(te_ref, nt_ref, hs_ref, wgu_ref, wd_ref, ys_ref):
    del te_ref
    i = pl.program_id(0)

    @pl.when(i < nt_ref[0])
    def _():
        gu = _dot(hs_ref[...], wgu_ref[0])
        hid = _silu(gu[:, :EXPERT_HIDDEN]) * gu[:, EXPERT_HIDDEN:]
        ys_ref[...] = _bf(_dot(_bf(hid), wd_ref[0]))

    @pl.when(i >= nt_ref[0])
    def _():
        ys_ref[...] = jnp.zeros_like(ys_ref)


def _expert_call(tile_expert, n_tiles, hs, wgu, wd, *, te_rows):
    npad, d = hs.shape
    grid_spec = pltpu.PrefetchScalarGridSpec(
        num_scalar_prefetch=2, grid=(npad // te_rows,),
        in_specs=[pl.BlockSpec((te_rows, d), lambda i, te, nt: (i, 0)),
                  pl.BlockSpec((1, d, 2 * EXPERT_HIDDEN), lambda i, te, nt: (te[i], 0, 0)),
                  pl.BlockSpec((1, EXPERT_HIDDEN, d), lambda i, te, nt: (te[i], 0, 0))],
        out_specs=pl.BlockSpec((te_rows, d), lambda i, te, nt: (i, 0)))
    return pl.pallas_call(
        _expert_kernel,
        grid_spec=grid_spec,
        out_shape=jax.ShapeDtypeStruct((npad, d), jnp.bfloat16),
        compiler_params=pltpu.CompilerParams(dimension_semantics=("arbitrary",), vmem_limit_bytes=VMEM_LIMIT),
        name="experts",
    )(tile_expert, n_tiles, hs, wgu, wd)


def _combine_kernel(off_ref, tab_ref, tab_next_ref, route_ref, ys_ref, o_ref, yloc, sem, pend):
    i = pl.program_id(0)
    last = pl.num_programs(0) - 1
    cur = lax.rem(i, 2)

    def granule_copy(buf, loc, srt):
        return pltpu.make_async_copy(ys_ref.at[pl.ds(srt, GRANULE), :],
                                     yloc.at[buf, pl.ds(loc, GRANULE), :], sem.at[buf])

    def gather(tab, buf):
        return _for_each_granule(off_ref, tab, lambda s, d: granule_copy(buf, s, d).start())

    @pl.when(i == 0)
    def _():
        yloc[...] = jnp.zeros_like(yloc)
        pend[0] = gather(tab_ref, 0)

    n_cur = pend[0]

    @pl.when(i < last)
    def _():
        pend[0] = gather(tab_next_ref, 1 - cur)

    lax.fori_loop(0, n_cur, lambda g, c: (granule_copy(cur, 0, 0).wait(), c)[1], 0)

    rt = route_ref[...]
    tt, n_slots = rt.shape[0], yloc.shape[1]
    slot = lax.broadcasted_iota(jnp.int32, (tt, n_slots), 1).astype(jnp.float32)
    wc = jnp.where(slot == rt[:, 2:3], rt[:, 0:1], 0.0) + jnp.where(slot == rt[:, 3:4], rt[:, 1:2], 0.0)
    o_ref[...] = _dot(_bf(wc), yloc[cur])


def _combine_call(off, tab, route, ys, *, n_slots):
    n = tab.shape[0]
    tt = route.shape[0] // n
    d = ys.shape[-1]
    tab_spec = lambda imap: pl.BlockSpec((1, SUBLANES, ROUTE_LANES), imap, memory_space=pltpu.SMEM)
    grid_spec = pltpu.PrefetchScalarGridSpec(
        num_scalar_prefetch=1, grid=(n,),
        in_specs=[tab_spec(lambda i, off: (i, 0, 0)),
                  tab_spec(lambda i, off: (jnp.minimum(i + 1, n - 1), 0, 0)),
                  pl.BlockSpec((tt, ROUTE_LANES), lambda i, off: (i, 0)),
                  pl.BlockSpec(memory_space=pl.ANY)],
        out_specs=pl.BlockSpec((tt, d), lambda i, off: (i, 0)),
        scratch_shapes=[pltpu.VMEM((2, n_slots, d), jnp.bfloat16), pltpu.SemaphoreType.DMA((2,)),
                        pltpu.SMEM((1,), jnp.int32)])
    return pl.pallas_call(
        _combine_kernel,
        grid_spec=grid_spec,
        out_shape=jax.ShapeDtypeStruct((n * tt, d), jnp.float32),
        compiler_params=pltpu.CompilerParams(dimension_semantics=("arbitrary",), vmem_limit_bytes=VMEM_LIMIT),
        name="combine",
    )(off, tab, tab, route, ys)


def _final_kernel(x_ref, moe_ref, mod_ref, w_ref, o_ref):
    x = x_ref[0] + mod_ref[0, 5:6, :] * moe_ref[0]
    o_ref[0] = _rms_rows(x) * w_ref[...]


def _final_call(x_mid, moe, mod, fw, *, tt):
    bsz, t, d = x_mid.shape
    tile = pl.BlockSpec((1, tt, d), lambda b, j: (b, j, 0))
    return pl.pallas_call(
        _final_kernel,
        grid=(bsz, t // tt),
        in_specs=[tile, tile, pl.BlockSpec((1, 6, d), lambda b, j: (b, 0, 0)),
                  pl.BlockSpec((1, d), lambda b, j: (0, 0))],
        out_specs=tile,
        out_shape=jax.ShapeDtypeStruct((bsz, t, d), jnp.float32),
        compiler_params=pltpu.CompilerParams(dimension_semantics=("arbitrary", "arbitrary"),
                                             vmem_limit_bytes=VMEM_LIMIT),
        name="final_norm",
    )(x_mid, moe, mod, fw.reshape(1, d))


def _block_diag_halves(w):
    nb, bd, _ = w.shape
    per = nb // 2
    eye = jnp.eye(per, dtype=w.dtype)
    halves = [jnp.einsum('nij,nm->nimj', w[h * per:(h + 1) * per], eye).reshape(per * bd, per * bd) for h in range(2)]
    return jnp.stack(halves)


def _tile_slots(tt):
    return -(-_tile_slots_used(tt) // LANES) * LANES


def _tile_slots_used(tt):
    return 2 * tt + (GRANULE - 1) * min(N_EXPERTS, 2 * tt)


def _tile_rows(t):
    for cand in (256, 128, 64, 32, 16, 8):
        if t % cand == 0:
            return cand
    raise ValueError(f"sequence length {t} is not a multiple of 8")


def kernel(x_prompt, x_sample, state_conv, state_lru, state_hgrn, c_prompt, c_sample, w_ada, b_ada, w_in, conv_w,
           conv_b, lru_wa, lru_ba, lru_wx, lru_bx, lru_lambda, hg_lower, hg_norm_w, w_out, w_grp, b_grp, w_rt, b_rt,
           w_gate, w_up, w_down, final_norm_w):
    depth = w_in.shape[0]
    bp, tp, d = x_prompt.shape
    bs, ts, _ = x_sample.shape
    f32 = jnp.float32

    c_all = jnp.concatenate([c_prompt, c_sample], axis=0)
    rows = -(-c_all.shape[0] // SUBLANES) * SUBLANES
    c_all = jnp.pad(c_all, ((0, rows - c_all.shape[0]), (0, 0)))
    mod_all = _mod_call(c_all, w_ada, b_ada).reshape(depth, rows, 6, d)

    p_low = jax.nn.softmax(hg_lower.astype(f32), axis=0)
    lbs = jnp.cumsum(p_low, axis=0) - p_low[0]
    w_in_bf, w_out_bf = _bf(w_in), _bf(w_out)
    wr = jnp.concatenate([w_grp, w_rt], axis=-1)
    wr = jnp.pad(wr, ((0, 0), (0, 0), (0, ROUTE_LANES - wr.shape[-1])))
    wr_hi = _bf(wr)
    wr_lo = _bf(wr - wr_hi.astype(f32))
    br = jnp.concatenate([b_grp, b_rt], axis=-1)
    br = jnp.pad(br, ((0, 0), (0, ROUTE_LANES - br.shape[-1])))
    wgu = _bf(jnp.concatenate([w_gate, w_up], axis=-1))
    wd = _bf(w_down)

    trunks = [
        dict(x=x_prompt, b0=0, nb=bp, t=tp,
             conv=jnp.zeros((depth, bp, CONV_W - 1, LRU_WIDTH), f32), lru=jnp.zeros((depth, bp, LRU_WIDTH), f32),
             hg=jnp.zeros((depth, bp, HG_HEADS, HG_DK, HG_DK), f32)),
        dict(x=x_sample, b0=bp, nb=bs, t=ts, conv=state_conv, lru=state_lru, hg=state_hgrn),
    ]
    for tr in trunks:
        tr["tt"] = _tile_rows(tr["t"])
        tr["chunk"] = min(64, tr["tt"])
        tr.update(moe=None, mod_prev=None, convs=[], lrus=[], hgs=[])

    n_rows = sum(tr["nb"] * (tr["t"] // tr["tt"]) * _tile_slots_used(tr["tt"]) for tr in trunks)
    max_tiles = (n_rows + N_EXPERTS * (EXPERT_TILE - 1)) // EXPERT_TILE
    hs_zero = jnp.zeros((max_tiles * EXPERT_TILE, d), jnp.bfloat16)
    tile_ids = jnp.arange(max_tiles, dtype=jnp.int32)

    for l in range(depth):
        wts = (w_in_bf[l], conv_w[l], conv_b[l].reshape(1, -1), _bf(_block_diag_halves(lru_wa[l])),
               _bf(_block_diag_halves(lru_wx[l])), lru_ba[l].reshape(1, -1), lru_bx[l].reshape(1, -1),
               lru_lambda[l].reshape(1, -1), lbs[l].reshape(1, -1), hg_norm_w[l].reshape(1, -1), w_out_bf[l],
               wr_hi[l], wr_lo[l], br[l].reshape(1, -1))
        cnt = jnp.zeros((1, ROUTE_LANES), f32)
        for tr in trunks:
            mod = mod_all[l, tr["b0"]:tr["b0"] + tr["nb"]]
            tr["x"], tr["h2"], tr["route"], tr["tab"], cnt, conv_n, lru_n, hg_n = _mixer_call(
                tr["x"], tr["moe"], tr["mod_prev"], mod, tr["conv"][l], tr["lru"][l], tr["hg"][l], wts, cnt,
                tt=tr["tt"], chunk=tr["chunk"])
            tr["mod_prev"] = mod
            tr["convs"].append(conv_n)
            tr["lrus"].append(lru_n.reshape(tr["nb"], LRU_WIDTH))
            tr["hgs"].append(hg_n)

        counts = cnt[0, N_GROUPS:N_GROUPS + N_EXPERTS].astype(jnp.int32)
        tiles_e = (counts + (EXPERT_TILE - 1)) // EXPERT_TILE
        ends = jnp.cumsum(tiles_e)
        off = (ends - tiles_e) * EXPERT_TILE
        n_tiles = ends[-1:]
        tile_expert = jnp.sum(jnp.minimum(tile_ids, n_tiles - 1)[:, None] >= ends[None, :], axis=1).astype(jnp.int32)

        hs = hs_zero
        for tr in trunks:
            tr["tab"] = tr["tab"].reshape((-1,) + tr["tab"].shape[2:])
            tr["route"] = tr["route"].reshape(-1, ROUTE_LANES)
            hs = _dispatch_call(off, tr["tab"], tr["h2"].reshape(-1, d), tr["route"], hs,
                                n_slots=_tile_slots(tr["tt"]))
        ys = _expert_call(tile_expert, n_tiles, hs, wgu[l], wd[l], te_rows=EXPERT_TILE)
        for tr in trunks:
            moe = _combine_call(off, tr["tab"], tr["route"], ys, n_slots=_tile_slots(tr["tt"]))
            tr["moe"] = moe.reshape(tr["nb"], tr["t"], d)

    results = []
    for tr in trunks:
        y = _final_call(tr["x"], tr["moe"], tr["mod_prev"], final_norm_w, tt=tr["tt"])
        results.append((y, jnp.stack(tr["convs"]), jnp.stack(tr["lrus"]), jnp.stack(tr["hgs"])))
    (yp, cp, lp, hp), (ys_, cs, ls, hs_) = results
    return (yp, ys_, cp, lp, hp, cs, ls, hs_)
```

```python
import functools

import jax
import jax.numpy as jnp
from jax import lax
from jax.experimental import pallas as pl
from jax.experimental.pallas import tpu as pltpu

D_MODEL = 1024
LRU_WIDTH = 512
LRU_BLOCKS = 8
LRU_C = 8.0
CONV_W = 4
HG_WIDTH = 512
HG_HEADS = 4
HG_DK = 128
N_GROUPS = 4
EXPERTS_PER_GROUP = 8
N_EXPERTS = 32
EXPERT_HIDDEN = 256
EPS = 1e-6

SUBLANES = 8
GRANULE = 16
LANES = 128
HG_SUB = 16
EXP_CLAMP = 80.0
ROUTE_LANES = LANES
EXPERT_TILE = 512
VMEM_LIMIT = 56 * 1024 * 1024

_NT = (((1,), (1,)), ((), ()))
_TN = (((0,), (0,)), ((), ()))


def _bf(x):
    return x.astype(jnp.bfloat16)


def _dot(a, b):
    return jnp.dot(a, b, preferred_element_type=jnp.float32)


def _sigmoid(x):
    return 1.0 / (1.0 + jnp.exp(-x))


def _silu(x):
    return x * _sigmoid(x)


def _gelu_tanh(x):
    return 0.5 * x * (1.0 + jnp.tanh(0.7978845608028654 * (x + 0.044715 * (x * x * x))))


def _softplus(x):
    return jnp.maximum(x, 0.0) + jnp.log(1.0 + jnp.exp(-jnp.abs(x)))


def _rms_rows(x):
    return x * lax.rsqrt(jnp.mean(x * x, axis=-1, keepdims=True) + EPS)


def _scan8(a, b):
    rows = lax.broadcasted_iota(jnp.int32, a.shape, 0)
    for d in (1, 2, 4):
        m = rows >= d
        a_sh = pltpu.roll(a, d, 0)
        b_sh = pltpu.roll(b, d, 0)
        b = jnp.where(m, a * b_sh + b, b)
        a = jnp.where(m, a * a_sh, a)
    return a, b


def _cumsum8(x):
    rows = lax.broadcasted_iota(jnp.int32, x.shape, 0)
    for d in (1, 2, 4):
        x = x + jnp.where(rows >= d, pltpu.roll(x, d, 0), 0.0)
    return x


def _block_refs(b, m):
    n, w = b.shape
    starts, ends = [], []
    for i in range(n // m):
        if i == 0:
            starts.append(jnp.zeros((m, w), jnp.float32))
        else:
            starts.append(jnp.broadcast_to(b[i * m - 1:i * m, :], (m, w)))
        ends.append(jnp.broadcast_to(b[(i + 1) * m - 1:(i + 1) * m, :], (m, w)))
    if len(starts) == 1:
        return starts[0], ends[0]
    return jnp.concatenate(starts, axis=0), jnp.concatenate(ends, axis=0)


def _mixer_kernel(*refs, tt, chunk):
    it = iter(refs)
    x_ref = next(it)
    (mod_ref, conv0_ref, lru0_ref, hg0_ref, w_in_ref, conv_w_ref, conv_b_ref, wa_ref, wx_ref, ba_ref, bx_ref,
     lam_ref, lbs_ref, hgn_ref, w_out_ref, wr_hi_ref, wr_lo_ref, br_ref, tri_ref, upper_ref, cnt_in_ref) = (
        next(it) for _ in range(21))
    (xmid_ref, h2_ref, route_ref, tab_ref, cnt_out_ref, conv_out_ref, lru_out_ref,
     hg_out_ref) = (next(it) for _ in range(8))
    conv_scr, lru_scr, st_scr, cnt_scr = (next(it) for _ in range(4))

    bi = pl.program_id(0)
    j = pl.program_id(1)
    last_b = pl.num_programs(0) - 1
    last_j = pl.num_programs(1) - 1
    pad = SUBLANES - (CONV_W - 1)

    @pl.when((bi == 0) & (j == 0))
    def _():
        cnt_scr[...] = cnt_in_ref[...]

    @pl.when(j == 0)
    def _():
        conv_scr[pad:SUBLANES, :] = conv0_ref[0]
        lru_scr[...] = lru0_ref[0]
        for hd in range(HG_HEADS):
            st_scr[hd] = hg0_ref[0, hd].T

    x = x_ref[0]
    mod = mod_ref[0]
    sh1, sc1, g1, sh2, sc2 = (mod[i:i + 1, :] for i in range(5))

    h = _rms_rows(x) * (1.0 + sc1) + sh1
    u = _dot(_bf(h), w_in_ref[...])
    lw, hw = LRU_WIDTH, HG_WIDTH
    xb, gb = u[:, :lw], u[:, lw:2 * lw]
    qh, fh = u[:, 2 * lw:2 * lw + hw], u[:, 2 * lw + hw:2 * lw + 2 * hw]
    ih, gh = u[:, 2 * lw + 2 * hw:2 * lw + 3 * hw], u[:, 2 * lw + 3 * hw:]

    conv_scr[SUBLANES:SUBLANES + tt, :] = xb
    xc = conv_b_ref[...]
    for k in range(CONV_W):
        xc = xc + conv_w_ref[k:k + 1, :] * conv_scr[pl.ds(pad + k, tt), :]
    new_hist = conv_scr[pl.ds(tt + pad, CONV_W - 1), :]
    conv_scr[pad:SUBLANES, :] = new_hist

    xc_bf = _bf(xc)
    half = lw // 2
    r_pre = jnp.concatenate([_dot(xc_bf[:, i * half:(i + 1) * half], wa_ref[i]) for i in range(2)], axis=1)
    i_pre = jnp.concatenate([_dot(xc_bf[:, i * half:(i + 1) * half], wx_ref[i]) for i in range(2)], axis=1)
    r = _sigmoid(r_pre + ba_ref[...])
    ig = _sigmoid(i_pre + bx_ref[...])
    log_a = (-LRU_C) * r * _softplus(-lam_ref[...])
    a = jnp.exp(log_a)
    th = jnp.tanh(log_a)
    one_minus_a2 = (-2.0) * th / (1.0 - th)
    bt = jnp.sqrt(one_minus_a2) * ig * xc

    carry = lru_scr[...]
    hs = []
    for g in range(tt // SUBLANES):
        sl = slice(g * SUBLANES, (g + 1) * SUBLANES)
        acum, hloc = _scan8(a[sl], bt[sl])
        hg = acum * carry + hloc
        hs.append(hg)
        carry = hg[SUBLANES - 1:SUBLANES, :]
    lru_scr[...] = carry
    hl = jnp.concatenate(hs, axis=0)
    y_lru = hl * _gelu_tanh(gb)

    lbs = lbs_ref[...]
    q = _silu(qh) * (HG_DK ** -0.5)
    f = lbs + (1.0 - lbs) * _sigmoid(fh)
    kk = 1.0 - f
    glog = jnp.log(f)

    ti = lax.broadcasted_iota(jnp.int32, (chunk, chunk), 0)
    si = lax.broadcasted_iota(jnp.int32, (chunk, chunk), 1)
    sub_shift = HG_SUB.bit_length() - 1
    mask_diag = ((ti >> sub_shift) == (si >> sub_shift)) & (ti >= si)
    levels = []
    m = HG_SUB
    while m < chunk:
        sh = m.bit_length() - 1
        levels.append((m, ((ti >> (sh + 1)) == (si >> (sh + 1))) & (((ti >> sh) & 1) == 1) & (((si >> sh) & 1) == 0)))
        m *= 2

    o_chunks = []
    for c in range(tt // chunk):
        rs = slice(c * chunk, (c + 1) * chunk)
        gc = glog[rs]
        bs, bcarry = [], None
        for g in range(chunk // SUBLANES):
            cs = _cumsum8(gc[g * SUBLANES:(g + 1) * SUBLANES])
            if bcarry is not None:
                cs = cs + bcarry
            bs.append(cs)
            bcarry = cs[SUBLANES - 1:SUBLANES, :]
        b = jnp.concatenate(bs, axis=0)
        b_last = bcarry
        qc, kc, vc = q[rs], kk[rs], ih[rs]

        st16, en16 = _block_refs(b, HG_SUB)
        q_lv = {HG_SUB: _bf(qc * jnp.exp(b - st16))}
        k_lv = {HG_SUB: _bf(kc * jnp.exp(en16 - b))}
        k_diag = _bf(kc * jnp.exp(jnp.minimum(st16 - b, EXP_CLAMP)))
        for m, _ in levels:
            if m == HG_SUB:
                continue
            st, en = _block_refs(b, m)
            q_lv[m] = _bf(qc * jnp.exp(b - st))
            k_lv[m] = _bf(kc * jnp.exp(en - b))
        q_all = _bf(qc * jnp.exp(b))
        k_all = _bf(kc * jnp.exp(b_last - b))
        v_bf = _bf(vc)
        eb_last = jnp.exp(b_last)

        o_heads = []
        for hd in range(HG_HEADS):
            hs_ = slice(hd * HG_DK, (hd + 1) * HG_DK)
            amat = jnp.where(mask_diag,
                             lax.dot_general(q_lv[HG_SUB][:, hs_], k_diag[:, hs_], _NT,
                                             preferred_element_type=jnp.float32), 0.0)
            for m, msk in levels:
                amat = amat + jnp.where(msk, lax.dot_general(q_lv[m][:, hs_], k_lv[m][:, hs_], _NT,
                                                             preferred_element_type=jnp.float32), 0.0)
            st_t = st_scr[hd]
            o_h = lax.dot_general(q_all[:, hs_], _bf(st_t), _NT, preferred_element_type=jnp.float32)
            o_h = o_h + _dot(_bf(amat), v_bf[:, hs_])
            st_scr[hd] = st_t * eb_last[:, hs_] + lax.dot_general(v_bf[:, hs_], k_all[:, hs_], _TN,
                                                                  preferred_element_type=jnp.float32)
            o_heads.append(_rms_rows(o_h) * hgn_ref[...])
        o_chunks.append(jnp.concatenate(o_heads, axis=1))
    o_all = o_chunks[0] if len(o_chunks) == 1 else jnp.concatenate(o_chunks, axis=0)
    y_hg = o_all * _silu(gh)

    mix = _dot(_bf(jnp.concatenate([y_lru, y_hg], axis=1)), w_out_ref[...])
    x_mid = x + g1 * mix
    xmid_ref[0] = x_mid

    h2 = _rms_rows(x_mid) * (1.0 + sc2) + sh2
    h2_hi = _bf(h2)
    h2_lo = _bf(h2 - h2_hi.astype(jnp.float32))
    h2_ref[0] = h2_hi
    logits = (_dot(h2_hi, wr_hi_ref[...]) + _dot(h2_hi, wr_lo_ref[...]) + _dot(h2_lo, wr_hi_ref[...])
              + br_ref[...])

    lane = lax.broadcasted_iota(jnp.int32, (tt, ROUTE_LANES), 1)
    lane_f = lane.astype(jnp.float32)
    neg, big = -1e30, 1e6
    is_grp = lane < N_GROUPS
    gl = jnp.where(is_grp, logits, neg)
    gmax = jnp.max(gl, axis=-1, keepdims=True)
    gsel = jnp.min(jnp.where(gl == gmax, lane_f, big), axis=-1, keepdims=True)
    gsum = jnp.sum(jnp.where(is_grp, jnp.exp(gl - gmax), 0.0), axis=-1, keepdims=True)
    p_grp = 1.0 / gsum
    lo = N_GROUPS + gsel * EXPERTS_PER_GROUP
    emask = (lane_f >= lo) & (lane_f < lo + EXPERTS_PER_GROUP)
    el = jnp.where(emask, logits, neg)
    tv1 = jnp.max(el, axis=-1, keepdims=True)
    ti1 = jnp.min(jnp.where(emask & (el == tv1), lane_f, big), axis=-1, keepdims=True)
    emask2 = emask & (lane_f != ti1)
    el2 = jnp.where(emask2, logits, neg)
    tv2 = jnp.max(el2, axis=-1, keepdims=True)
    ti2 = jnp.min(jnp.where(emask2 & (el2 == tv2), lane_f, big), axis=-1, keepdims=True)
    e21 = jnp.exp(tv2 - tv1)
    w1 = p_grp / (1.0 + e21)
    w2 = w1 * e21
    hot1 = lane_f == ti1
    hot2 = lane_f == ti2
    hot12 = jnp.where(hot1 | hot2, 1.0, 0.0)
    before = _dot(tri_ref[...], _bf(hot12))
    c_tile = before[tt - 1:tt, :] + hot12[tt - 1:tt, :]
    c_pad = jnp.floor((c_tile + (GRANULE - 1.0)) * (1.0 / GRANULE)) * GRANULE
    lstart = _dot(_bf(jnp.broadcast_to(c_pad, (SUBLANES, ROUTE_LANES))), upper_ref[...])[0:1, :]
    pos = before + lstart
    slot1 = jnp.sum(jnp.where(hot1, pos, 0.0), axis=-1, keepdims=True)
    slot2 = jnp.sum(jnp.where(hot2, pos, 0.0), axis=-1, keepdims=True)
    gbase = cnt_scr[...]
    cnt_scr[...] = gbase + c_pad
    tab_ref[0, 0] = jnp.concatenate(
        [lstart, c_pad, gbase, jnp.zeros((SUBLANES - 3, ROUTE_LANES), jnp.float32)], axis=0).astype(jnp.int32)
    route_ref[0] = jnp.where(lane == 0, w1, jnp.where(lane == 1, w2, jnp.where(lane == 2, slot1,
                                                                                jnp.where(lane == 3, slot2, 0.0))))

    @pl.when(j == last_j)
    def _():
        conv_out_ref[0] = new_hist
        lru_out_ref[0] = carry
        for hd in range(HG_HEADS):
            hg_out_ref[0, hd] = st_scr[hd].T

    @pl.when((bi == last_b) & (j == last_j))
    def _():
        cnt_out_ref[...] = cnt_scr[...]


def _mixer_call(x, mod, conv0, lru0, hg0, wts, cnt_in, *, tt, chunk):
    bsz, t, d = x.shape
    grid = (bsz, t // tt)
    tile = lambda last: pl.BlockSpec((1, tt, last), lambda b, j: (b, j, 0))
    per_b = lambda *shape: pl.BlockSpec((1,) + shape, lambda b, j: (b,) + (0,) * len(shape))
    full = lambda a: pl.BlockSpec(a.shape, lambda b, j: (0,) * a.ndim)
    tri = _bf(jnp.tril(jnp.ones((tt, tt), jnp.float32), -1))
    upper = _bf(jnp.triu(jnp.ones((ROUTE_LANES, ROUTE_LANES), jnp.float32), 1))
    per_tile = lambda *shape: pl.BlockSpec((1, 1) + shape, lambda b, j: (b, j) + (0,) * len(shape))

    args, in_specs = [x], [tile(d)]
    args += [mod, conv0, lru0.reshape(bsz, 1, LRU_WIDTH), hg0]
    in_specs += [per_b(6, d), per_b(CONV_W - 1, LRU_WIDTH), per_b(1, LRU_WIDTH), per_b(HG_HEADS, HG_DK, HG_DK)]
    args += list(wts) + [tri, upper, cnt_in]
    in_specs += [full(w) for w in wts] + [full(tri), full(upper), full(cnt_in)]

    out_shape = (
        jax.ShapeDtypeStruct((bsz, t, d), jnp.float32),
        jax.ShapeDtypeStruct((bsz, t, d), jnp.bfloat16),
        jax.ShapeDtypeStruct((bsz, t, ROUTE_LANES), jnp.float32),
        jax.ShapeDtypeStruct((bsz, t // tt, SUBLANES, ROUTE_LANES), jnp.int32),
        jax.ShapeDtypeStruct((1, ROUTE_LANES), jnp.float32),
        jax.ShapeDtypeStruct((bsz, CONV_W - 1, LRU_WIDTH), jnp.float32),
        jax.ShapeDtypeStruct((bsz, 1, LRU_WIDTH), jnp.float32),
        jax.ShapeDtypeStruct((bsz, HG_HEADS, HG_DK, HG_DK), jnp.float32),
    )
    out_specs = (tile(d), tile(d), tile(ROUTE_LANES), per_tile(SUBLANES, ROUTE_LANES),
                 pl.BlockSpec((1, ROUTE_LANES), lambda b, j: (0, 0)),
                 per_b(CONV_W - 1, LRU_WIDTH), per_b(1, LRU_WIDTH), per_b(HG_HEADS, HG_DK, HG_DK))
    scratch = [
        pltpu.VMEM((tt + SUBLANES, LRU_WIDTH), jnp.float32),
        pltpu.VMEM((1, LRU_WIDTH), jnp.float32),
        pltpu.VMEM((HG_HEADS, HG_DK, HG_DK), jnp.float32),
        pltpu.VMEM((1, ROUTE_LANES), jnp.float32),
    ]
    return pl.pallas_call(
        functools.partial(_mixer_kernel, tt=tt, chunk=chunk),
        grid=grid, in_specs=in_specs, out_specs=out_specs, out_shape=out_shape, scratch_shapes=scratch,
        compiler_params=pltpu.CompilerParams(dimension_semantics=("arbitrary", "arbitrary"),
                                             vmem_limit_bytes=VMEM_LIMIT),
        name="mixer",
    )(*args)


def _mod_kernel(c_ref, w_ref, b_ref, o_ref):
    s = _silu(c_ref[...])
    o_ref[0] = jnp.dot(s, w_ref[0], preferred_element_type=jnp.float32,
                       precision=lax.Precision.HIGHEST) + b_ref[0]


def _mod_call(c_all, w_ada, b_ada):
    depth, d, n = w_ada.shape
    rows = c_all.shape[0]
    tn = d
    return pl.pallas_call(
        _mod_kernel,
        grid=(depth, n // tn),
        in_specs=[pl.BlockSpec((rows, d), lambda l, i: (0, 0)),
                  pl.BlockSpec((1, d, tn), lambda l, i: (l, 0, i)),
                  pl.BlockSpec((1, 1, tn), lambda l, i: (l, 0, i))],
        out_specs=pl.BlockSpec((1, rows, tn), lambda l, i: (l, 0, i)),
        out_shape=jax.ShapeDtypeStruct((depth, rows, n), jnp.float32),
        compiler_params=pltpu.CompilerParams(dimension_semantics=("arbitrary", "arbitrary"),
                                             vmem_limit_bytes=VMEM_LIMIT),
        name="modulation",
    )(c_all, w_ada, b_ada.reshape(depth, 1, n))


def _for_each_granule(off_ref, tab_ref, fn):
    def per_expert(e, total):
        lane = N_GROUPS + e
        n_gran = lax.shift_right_logical(tab_ref[0, 1, lane], GRANULE.bit_length() - 1)
        src0 = tab_ref[0, 0, lane]
        dst0 = off_ref[e] + tab_ref[0, 2, lane]

        def per_granule(g, c):
            fn(pl.multiple_of(src0 + g * GRANULE, GRANULE), pl.multiple_of(dst0 + g * GRANULE, GRANULE))
            return c

        lax.fori_loop(0, n_gran, per_granule, 0)
        return total + n_gran

    return lax.fori_loop(0, N_EXPERTS, per_expert, 0)


def _dispatch_kernel(off_ref, tab_ref, h_ref, route_ref, hs_in_ref, hs_ref, srt, sem, pend):
    del hs_in_ref
    i = pl.program_id(0)
    cur = lax.rem(i, 2)
    tt, n_slots = h_ref.shape[0], srt.shape[1]

    def granule_copy(buf, src, dst):
        return pltpu.make_async_copy(srt.at[buf, pl.ds(src, GRANULE), :],
                                     hs_ref.at[pl.ds(dst, GRANULE), :], sem.at[buf])

    def drain(buf, n):
        lax.fori_loop(0, n, lambda g, c: (granule_copy(buf, 0, 0).wait(), c)[1], 0)

    @pl.when(i == 0)
    def _():
        pend[0] = 0

    rec = route_ref[...]
    if tt < LANES:
        rec = jnp.concatenate([rec, jnp.zeros((LANES - tt, ROUTE_LANES), jnp.float32)], axis=0)
    rec_t = rec.T
    s1_row, s2_row = rec_t[2:3, :tt], rec_t[3:4, :tt]
    slot_iota = lax.broadcasted_iota(jnp.int32, (n_slots, tt), 0).astype(jnp.float32)
    perm = _bf(jnp.where((slot_iota == s1_row) | (slot_iota == s2_row), 1.0, 0.0))
    srt[cur] = _bf(_dot(perm, h_ref[...]))

    earlier = pend[0]
    pend[0] = _for_each_granule(off_ref, tab_ref, lambda s, d: granule_copy(cur, s, d).start())
    drain(1 - cur, earlier)

    @pl.when(i == pl.num_programs(0) - 1)
    def _():
        drain(cur, pend[0])


def _dispatch_call(off, tab, h2, route, hs, *, n_slots):
    n = tab.shape[0]
    tt = h2.shape[0] // n
    d = h2.shape[1]
    grid_spec = pltpu.PrefetchScalarGridSpec(
        num_scalar_prefetch=1, grid=(n,),
        in_specs=[pl.BlockSpec((1, SUBLANES, ROUTE_LANES), lambda i, off: (i, 0, 0), memory_space=pltpu.SMEM),
                  pl.BlockSpec((tt, d), lambda i, off: (i, 0)),
                  pl.BlockSpec((tt, ROUTE_LANES), lambda i, off: (i, 0)),
                  pl.BlockSpec(memory_space=pl.ANY)],
        out_specs=pl.BlockSpec(memory_space=pl.ANY),
        scratch_shapes=[pltpu.VMEM((2, n_slots, d), jnp.bfloat16), pltpu.SemaphoreType.DMA((2,)),
                        pltpu.SMEM((1,), jnp.int32)])
    return pl.pallas_call(
        _dispatch_kernel,
        grid_spec=grid_spec,
        out_shape=jax.ShapeDtypeStruct(hs.shape, hs.dtype),
        input_output_aliases={4: 0},
        compiler_params=pltpu.CompilerParams(dimension_semantics=("arbitrary",), vmem_limit_bytes=VMEM_LIMIT),
        name="dispatch",
    )(off, tab, h2, route, hs)


def _expert_kernel(te_ref, nt_ref, hs_ref, wg_ref, wu_ref, wd_ref, ys_ref, wgu_bf, wd_bf):
    i = pl.program_id(0)
    active = i < nt_ref[0]

    @pl.when(active & ((i == 0) | (te_ref[i] != te_ref[jnp.maximum(i - 1, 0)])))
    def _():
        wgu_bf[:, :EXPERT_HIDDEN] = _bf(wg_ref[0])
        wgu_bf[:, EXPERT_HIDDEN:] = _bf(wu_ref[0])
        wd_bf[...] = _bf(wd_ref[0])

    @pl.when(active)
    def _():
        gu = _dot(hs_ref[...], wgu_bf[...])
        hid = _silu(gu[:, :EXPERT_HIDDEN]) * gu[:, EXPERT_HIDDEN:]
        ys_ref[...] = _bf(_dot(_bf(hid), wd_bf[...]))

    @pl.when(jnp.logical_not(active))
    def _():
        ys_ref[...] = jnp.zeros_like(ys_ref)


def _expert_call(tile_expert, n_tiles, hs, w_gate, w_up, w_down, *, te_rows):
    npad, d = hs.shape
    wspec = lambda a: pl.BlockSpec((1,) + a.shape[1:], lambda i, te, nt: (te[i], 0, 0))
    grid_spec = pltpu.PrefetchScalarGridSpec(
        num_scalar_prefetch=2, grid=(npad // te_rows,),
        in_specs=[pl.BlockSpec((te_rows, d), lambda i, te, nt: (i, 0)), wspec(w_gate), wspec(w_up), wspec(w_down)],
        out_specs=pl.BlockSpec((te_rows, d), lambda i, te, nt: (i, 0)),
        scratch_shapes=[pltpu.VMEM((d, 2 * EXPERT_HIDDEN), jnp.bfloat16),
                        pltpu.VMEM((EXPERT_HIDDEN, d), jnp.bfloat16)])
    return pl.pallas_call(
        _expert_kernel,
        grid_spec=grid_spec,
        out_shape=jax.ShapeDtypeStruct((npad, d), jnp.bfloat16),
        compiler_params=pltpu.CompilerParams(dimension_semantics=("arbitrary",), vmem_limit_bytes=VMEM_LIMIT),
        name="experts",
    )(tile_expert, n_tiles, hs, w_gate, w_up, w_down)


def _combine_kernel(off_ref, tab_ref, tab_next_ref, route_ref, x_ref, mod_ref, fw_ref, ys_ref, o_ref, yloc, sem,
                    pend, *, final):
    i = pl.program_id(0)
    last = pl.num_programs(0) - 1
    cur = lax.rem(i, 2)

    def granule_copy(buf, loc, srt):
        return pltpu.make_async_copy(ys_ref.at[pl.ds(srt, GRANULE), :],
                                     yloc.at[buf, pl.ds(loc, GRANULE), :], sem.at[buf])

    def gather(tab, buf):
        return _for_each_granule(off_ref, tab, lambda s, d: granule_copy(buf, s, d).start())

    @pl.when(i == 0)
    def _():
        yloc[...] = jnp.zeros_like(yloc)
        pend[0] = gather(tab_ref, 0)

    n_cur = pend[0]

    @pl.when(i < last)
    def _():
        pend[0] = gather(tab_next_ref, 1 - cur)

    lax.fori_loop(0, n_cur, lambda g, c: (granule_copy(cur, 0, 0).wait(), c)[1], 0)

    rt = route_ref[...]
    tt, n_slots = rt.shape[0], yloc.shape[1]
    slot = lax.broadcasted_iota(jnp.int32, (tt, n_slots), 1).astype(jnp.float32)
    wc = jnp.where(slot == rt[:, 2:3], rt[:, 0:1], 0.0) + jnp.where(slot == rt[:, 3:4], rt[:, 1:2], 0.0)
    x_new = x_ref[...] + mod_ref[0, 5:6, :] * _dot(_bf(wc), yloc[cur])
    o_ref[...] = _rms_rows(x_new) * fw_ref[...] if final else x_new


def _combine_call(off, tab, route, x_mid, mod, fw, ys, *, n_slots, final):
    n = tab.shape[0]
    tt = route.shape[0] // n
    tiles_per_seq = n // mod.shape[0]
    d = ys.shape[-1]
    tab_spec = lambda imap: pl.BlockSpec((1, SUBLANES, ROUTE_LANES), imap, memory_space=pltpu.SMEM)
    grid_spec = pltpu.PrefetchScalarGridSpec(
        num_scalar_prefetch=1, grid=(n,),
        in_specs=[tab_spec(lambda i, off: (i, 0, 0)),
                  tab_spec(lambda i, off: (jnp.minimum(i + 1, n - 1), 0, 0)),
                  pl.BlockSpec((tt, ROUTE_LANES), lambda i, off: (i, 0)),
                  pl.BlockSpec((tt, d), lambda i, off: (i, 0)),
                  pl.BlockSpec((1,) + mod.shape[1:], lambda i, off: (i // tiles_per_seq, 0, 0)),
                  pl.BlockSpec((1, d), lambda i, off: (0, 0)),
                  pl.BlockSpec(memory_space=pl.ANY)],
        out_specs=pl.BlockSpec((tt, d), lambda i, off: (i, 0)),
        scratch_shapes=[pltpu.VMEM((2, n_slots, d), jnp.bfloat16), pltpu.SemaphoreType.DMA((2,)),
                        pltpu.SMEM((1,), jnp.int32)])
    return pl.pallas_call(
        functools.partial(_combine_kernel, final=final),
        grid_spec=grid_spec,
        out_shape=jax.ShapeDtypeStruct((n * tt, d), jnp.float32),
        compiler_params=pltpu.CompilerParams(dimension_semantics=("arbitrary",), vmem_limit_bytes=VMEM_LIMIT),
        name="combine",
    )(off, tab, tab, route, x_mid, mod, fw, ys)


def _block_diag_halves(w):
    nb, bd, _ = w.shape
    per = nb // 2
    eye = jnp.eye(per, dtype=w.dtype)
    halves = [jnp.einsum('nij,nm->nimj', w[h * per:(h + 1) * per], eye).reshape(per * bd, per * bd) for h in range(2)]
    return jnp.stack(halves)


def _tile_slots(tt):
    return -(-_tile_slots_used(tt) // LANES) * LANES


def _tile_slots_used(tt):
    return 2 * tt + (GRANULE - 1) * min(N_EXPERTS, 2 * tt)


def _tile_rows(t):
    for cand in (256, 128, 64, 32, 16, 8):
        if t % cand == 0:
            return cand
    raise ValueError(f"sequence length {t} is not a multiple of 8")


def kernel(x_prompt, x_sample, state_conv, state_lru, state_hgrn, c_prompt, c_sample, w_ada, b_ada, w_in, conv_w,
           conv_b, lru_wa, lru_ba, lru_wx, lru_bx, lru_lambda, hg_lower, hg_norm_w, w_out, w_grp, b_grp, w_rt, b_rt,
           w_gate, w_up, w_down, final_norm_w):
    depth = w_in.shape[0]
    bp, tp, d = x_prompt.shape
    bs, ts, _ = x_sample.shape
    f32 = jnp.float32

    c_all = jnp.concatenate([c_prompt, c_sample], axis=0)
    rows = -(-c_all.shape[0] // SUBLANES) * SUBLANES
    c_all = jnp.pad(c_all, ((0, rows - c_all.shape[0]), (0, 0)))
    mod_all = _mod_call(c_all, w_ada, b_ada).reshape(depth, rows, 6, d)

    p_low = jax.nn.softmax(hg_lower.astype(f32), axis=0)
    lbs = jnp.cumsum(p_low, axis=0) - p_low[0]
    w_in_bf, w_out_bf = _bf(w_in), _bf(w_out)
    wr = jnp.concatenate([w_grp, w_rt], axis=-1)
    wr = jnp.pad(wr, ((0, 0), (0, 0), (0, ROUTE_LANES - wr.shape[-1])))
    wr_hi = _bf(wr)
    wr_lo = _bf(wr - wr_hi.astype(f32))
    br = jnp.concatenate([b_grp, b_rt], axis=-1)
    br = jnp.pad(br, ((0, 0), (0, ROUTE_LANES - br.shape[-1])))
    fw = final_norm_w.reshape(1, d)

    trunks = [
        dict(x=x_prompt, b0=0, nb=bp, t=tp,
             conv=jnp.zeros((depth, bp, CONV_W - 1, LRU_WIDTH), f32), lru=jnp.zeros((depth, bp, LRU_WIDTH), f32),
             hg=jnp.zeros((depth, bp, HG_HEADS, HG_DK, HG_DK), f32)),
        dict(x=x_sample, b0=bp, nb=bs, t=ts, conv=state_conv, lru=state_lru, hg=state_hgrn),
    ]
    for tr in trunks:
        tr["tt"] = _tile_rows(tr["t"])
        tr["chunk"] = min(64, tr["tt"])
        tr.update(convs=[], lrus=[], hgs=[])

    n_rows = sum(tr["nb"] * (tr["t"] // tr["tt"]) * _tile_slots_used(tr["tt"]) for tr in trunks)
    max_tiles = (n_rows + N_EXPERTS * (EXPERT_TILE - 1)) // EXPERT_TILE
    hs_zero = jnp.zeros((max_tiles * EXPERT_TILE, d), jnp.bfloat16)
    tile_ids = jnp.arange(max_tiles, dtype=jnp.int32)

    for l in range(depth):
        wts = (w_in_bf[l], conv_w[l], conv_b[l].reshape(1, -1), _bf(_block_diag_halves(lru_wa[l])),
               _bf(_block_diag_halves(lru_wx[l])), lru_ba[l].reshape(1, -1), lru_bx[l].reshape(1, -1),
               lru_lambda[l].reshape(1, -1), lbs[l].reshape(1, -1), hg_norm_w[l].reshape(1, -1), w_out_bf[l],
               wr_hi[l], wr_lo[l], br[l].reshape(1, -1))
        cnt = jnp.zeros((1, ROUTE_LANES), f32)
        for tr in trunks:
            mod = mod_all[l, tr["b0"]:tr["b0"] + tr["nb"]]
            tr["x"], tr["h2"], tr["route"], tr["tab"], cnt, conv_n, lru_n, hg_n = _mixer_call(
                tr["x"], mod, tr["conv"][l], tr["lru"][l], tr["hg"][l], wts, cnt, tt=tr["tt"], chunk=tr["chunk"])
            tr["mod"] = mod
            tr["convs"].append(conv_n)
            tr["lrus"].append(lru_n.reshape(tr["nb"], LRU_WIDTH))
            tr["hgs"].append(hg_n)

        counts = cnt[0, N_GROUPS:N_GROUPS + N_EXPERTS].astype(jnp.int32)
        tiles_e = (counts + (EXPERT_TILE - 1)) // EXPERT_TILE
        ends = jnp.cumsum(tiles_e)
        off = (ends - tiles_e) * EXPERT_TILE
        n_tiles = ends[-1:]
        tile_expert = jnp.sum(jnp.minimum(tile_ids, n_tiles - 1)[:, None] >= ends[None, :], axis=1).astype(jnp.int32)

        hs = hs_zero
        for tr in trunks:
            tr["tab"] = tr["tab"].reshape((-1,) + tr["tab"].shape[2:])
            tr["route"] = tr["route"].reshape(-1, ROUTE_LANES)
            hs = _dispatch_call(off, tr["tab"], tr["h2"].reshape(-1, d), tr["route"], hs,
                                n_slots=_tile_slots(tr["tt"]))
        ys = _expert_call(tile_expert, n_tiles, hs, w_gate[l], w_up[l], w_down[l], te_rows=EXPERT_TILE)
        for tr in trunks:
            x_new = _combine_call(off, tr["tab"], tr["route"], tr["x"].reshape(-1, d), tr["mod"], fw, ys,
                                  n_slots=_tile_slots(tr["tt"]), final=(l == depth - 1))
            tr["x"] = x_new.reshape(tr["nb"], tr["t"], d)

    (yp, cp, lp, hp), (ys_, cs, ls, hs_) = [
        (tr["x"], jnp.stack(tr["convs"]), jnp.stack(tr["lrus"]), jnp.stack(tr["hgs"])) for tr in trunks]
    return (yp, ys_, cp, lp, hp, cs, ls, hs_)
```

```python
import functools

import jax
import jax.numpy as jnp
from jax import lax
from jax.experimental import pallas as pl
from jax.experimental.pallas import tpu as pltpu

D_MODEL = 1024
LRU_WIDTH = 512
LRU_BLOCKS = 8
LRU_C = 8.0
CONV_W = 4
HG_WIDTH = 512
HG_HEADS = 4
HG_DK = 128
N_GROUPS = 4
EXPERTS_PER_GROUP = 8
N_EXPERTS = 32
EXPERT_HIDDEN = 256
EPS = 1e-6

SUBLANES = 8
GRANULE = 16
LANES = 128
HG_SUB = 16
EXP_CLAMP = 80.0
ROUTE_LANES = LANES
EXPERT_TILE = 512
VMEM_LIMIT = 56 * 1024 * 1024

_NT = (((1,), (1,)), ((), ()))
_TN = (((0,), (0,)), ((), ()))


def _bf(x):
    return x.astype(jnp.bfloat16)


def _dot(a, b):
    return jnp.dot(a, b, preferred_element_type=jnp.float32)


def _sigmoid(x):
    return 1.0 / (1.0 + jnp.exp(-x))


def _silu(x):
    return x * _sigmoid(x)


def _gelu_tanh(x):
    return 0.5 * x * (1.0 + jnp.tanh(0.7978845608028654 * (x + 0.044715 * (x * x * x))))


def _softplus(x):
    return jnp.maximum(x, 0.0) + jnp.log(1.0 + jnp.exp(-jnp.abs(x)))


def _rms_rows(x):
    return x * lax.rsqrt(jnp.mean(x * x, axis=-1, keepdims=True) + EPS)


def _scan8(a, b):
    rows = lax.broadcasted_iota(jnp.int32, a.shape, 0)
    for d in (1, 2, 4):
        m = rows >= d
        a_sh = pltpu.roll(a, d, 0)
        b_sh = pltpu.roll(b, d, 0)
        b = jnp.where(m, a * b_sh + b, b)
        a = jnp.where(m, a * a_sh, a)
    return a, b


def _cumsum8(x):
    rows = lax.broadcasted_iota(jnp.int32, x.shape, 0)
    for d in (1, 2, 4):
        x = x + jnp.where(rows >= d, pltpu.roll(x, d, 0), 0.0)
    return x


def _block_refs(b, m):
    n, w = b.shape
    starts, ends = [], []
    for i in range(n // m):
        if i == 0:
            starts.append(jnp.zeros((m, w), jnp.float32))
        else:
            starts.append(jnp.broadcast_to(b[i * m - 1:i * m, :], (m, w)))
        ends.append(jnp.broadcast_to(b[(i + 1) * m - 1:(i + 1) * m, :], (m, w)))
    if len(starts) == 1:
        return starts[0], ends[0]
    return jnp.concatenate(starts, axis=0), jnp.concatenate(ends, axis=0)


def _mixer_kernel(*refs, tt, chunk):
    it = iter(refs)
    x_ref = next(it)
    (mod_ref, conv0_ref, lru0_ref, hg0_ref, w_in_ref, conv_w_ref, conv_b_ref, wa_ref, wx_ref, ba_ref, bx_ref,
     lam_ref, lbs_ref, hgn_ref, w_out_ref, wr_hi_ref, wr_lo_ref, br_ref, tri_ref, upper_ref, cnt_in_ref) = (
        next(it) for _ in range(21))
    (xmid_ref, h2_ref, route_ref, tab_ref, cnt_out_ref, conv_out_ref, lru_out_ref,
     hg_out_ref) = (next(it) for _ in range(8))
    conv_scr, lru_scr, st_scr, cnt_scr = (next(it) for _ in range(4))

    bi = pl.program_id(0)
    j = pl.program_id(1)
    last_b = pl.num_programs(0) - 1
    last_j = pl.num_programs(1) - 1
    pad = SUBLANES - (CONV_W - 1)

    @pl.when((bi == 0) & (j == 0))
    def _():
        cnt_scr[...] = cnt_in_ref[...]

    @pl.when(j == 0)
    def _():
        conv_scr[pad:SUBLANES, :] = conv0_ref[0]
        lru_scr[...] = lru0_ref[0]
        for hd in range(HG_HEADS):
            st_scr[hd] = hg0_ref[0, hd].T

    x = x_ref[0]
    mod = mod_ref[0]
    sh1, sc1, g1, sh2, sc2 = (mod[i:i + 1, :] for i in range(5))

    h = _rms_rows(x) * (1.0 + sc1) + sh1
    u = _dot(_bf(h), w_in_ref[...])
    lw, hw = LRU_WIDTH, HG_WIDTH
    xb, gb = u[:, :lw], u[:, lw:2 * lw]
    qh, fh = u[:, 2 * lw:2 * lw + hw], u[:, 2 * lw + hw:2 * lw + 2 * hw]
    ih, gh = u[:, 2 * lw + 2 * hw:2 * lw + 3 * hw], u[:, 2 * lw + 3 * hw:]

    conv_scr[SUBLANES:SUBLANES + tt, :] = xb
    xc = conv_b_ref[...]
    for k in range(CONV_W):
        xc = xc + conv_w_ref[k:k + 1, :] * conv_scr[pl.ds(pad + k, tt), :]
    new_hist = conv_scr[pl.ds(tt + pad, CONV_W - 1), :]
    conv_scr[pad:SUBLANES, :] = new_hist

    xc_bf = _bf(xc)
    half = lw // 2
    r_pre = jnp.concatenate([_dot(xc_bf[:, i * half:(i + 1) * half], wa_ref[i]) for i in range(2)], axis=1)
    i_pre = jnp.concatenate([_dot(xc_bf[:, i * half:(i + 1) * half], wx_ref[i]) for i in range(2)], axis=1)
    r = _sigmoid(r_pre + ba_ref[...])
    ig = _sigmoid(i_pre + bx_ref[...])
    log_a = (-LRU_C) * r * _softplus(-lam_ref[...])
    a = jnp.exp(log_a)
    th = jnp.tanh(log_a)
    one_minus_a2 = (-2.0) * th / (1.0 - th)
    bt = jnp.sqrt(one_minus_a2) * ig * xc

    carry = lru_scr[...]
    hs = []
    for g in range(tt // SUBLANES):
        sl = slice(g * SUBLANES, (g + 1) * SUBLANES)
        acum, hloc = _scan8(a[sl], bt[sl])
        hg = acum * carry + hloc
        hs.append(hg)
        carry = hg[SUBLANES - 1:SUBLANES, :]
    lru_scr[...] = carry
    hl = jnp.concatenate(hs, axis=0)
    y_lru = hl * _gelu_tanh(gb)

    lbs = lbs_ref[...]
    q = _silu(qh) * (HG_DK ** -0.5)
    f = lbs + (1.0 - lbs) * _sigmoid(fh)
    kk = 1.0 - f
    glog = jnp.log(f)

    ti = lax.broadcasted_iota(jnp.int32, (chunk, chunk), 0)
    si = lax.broadcasted_iota(jnp.int32, (chunk, chunk), 1)
    sub_shift = HG_SUB.bit_length() - 1
    mask_diag = ((ti >> sub_shift) == (si >> sub_shift)) & (ti >= si)
    levels = []
    m = HG_SUB
    while m < chunk:
        sh = m.bit_length() - 1
        levels.append((m, ((ti >> (sh + 1)) == (si >> (sh + 1))) & (((ti >> sh) & 1) == 1) & (((si >> sh) & 1) == 0)))
        m *= 2

    o_chunks = []
    for c in range(tt // chunk):
        rs = slice(c * chunk, (c + 1) * chunk)
        gc = glog[rs]
        bs, bcarry = [], None
        for g in range(chunk // SUBLANES):
            cs = _cumsum8(gc[g * SUBLANES:(g + 1) * SUBLANES])
            if bcarry is not None:
                cs = cs + bcarry
            bs.append(cs)
            bcarry = cs[SUBLANES - 1:SUBLANES, :]
        b = jnp.concatenate(bs, axis=0)
        b_last = bcarry
        qc, kc, vc = q[rs], kk[rs], ih[rs]

        st16, en16 = _block_refs(b, HG_SUB)
        q_lv = {HG_SUB: _bf(qc * jnp.exp(b - st16))}
        k_lv = {HG_SUB: _bf(kc * jnp.exp(en16 - b))}
        k_diag = _bf(kc * jnp.exp(jnp.minimum(st16 - b, EXP_CLAMP)))
        for m, _ in levels:
            if m == HG_SUB:
                continue
            st, en = _block_refs(b, m)
            q_lv[m] = _bf(qc * jnp.exp(b - st))
            k_lv[m] = _bf(kc * jnp.exp(en - b))
        q_all = _bf(qc * jnp.exp(b))
        k_all = _bf(kc * jnp.exp(b_last - b))
        v_bf = _bf(vc)
        eb_last = jnp.exp(b_last)

        o_heads = []
        for hd in range(HG_HEADS):
            hs_ = slice(hd * HG_DK, (hd + 1) * HG_DK)
            amat = jnp.where(mask_diag,
                             lax.dot_general(q_lv[HG_SUB][:, hs_], k_diag[:, hs_], _NT,
                                             preferred_element_type=jnp.float32), 0.0)
            for m, msk in levels:
                amat = amat + jnp.where(msk, lax.dot_general(q_lv[m][:, hs_], k_lv[m][:, hs_], _NT,
                                                             preferred_element_type=jnp.float32), 0.0)
            st_t = st_scr[hd]
            o_h = lax.dot_general(q_all[:, hs_], _bf(st_t), _NT, preferred_element_type=jnp.float32)
            o_h = o_h + _dot(_bf(amat), v_bf[:, hs_])
            st_scr[hd] = st_t * eb_last[:, hs_] + lax.dot_general(v_bf[:, hs_], k_all[:, hs_], _TN,
                                                                  preferred_element_type=jnp.float32)
            o_heads.append(_rms_rows(o_h) * hgn_ref[...])
        o_chunks.append(jnp.concatenate(o_heads, axis=1))
    o_all = o_chunks[0] if len(o_chunks) == 1 else jnp.concatenate(o_chunks, axis=0)
    y_hg = o_all * _silu(gh)

    mix = _dot(_bf(jnp.concatenate([y_lru, y_hg], axis=1)), w_out_ref[...])
    x_mid = x + g1 * mix
    xmid_ref[0] = x_mid

    h2 = _rms_rows(x_mid) * (1.0 + sc2) + sh2
    h2_hi = _bf(h2)
    h2_lo = _bf(h2 - h2_hi.astype(jnp.float32))
    h2_ref[0] = h2_hi
    logits = (_dot(h2_hi, wr_hi_ref[...]) + _dot(h2_hi, wr_lo_ref[...]) + _dot(h2_lo, wr_hi_ref[...])
              + br_ref[...])

    lane = lax.broadcasted_iota(jnp.int32, (tt, ROUTE_LANES), 1)
    lane_f = lane.astype(jnp.float32)
    neg, big = -1e30, 1e6
    is_grp = lane < N_GROUPS
    gl = jnp.where(is_grp, logits, neg)
    gmax = jnp.max(gl, axis=-1, keepdims=True)
    gsel = jnp.min(jnp.where(gl == gmax, lane_f, big), axis=-1, keepdims=True)
    gsum = jnp.sum(jnp.where(is_grp, jnp.exp(gl - gmax), 0.0), axis=-1, keepdims=True)
    p_grp = 1.0 / gsum
    lo = N_GROUPS + gsel * EXPERTS_PER_GROUP
    emask = (lane_f >= lo) & (lane_f < lo + EXPERTS_PER_GROUP)
    el = jnp.where(emask, logits, neg)
    tv1 = jnp.max(el, axis=-1, keepdims=True)
    ti1 = jnp.min(jnp.where(emask & (el == tv1), lane_f, big), axis=-1, keepdims=True)
    emask2 = emask & (lane_f != ti1)
    el2 = jnp.where(emask2, logits, neg)
    tv2 = jnp.max(el2, axis=-1, keepdims=True)
    ti2 = jnp.min(jnp.where(emask2 & (el2 == tv2), lane_f, big), axis=-1, keepdims=True)
    e21 = jnp.exp(tv2 - tv1)
    w1 = p_grp / (1.0 + e21)
    w2 = w1 * e21
    hot1 = lane_f == ti1
    hot2 = lane_f == ti2
    hot12 = jnp.where(hot1 | hot2, 1.0, 0.0)
    before = _dot(tri_ref[...], _bf(hot12))
    c_tile = before[tt - 1:tt, :] + hot12[tt - 1:tt, :]
    c_pad = jnp.floor((c_tile + (GRANULE - 1.0)) * (1.0 / GRANULE)) * GRANULE
    lstart = _dot(_bf(jnp.broadcast_to(c_pad, (SUBLANES, ROUTE_LANES))), upper_ref[...])[0:1, :]
    pos = before + lstart
    slot1 = jnp.sum(jnp.where(hot1, pos, 0.0), axis=-1, keepdims=True)
    slot2 = jnp.sum(jnp.where(hot2, pos, 0.0), axis=-1, keepdims=True)
    gbase = cnt_scr[...]
    cnt_scr[...] = gbase + c_pad
    tab_ref[0, 0] = jnp.concatenate(
        [lstart, c_pad, gbase, jnp.zeros((SUBLANES - 3, ROUTE_LANES), jnp.float32)], axis=0).astype(jnp.int32)
    route_ref[0] = jnp.where(lane == 0, w1, jnp.where(lane == 1, w2, jnp.where(lane == 2, slot1,
                                                                                jnp.where(lane == 3, slot2, 0.0))))

    @pl.when(j == last_j)
    def _():
        conv_out_ref[0] = new_hist
        lru_out_ref[0] = carry
        for hd in range(HG_HEADS):
            hg_out_ref[0, hd] = st_scr[hd].T

    @pl.when((bi == last_b) & (j == last_j))
    def _():
        cnt_out_ref[...] = cnt_scr[...]


def _mixer_call(x, mod, conv0, lru0, hg0, wts, cnt_in, *, tt, chunk):
    bsz, t, d = x.shape
    grid = (bsz, t // tt)
    tile = lambda last: pl.BlockSpec((1, tt, last), lambda b, j: (b, j, 0))
    per_b = lambda *shape: pl.BlockSpec((1,) + shape, lambda b, j: (b,) + (0,) * len(shape))
    full = lambda a: pl.BlockSpec(a.shape, lambda b, j: (0,) * a.ndim)
    tri = _bf(jnp.tril(jnp.ones((tt, tt), jnp.float32), -1))
    upper = _bf(jnp.triu(jnp.ones((ROUTE_LANES, ROUTE_LANES), jnp.float32), 1))
    per_tile = lambda *shape: pl.BlockSpec((1, 1) + shape, lambda b, j: (b, j) + (0,) * len(shape))

    args, in_specs = [x], [tile(d)]
    args += [mod, conv0, lru0.reshape(bsz, 1, LRU_WIDTH), hg0]
    in_specs += [per_b(6, d), per_b(CONV_W - 1, LRU_WIDTH), per_b(1, LRU_WIDTH), per_b(HG_HEADS, HG_DK, HG_DK)]
    args += list(wts) + [tri, upper, cnt_in]
    in_specs += [full(w) for w in wts] + [full(tri), full(upper), full(cnt_in)]

    out_shape = (
        jax.ShapeDtypeStruct((bsz, t, d), jnp.float32),
        jax.ShapeDtypeStruct((bsz, t, d), jnp.bfloat16),
        jax.ShapeDtypeStruct((bsz, t, ROUTE_LANES), jnp.float32),
        jax.ShapeDtypeStruct((bsz, t // tt, SUBLANES, ROUTE_LANES), jnp.int32),
        jax.ShapeDtypeStruct((1, ROUTE_LANES), jnp.float32),
        jax.ShapeDtypeStruct((bsz, CONV_W - 1, LRU_WIDTH), jnp.float32),
        jax.ShapeDtypeStruct((bsz, 1, LRU_WIDTH), jnp.float32),
        jax.ShapeDtypeStruct((bsz, HG_HEADS, HG_DK, HG_DK), jnp.float32),
    )
    out_specs = (tile(d), tile(d), tile(ROUTE_LANES), per_tile(SUBLANES, ROUTE_LANES),
                 pl.BlockSpec((1, ROUTE_LANES), lambda b, j: (0, 0)),
                 per_b(CONV_W - 1, LRU_WIDTH), per_b(1, LRU_WIDTH), per_b(HG_HEADS, HG_DK, HG_DK))
    scratch = [
        pltpu.VMEM((tt + SUBLANES, LRU_WIDTH), jnp.float32),
        pltpu.VMEM((1, LRU_WIDTH), jnp.float32),
        pltpu.VMEM((HG_HEADS, HG_DK, HG_DK), jnp.float32),
        pltpu.VMEM((1, ROUTE_LANES), jnp.float32),
    ]
    return pl.pallas_call(
        functools.partial(_mixer_kernel, tt=tt, chunk=chunk),
        grid=grid, in_specs=in_specs, out_specs=out_specs, out_shape=out_shape, scratch_shapes=scratch,
        compiler_params=pltpu.CompilerParams(dimension_semantics=("arbitrary", "arbitrary"),
                                             vmem_limit_bytes=VMEM_LIMIT),
        name="mixer",
    )(*args)


def _mod_kernel(c_ref, w_ref, b_ref, o_ref):
    s = _silu(c_ref[...])
    o_ref[0] = jnp.dot(s, w_ref[0], preferred_element_type=jnp.float32,
                       precision=lax.Precision.HIGHEST) + b_ref[0]


def _mod_call(c_all, w_ada, b_ada):
    depth, d, n = w_ada.shape
    rows = c_all.shape[0]
    tn = d
    return pl.pallas_call(
        _mod_kernel,
        grid=(depth, n // tn),
        in_specs=[pl.BlockSpec((rows, d), lambda l, i: (0, 0)),
                  pl.BlockSpec((1, d, tn), lambda l, i: (l, 0, i)),
                  pl.BlockSpec((1, 1, tn), lambda l, i: (l, 0, i))],
        out_specs=pl.BlockSpec((1, rows, tn), lambda l, i: (l, 0, i)),
        out_shape=jax.ShapeDtypeStruct((depth, rows, n), jnp.float32),
        compiler_params=pltpu.CompilerParams(dimension_semantics=("arbitrary", "arbitrary"),
                                             vmem_limit_bytes=VMEM_LIMIT),
        name="modulation",
    )(c_all, w_ada, b_ada.reshape(depth, 1, n))


def _for_each_granule(off_ref, tab_ref, fn):
    def per_expert(e, total):
        lane = N_GROUPS + e
        n_gran = lax.shift_right_logical(tab_ref[0, 1, lane], GRANULE.bit_length() - 1)
        src0 = tab_ref[0, 0, lane]
        dst0 = off_ref[e] + tab_ref[0, 2, lane]

        def per_granule(g, c):
            fn(pl.multiple_of(src0 + g * GRANULE, GRANULE), pl.multiple_of(dst0 + g * GRANULE, GRANULE))
            return c

        lax.fori_loop(0, n_gran, per_granule, 0)
        return total + n_gran

    return lax.fori_loop(0, N_EXPERTS, per_expert, 0)


def _dispatch_kernel(off_ref, tab_ref, h_ref, route_ref, hs_in_ref, hs_ref, srt, sem, pend):
    del hs_in_ref
    i = pl.program_id(0)
    cur = lax.rem(i, 2)
    tt, n_slots = h_ref.shape[0], srt.shape[1]

    def granule_copy(buf, src, dst):
        return pltpu.make_async_copy(srt.at[buf, pl.ds(src, GRANULE), :],
                                     hs_ref.at[pl.ds(dst, GRANULE), :], sem.at[buf])

    def drain(buf, n):
        lax.fori_loop(0, n, lambda g, c: (granule_copy(buf, 0, 0).wait(), c)[1], 0)

    @pl.when(i == 0)
    def _():
        pend[0] = 0

    rec = route_ref[...]
    if tt < LANES:
        rec = jnp.concatenate([rec, jnp.zeros((LANES - tt, ROUTE_LANES), jnp.float32)], axis=0)
    rec_t = rec.T
    s1_row, s2_row = rec_t[2:3, :tt], rec_t[3:4, :tt]
    slot_iota = lax.broadcasted_iota(jnp.int32, (n_slots, tt), 0).astype(jnp.float32)
    perm = _bf(jnp.where((slot_iota == s1_row) | (slot_iota == s2_row), 1.0, 0.0))
    srt[cur] = _bf(_dot(perm, h_ref[...]))

    earlier = pend[0]
    pend[0] = _for_each_granule(off_ref, tab_ref, lambda s, d: granule_copy(cur, s, d).start())
    drain(1 - cur, earlier)

    @pl.when(i == pl.num_programs(0) - 1)
    def _():
        drain(cur, pend[0])


def _dispatch_call(off, tab, h2, route, hs, *, n_slots):
    n = tab.shape[0]
    tt = h2.shape[0] // n
    d = h2.shape[1]
    grid_spec = pltpu.PrefetchScalarGridSpec(
        num_scalar_prefetch=1, grid=(n,),
        in_specs=[pl.BlockSpec((1, SUBLANES, ROUTE_LANES), lambda i, off: (i, 0, 0), memory_space=pltpu.SMEM),
                  pl.BlockSpec((tt, d), lambda i, off: (i, 0)),
                  pl.BlockSpec((tt, ROUTE_LANES), lambda i, off: (i, 0)),
                  pl.BlockSpec(memory_space=pl.ANY)],
        out_specs=pl.BlockSpec(memory_space=pl.ANY),
        scratch_shapes=[pltpu.VMEM((2, n_slots, d), jnp.bfloat16), pltpu.SemaphoreType.DMA((2,)),
                        pltpu.SMEM((1,), jnp.int32)])
    return pl.pallas_call(
        _dispatch_kernel,
        grid_spec=grid_spec,
        out_shape=jax.ShapeDtypeStruct(hs.shape, hs.dtype),
        input_output_aliases={4: 0},
        compiler_params=pltpu.CompilerParams(dimension_semantics=("arbitrary",), vmem_limit_bytes=VMEM_LIMIT),
        name="dispatch",
    )(off, tab, h2, route, hs)


def _expert_kernel(te_ref, nt_ref, hs_ref, wg_ref, wu_ref, wd_ref, ys_ref, wgu_bf, wd_bf):
    i = pl.program_id(0)
    active = i < nt_ref[0]

    @pl.when(active & ((i == 0) | (te_ref[i] != te_ref[jnp.maximum(i - 1, 0)])))
    def _():
        wgu_bf[:, :EXPERT_HIDDEN] = _bf(wg_ref[0, 0])
        wgu_bf[:, EXPERT_HIDDEN:] = _bf(wu_ref[0, 0])
        wd_bf[...] = _bf(wd_ref[0, 0])

    @pl.when(active)
    def _():
        gu = _dot(hs_ref[...], wgu_bf[...])
        hid = _silu(gu[:, :EXPERT_HIDDEN]) * gu[:, EXPERT_HIDDEN:]
        ys_ref[...] = _bf(_dot(_bf(hid), wd_bf[...]))

    @pl.when(jnp.logical_not(active))
    def _():
        ys_ref[...] = jnp.zeros_like(ys_ref)


def _expert_call(tile_expert, n_tiles, hs, w_gate, w_up, w_down, *, layer, te_rows):
    npad, d = hs.shape
    wspec = lambda a: pl.BlockSpec((1, 1) + a.shape[2:], lambda i, te, nt: (layer, te[i], 0, 0))
    grid_spec = pltpu.PrefetchScalarGridSpec(
        num_scalar_prefetch=2, grid=(npad // te_rows,),
        in_specs=[pl.BlockSpec((te_rows, d), lambda i, te, nt: (i, 0)), wspec(w_gate), wspec(w_up), wspec(w_down)],
        out_specs=pl.BlockSpec((te_rows, d), lambda i, te, nt: (i, 0)),
        scratch_shapes=[pltpu.VMEM((d, 2 * EXPERT_HIDDEN), jnp.bfloat16),
                        pltpu.VMEM((EXPERT_HIDDEN, d), jnp.bfloat16)])
    return pl.pallas_call(
        _expert_kernel,
        grid_spec=grid_spec,
        out_shape=jax.ShapeDtypeStruct((npad, d), jnp.bfloat16),
        compiler_params=pltpu.CompilerParams(dimension_semantics=("arbitrary",), vmem_limit_bytes=VMEM_LIMIT),
        name="experts",
    )(tile_expert, n_tiles, hs, w_gate, w_up, w_down)


def _combine_kernel(off_ref, tab_ref, tab_next_ref, route_ref, x_ref, mod_ref, fw_ref, ys_ref, o_ref, yloc, sem,
                    pend, *, final):
    i = pl.program_id(0)
    last = pl.num_programs(0) - 1
    cur = lax.rem(i, 2)

    def granule_copy(buf, loc, srt):
        return pltpu.make_async_copy(ys_ref.at[pl.ds(srt, GRANULE), :],
                                     yloc.at[buf, pl.ds(loc, GRANULE), :], sem.at[buf])

    def gather(tab, buf):
        return _for_each_granule(off_ref, tab, lambda s, d: granule_copy(buf, s, d).start())

    @pl.when(i == 0)
    def _():
        yloc[...] = jnp.zeros_like(yloc)
        pend[0] = gather(tab_ref, 0)

    n_cur = pend[0]

    @pl.when(i < last)
    def _():
        pend[0] = gather(tab_next_ref, 1 - cur)

    lax.fori_loop(0, n_cur, lambda g, c: (granule_copy(cur, 0, 0).wait(), c)[1], 0)

    rt = route_ref[...]
    tt, n_slots = rt.shape[0], yloc.shape[1]
    slot = lax.broadcasted_iota(jnp.int32, (tt, n_slots), 1).astype(jnp.float32)
    wc = jnp.where(slot == rt[:, 2:3], rt[:, 0:1], 0.0) + jnp.where(slot == rt[:, 3:4], rt[:, 1:2], 0.0)
    x_new = x_ref[...] + mod_ref[0, 5:6, :] * _dot(_bf(wc), yloc[cur])
    o_ref[...] = _rms_rows(x_new) * fw_ref[...] if final else x_new


def _combine_call(off, tab, route, x_mid, mod, fw, ys, *, n_slots, final):
    n = tab.shape[0]
    tt = route.shape[0] // n
    tiles_per_seq = n // mod.shape[0]
    d = ys.shape[-1]
    tab_spec = lambda imap: pl.BlockSpec((1, SUBLANES, ROUTE_LANES), imap, memory_space=pltpu.SMEM)
    grid_spec = pltpu.PrefetchScalarGridSpec(
        num_scalar_prefetch=1, grid=(n,),
        in_specs=[tab_spec(lambda i, off: (i, 0, 0)),
                  tab_spec(lambda i, off: (jnp.minimum(i + 1, n - 1), 0, 0)),
                  pl.BlockSpec((tt, ROUTE_LANES), lambda i, off: (i, 0)),
                  pl.BlockSpec((tt, d), lambda i, off: (i, 0)),
                  pl.BlockSpec((1,) + mod.shape[1:], lambda i, off: (i // tiles_per_seq, 0, 0)),
                  pl.BlockSpec((1, d), lambda i, off: (0, 0)),
                  pl.BlockSpec(memory_space=pl.ANY)],
        out_specs=pl.BlockSpec((tt, d), lambda i, off: (i, 0)),
        scratch_shapes=[pltpu.VMEM((2, n_slots, d), jnp.bfloat16), pltpu.SemaphoreType.DMA((2,)),
                        pltpu.SMEM((1,), jnp.int32)])
    return pl.pallas_call(
        functools.partial(_combine_kernel, final=final),
        grid_spec=grid_spec,
        out_shape=jax.ShapeDtypeStruct((n * tt, d), jnp.float32),
        compiler_params=pltpu.CompilerParams(dimension_semantics=("arbitrary",), vmem_limit_bytes=VMEM_LIMIT),
        name="combine",
    )(off, tab, tab, route, x_mid, mod, fw, ys)


def _block_diag_halves(w):
    nb, bd, _ = w.shape
    per = nb // 2
    eye = jnp.eye(per, dtype=w.dtype)
    halves = [jnp.einsum('nij,nm->nimj', w[h * per:(h + 1) * per], eye).reshape(per * bd, per * bd) for h in range(2)]
    return jnp.stack(halves)


def _tile_slots(tt):
    return -(-_tile_slots_used(tt) // LANES) * LANES


def _tile_slots_used(tt):
    return 2 * tt + (GRANULE - 1) * min(N_EXPERTS, 2 * tt)


def _tile_rows(t):
    for cand in (512, 256, 128, 64, 32, 16, 8):
        if t % cand == 0:
            return cand
    raise ValueError(f"sequence length {t} is not a multiple of 8")


def kernel(x_prompt, x_sample, state_conv, state_lru, state_hgrn, c_prompt, c_sample, w_ada, b_ada, w_in, conv_w,
           conv_b, lru_wa, lru_ba, lru_wx, lru_bx, lru_lambda, hg_lower, hg_norm_w, w_out, w_grp, b_grp, w_rt, b_rt,
           w_gate, w_up, w_down, final_norm_w):
    depth = w_in.shape[0]
    bp, tp, d = x_prompt.shape
    bs, ts, _ = x_sample.shape
    f32 = jnp.float32

    c_all = jnp.concatenate([c_prompt, c_sample], axis=0)
    rows = -(-c_all.shape[0] // SUBLANES) * SUBLANES
    c_all = jnp.pad(c_all, ((0, rows - c_all.shape[0]), (0, 0)))
    mod_all = _mod_call(c_all, w_ada, b_ada).reshape(depth, rows, 6, d)

    p_low = jax.nn.softmax(hg_lower.astype(f32), axis=0)
    lbs = jnp.cumsum(p_low, axis=0) - p_low[0]
    w_in_bf, w_out_bf = _bf(w_in), _bf(w_out)
    wr = jnp.concatenate([w_grp, w_rt], axis=-1)
    wr = jnp.pad(wr, ((0, 0), (0, 0), (0, ROUTE_LANES - wr.shape[-1])))
    wr_hi = _bf(wr)
    wr_lo = _bf(wr - wr_hi.astype(f32))
    br = jnp.concatenate([b_grp, b_rt], axis=-1)
    br = jnp.pad(br, ((0, 0), (0, ROUTE_LANES - br.shape[-1])))
    fw = final_norm_w.reshape(1, d)

    trunks = [
        dict(x=x_prompt, b0=0, nb=bp, t=tp,
             conv=jnp.zeros((depth, bp, CONV_W - 1, LRU_WIDTH), f32), lru=jnp.zeros((depth, bp, LRU_WIDTH), f32),
             hg=jnp.zeros((depth, bp, HG_HEADS, HG_DK, HG_DK), f32)),
        dict(x=x_sample, b0=bp, nb=bs, t=ts, conv=state_conv, lru=state_lru, hg=state_hgrn),
    ]
    for tr in trunks:
        tr["tt"] = _tile_rows(tr["t"])
        tr["chunk"] = min(64, tr["tt"])
        tr.update(convs=[], lrus=[], hgs=[])

    n_rows = sum(tr["nb"] * (tr["t"] // tr["tt"]) * _tile_slots_used(tr["tt"]) for tr in trunks)
    max_tiles = (n_rows + N_EXPERTS * (EXPERT_TILE - 1)) // EXPERT_TILE
    hs_zero = jnp.zeros((max_tiles * EXPERT_TILE, d), jnp.bfloat16)
    tile_ids = jnp.arange(max_tiles, dtype=jnp.int32)

    for l in range(depth):
        wts = (w_in_bf[l], conv_w[l], conv_b[l].reshape(1, -1), _bf(_block_diag_halves(lru_wa[l])),
               _bf(_block_diag_halves(lru_wx[l])), lru_ba[l].reshape(1, -1), lru_bx[l].reshape(1, -1),
               lru_lambda[l].reshape(1, -1), lbs[l].reshape(1, -1), hg_norm_w[l].reshape(1, -1), w_out_bf[l],
               wr_hi[l], wr_lo[l], br[l].reshape(1, -1))
        cnt = jnp.zeros((1, ROUTE_LANES), f32)
        for tr in trunks:
            mod = mod_all[l, tr["b0"]:tr["b0"] + tr["nb"]]
            tr["x"], tr["h2"], tr["route"], tr["tab"], cnt, conv_n, lru_n, hg_n = _mixer_call(
                tr["x"], mod, tr["conv"][l], tr["lru"][l], tr["hg"][l], wts, cnt, tt=tr["tt"], chunk=tr["chunk"])
            tr["mod"] = mod
            tr["convs"].append(conv_n)
            tr["lrus"].append(lru_n.reshape(tr["nb"], LRU_WIDTH))
            tr["hgs"].append(hg_n)

        counts = cnt[0, N_GROUPS:N_GROUPS + N_EXPERTS].astype(jnp.int32)
        tiles_e = (counts + (EXPERT_TILE - 1)) // EXPERT_TILE
        ends = jnp.cumsum(tiles_e)
        off = (ends - tiles_e) * EXPERT_TILE
        n_tiles = ends[-1:]
        tile_expert = jnp.sum(jnp.minimum(tile_ids, n_tiles - 1)[:, None] >= ends[None, :], axis=1).astype(jnp.int32)

        hs = hs_zero
        for tr in trunks:
            tr["tab"] = tr["tab"].reshape((-1,) + tr["tab"].shape[2:])
            tr["route"] = tr["route"].reshape(-1, ROUTE_LANES)
            hs = _dispatch_call(off, tr["tab"], tr["h2"].reshape(-1, d), tr["route"], hs,
                                n_slots=_tile_slots(tr["tt"]))
        ys = _expert_call(tile_expert, n_tiles, hs, w_gate, w_up, w_down, layer=l, te_rows=EXPERT_TILE)
        for tr in trunks:
            x_new = _combine_call(off, tr["tab"], tr["route"], tr["x"].reshape(-1, d), tr["mod"], fw, ys,
                                  n_slots=_tile_slots(tr["tt"]), final=(l == depth - 1))
            tr["x"] = x_new.reshape(tr["nb"], tr["t"], d)

    (yp, cp, lp, hp), (ys_, cs, ls, hs_) = [
        (tr["x"], jnp.stack(tr["convs"]), jnp.stack(tr["lrus"]), jnp.stack(tr["hgs"])) for tr in trunks]
    return (yp, ys_, cp, lp, hp, cs, ls, hs_)
```

```python
import functools

import jax
import jax.numpy as jnp
from jax import lax
from jax.experimental import pallas as pl
from jax.experimental.pallas import tpu as pltpu

D_MODEL = 1024
LRU_WIDTH = 512
LRU_BLOCKS = 8
LRU_C = 8.0
CONV_W = 4
HG_WIDTH = 512
HG_HEADS = 4
HG_DK = 128
N_GROUPS = 4
EXPERTS_PER_GROUP = 8
N_EXPERTS = 32
EXPERT_HIDDEN = 256
EPS = 1e-6

SUBLANES = 8
GRANULE = 16
LANES = 128
HG_SUB = 16
EXP_CLAMP = 80.0
ROUTE_LANES = LANES
EXPERT_TILE = 512
VMEM_LIMIT = 56 * 1024 * 1024

_NT = (((1,), (1,)), ((), ()))
_TN = (((0,), (0,)), ((), ()))


def _bf(x):
    return x.astype(jnp.bfloat16)


def _dot(a, b):
    return jnp.dot(a, b, preferred_element_type=jnp.float32)


def _sigmoid(x):
    return 1.0 / (1.0 + jnp.exp(-x))


def _silu(x):
    return x * _sigmoid(x)


def _gelu_tanh(x):
    return 0.5 * x * (1.0 + jnp.tanh(0.7978845608028654 * (x + 0.044715 * (x * x * x))))


def _softplus(x):
    return jnp.maximum(x, 0.0) + jnp.log(1.0 + jnp.exp(-jnp.abs(x)))


def _rms_rows(x):
    return x * lax.rsqrt(jnp.mean(x * x, axis=-1, keepdims=True) + EPS)


def _scan8(a, b):
    rows = lax.broadcasted_iota(jnp.int32, a.shape, 0)
    for d in (1, 2, 4):
        m = rows >= d
        a_sh = pltpu.roll(a, d, 0)
        b_sh = pltpu.roll(b, d, 0)
        b = jnp.where(m, a * b_sh + b, b)
        a = jnp.where(m, a * a_sh, a)
    return a, b


def _cumsum8(x):
    rows = lax.broadcasted_iota(jnp.int32, x.shape, 0)
    for d in (1, 2, 4):
        x = x + jnp.where(rows >= d, pltpu.roll(x, d, 0), 0.0)
    return x


def _block_refs(b, m):
    n, w = b.shape
    starts, ends = [], []
    for i in range(n // m):
        if i == 0:
            starts.append(jnp.zeros((m, w), jnp.float32))
        else:
            starts.append(jnp.broadcast_to(b[i * m - 1:i * m, :], (m, w)))
        ends.append(jnp.broadcast_to(b[(i + 1) * m - 1:(i + 1) * m, :], (m, w)))
    if len(starts) == 1:
        return starts[0], ends[0]
    return jnp.concatenate(starts, axis=0), jnp.concatenate(ends, axis=0)


def _mixer_kernel(*refs, tt, chunk):
    it = iter(refs)
    x_ref = next(it)
    (mod_ref, conv0_ref, lru0_ref, hg0_ref, w_in_ref, conv_w_ref, conv_b_ref, wa_ref, wx_ref, ba_ref, bx_ref,
     lam_ref, lbs_ref, hgn_ref, w_out_ref, wr_hi_ref, wr_hl_ref, br_ref, tri_ref, upper_ref, cnt_in_ref) = (
        next(it) for _ in range(21))
    (xmid_ref, h2_ref, route_ref, tab_ref, cnt_out_ref, conv_out_ref, lru_out_ref,
     hg_out_ref) = (next(it) for _ in range(8))
    conv_scr, lru_scr, st_scr, cnt_scr = (next(it) for _ in range(4))

    bi = pl.program_id(0)
    j = pl.program_id(1)
    last_b = pl.num_programs(0) - 1
    last_j = pl.num_programs(1) - 1
    pad = SUBLANES - (CONV_W - 1)

    @pl.when((bi == 0) & (j == 0))
    def _():
        cnt_scr[...] = cnt_in_ref[...]

    @pl.when(j == 0)
    def _():
        conv_scr[pad:SUBLANES, :] = conv0_ref[0]
        lru_scr[...] = lru0_ref[0]
        for hd in range(HG_HEADS):
            st_scr[hd] = hg0_ref[0, hd].T

    x = x_ref[0]
    mod = mod_ref[0]
    sh1, sc1, g1, sh2, sc2 = (mod[i:i + 1, :] for i in range(5))

    h = _rms_rows(x) * (1.0 + sc1) + sh1
    h_bf = _bf(h)
    lw, hw = LRU_WIDTH, HG_WIDTH
    proj = lambda lo, width: _dot(h_bf, w_in_ref[:, lo:lo + width])
    xb = proj(0, lw)
    gb = proj(lw, lw)

    conv_scr[SUBLANES:SUBLANES + tt, :] = xb
    xc = conv_b_ref[...]
    for k in range(CONV_W):
        xc = xc + conv_w_ref[k:k + 1, :] * conv_scr[pl.ds(pad + k, tt), :]
    new_hist = conv_scr[pl.ds(tt + pad, CONV_W - 1), :]
    conv_scr[pad:SUBLANES, :] = new_hist
    qh = proj(2 * lw, hw)

    xc_bf = _bf(xc)
    half = lw // 2
    r_pre = jnp.concatenate([_dot(xc_bf[:, i * half:(i + 1) * half], wa_ref[i]) for i in range(2)], axis=1)
    i_pre = jnp.concatenate([_dot(xc_bf[:, i * half:(i + 1) * half], wx_ref[i]) for i in range(2)], axis=1)
    r = _sigmoid(r_pre + ba_ref[...])
    ig = _sigmoid(i_pre + bx_ref[...])
    fh = proj(2 * lw + hw, hw)
    log_a = (-LRU_C) * r * _softplus(-lam_ref[...])
    a = jnp.exp(log_a)
    th = jnp.tanh(log_a)
    one_minus_a2 = (-2.0) * th / (1.0 - th)
    bt = jnp.sqrt(one_minus_a2) * ig * xc
    ih = proj(2 * lw + 2 * hw, hw)

    carry = lru_scr[...]
    hs = []
    for g in range(tt // SUBLANES):
        sl = slice(g * SUBLANES, (g + 1) * SUBLANES)
        acum, hloc = _scan8(a[sl], bt[sl])
        hg = acum * carry + hloc
        hs.append(hg)
        carry = hg[SUBLANES - 1:SUBLANES, :]
    lru_scr[...] = carry
    gh = proj(2 * lw + 3 * hw, hw)
    hl = jnp.concatenate(hs, axis=0)
    y_lru = hl * _gelu_tanh(gb)
    mix_lru = _dot(_bf(y_lru), w_out_ref[:lw, :])

    lbs = lbs_ref[...]
    q = _silu(qh) * (HG_DK ** -0.5)
    f = lbs + (1.0 - lbs) * _sigmoid(fh)
    kk = 1.0 - f
    glog = jnp.log(f)

    ti = lax.broadcasted_iota(jnp.int32, (chunk, chunk), 0)
    si = lax.broadcasted_iota(jnp.int32, (chunk, chunk), 1)
    sub_shift = HG_SUB.bit_length() - 1
    mask_diag = ((ti >> sub_shift) == (si >> sub_shift)) & (ti >= si)
    levels = []
    m = HG_SUB
    while m < chunk:
        sh = m.bit_length() - 1
        levels.append((m, ((ti >> (sh + 1)) == (si >> (sh + 1))) & (((ti >> sh) & 1) == 1) & (((si >> sh) & 1) == 0)))
        m *= 2

    o_chunks = []
    for c in range(tt // chunk):
        rs = slice(c * chunk, (c + 1) * chunk)
        gc = glog[rs]
        bs, bcarry = [], None
        for g in range(chunk // SUBLANES):
            cs = _cumsum8(gc[g * SUBLANES:(g + 1) * SUBLANES])
            if bcarry is not None:
                cs = cs + bcarry
            bs.append(cs)
            bcarry = cs[SUBLANES - 1:SUBLANES, :]
        b = jnp.concatenate(bs, axis=0)
        b_last = bcarry
        qc, kc, vc = q[rs], kk[rs], ih[rs]

        st16, en16 = _block_refs(b, HG_SUB)
        q_lv = {HG_SUB: _bf(qc * jnp.exp(b - st16))}
        k_lv = {HG_SUB: _bf(kc * jnp.exp(en16 - b))}
        k_diag = _bf(kc * jnp.exp(jnp.minimum(st16 - b, EXP_CLAMP)))
        for m, _ in levels:
            if m == HG_SUB:
                continue
            st, en = _block_refs(b, m)
            q_lv[m] = _bf(qc * jnp.exp(b - st))
            k_lv[m] = _bf(kc * jnp.exp(en - b))
        q_all = _bf(qc * jnp.exp(b))
        k_all = _bf(kc * jnp.exp(b_last - b))
        v_bf = _bf(vc)
        eb_last = jnp.exp(b_last)

        heads = [slice(hd * HG_DK, (hd + 1) * HG_DK) for hd in range(HG_HEADS)]
        nt_dot = lambda lhs, rhs: lax.dot_general(lhs, rhs, _NT, preferred_element_type=jnp.float32)
        a_diag = [nt_dot(q_lv[HG_SUB][:, hs_], k_diag[:, hs_]) for hs_ in heads]
        a_lvl = [[nt_dot(q_lv[m][:, hs_], k_lv[m][:, hs_]) for m, _ in levels] for hs_ in heads]
        st_old = [st_scr[hd] for hd in range(HG_HEADS)]
        o_inter = [nt_dot(q_all[:, hs_], _bf(st_t)) for hs_, st_t in zip(heads, st_old)]
        kv = [lax.dot_general(v_bf[:, hs_], k_all[:, hs_], _TN, preferred_element_type=jnp.float32)
              for hs_ in heads]
        amats = []
        for hd in range(HG_HEADS):
            amat = jnp.where(mask_diag, a_diag[hd], 0.0)
            for (m, msk), a_m in zip(levels, a_lvl[hd]):
                amat = amat + jnp.where(msk, a_m, 0.0)
            amats.append(_bf(amat))
        o_intra = [_dot(amat, v_bf[:, hs_]) for amat, hs_ in zip(amats, heads)]
        o_heads = []
        for hd, hs_ in enumerate(heads):
            st_scr[hd] = st_old[hd] * eb_last[:, hs_] + kv[hd]
            o_heads.append(_rms_rows(o_inter[hd] + o_intra[hd]) * hgn_ref[...])
        o_chunks.append(jnp.concatenate(o_heads, axis=1))
    o_all = o_chunks[0] if len(o_chunks) == 1 else jnp.concatenate(o_chunks, axis=0)
    y_hg = o_all * _silu(gh)

    mix = mix_lru + _dot(_bf(y_hg), w_out_ref[lw:, :])
    x_mid = x + g1 * mix
    xmid_ref[0] = x_mid

    h2 = _rms_rows(x_mid) * (1.0 + sc2) + sh2
    h2_hi = _bf(h2)
    h2_lo = _bf(h2 - h2_hi.astype(jnp.float32))
    h2_ref[0] = h2_hi
    hi_terms = _dot(h2_hi, wr_hl_ref[...])
    logits = (hi_terms[:, :ROUTE_LANES] + hi_terms[:, ROUTE_LANES:] + _dot(h2_lo, wr_hi_ref[...])
              + br_ref[...])

    lane = lax.broadcasted_iota(jnp.int32, (tt, ROUTE_LANES), 1)
    lane_f = lane.astype(jnp.float32)
    neg, big = -1e30, 1e6
    is_grp = lane < N_GROUPS
    gl = jnp.where(is_grp, logits, neg)
    gmax = jnp.max(gl, axis=-1, keepdims=True)
    gsel = jnp.min(jnp.where(gl == gmax, lane_f, big), axis=-1, keepdims=True)
    gsum = jnp.sum(jnp.where(is_grp, jnp.exp(gl - gmax), 0.0), axis=-1, keepdims=True)
    p_grp = 1.0 / gsum
    lo = N_GROUPS + gsel * EXPERTS_PER_GROUP
    emask = (lane_f >= lo) & (lane_f < lo + EXPERTS_PER_GROUP)
    el = jnp.where(emask, logits, neg)
    tv1 = jnp.max(el, axis=-1, keepdims=True)
    ti1 = jnp.min(jnp.where(emask & (el == tv1), lane_f, big), axis=-1, keepdims=True)
    emask2 = emask & (lane_f != ti1)
    el2 = jnp.where(emask2, logits, neg)
    tv2 = jnp.max(el2, axis=-1, keepdims=True)
    ti2 = jnp.min(jnp.where(emask2 & (el2 == tv2), lane_f, big), axis=-1, keepdims=True)
    e21 = jnp.exp(tv2 - tv1)
    w1 = p_grp / (1.0 + e21)
    w2 = w1 * e21
    hot1 = lane_f == ti1
    hot2 = lane_f == ti2
    hot12 = jnp.where(hot1 | hot2, 1.0, 0.0)
    before = _dot(tri_ref[...], _bf(hot12))
    c_tile = before[tt - 1:tt, :] + hot12[tt - 1:tt, :]
    c_pad = jnp.floor((c_tile + (GRANULE - 1.0)) * (1.0 / GRANULE)) * GRANULE
    lstart = _dot(_bf(jnp.broadcast_to(c_pad, (SUBLANES, ROUTE_LANES))), upper_ref[...])[0:1, :]
    pos = before + lstart
    slot1 = jnp.sum(jnp.where(hot1, pos, 0.0), axis=-1, keepdims=True)
    slot2 = jnp.sum(jnp.where(hot2, pos, 0.0), axis=-1, keepdims=True)
    gbase = cnt_scr[...]
    cnt_scr[...] = gbase + c_pad
    tab_ref[0, 0] = jnp.concatenate(
        [lstart, c_pad, gbase, jnp.zeros((SUBLANES - 3, ROUTE_LANES), jnp.float32)], axis=0).astype(jnp.int32)
    route_ref[0] = jnp.where(lane == 0, w1, jnp.where(lane == 1, w2, jnp.where(lane == 2, slot1,
                                                                                jnp.where(lane == 3, slot2, 0.0))))

    @pl.when(j == last_j)
    def _():
        conv_out_ref[0] = new_hist
        lru_out_ref[0] = carry
        for hd in range(HG_HEADS):
            hg_out_ref[0, hd] = st_scr[hd].T

    @pl.when((bi == last_b) & (j == last_j))
    def _():
        cnt_out_ref[...] = cnt_scr[...]


def _mixer_call(x, mod, conv0, lru0, hg0, wts, cnt_in, *, tt, chunk):
    bsz, t, d = x.shape
    grid = (bsz, t // tt)
    tile = lambda last: pl.BlockSpec((1, tt, last), lambda b, j: (b, j, 0))
    per_b = lambda *shape: pl.BlockSpec((1,) + shape, lambda b, j: (b,) + (0,) * len(shape))
    full = lambda a: pl.BlockSpec(a.shape, lambda b, j: (0,) * a.ndim)
    tri = _bf(jnp.tril(jnp.ones((tt, tt), jnp.float32), -1))
    upper = _bf(jnp.triu(jnp.ones((ROUTE_LANES, ROUTE_LANES), jnp.float32), 1))
    per_tile = lambda *shape: pl.BlockSpec((1, 1) + shape, lambda b, j: (b, j) + (0,) * len(shape))

    args, in_specs = [x], [tile(d)]
    args += [mod, conv0, lru0.reshape(bsz, 1, LRU_WIDTH), hg0]
    in_specs += [per_b(6, d), per_b(CONV_W - 1, LRU_WIDTH), per_b(1, LRU_WIDTH), per_b(HG_HEADS, HG_DK, HG_DK)]
    args += list(wts) + [tri, upper, cnt_in]
    in_specs += [full(w) for w in wts] + [full(tri), full(upper), full(cnt_in)]

    out_shape = (
        jax.ShapeDtypeStruct((bsz, t, d), jnp.float32),
        jax.ShapeDtypeStruct((bsz, t, d), jnp.bfloat16),
        jax.ShapeDtypeStruct((bsz, t, ROUTE_LANES), jnp.float32),
        jax.ShapeDtypeStruct((bsz, t // tt, SUBLANES, ROUTE_LANES), jnp.int32),
        jax.ShapeDtypeStruct((1, ROUTE_LANES), jnp.float32),
        jax.ShapeDtypeStruct((bsz, CONV_W - 1, LRU_WIDTH), jnp.float32),
        jax.ShapeDtypeStruct((bsz, 1, LRU_WIDTH), jnp.float32),
        jax.ShapeDtypeStruct((bsz, HG_HEADS, HG_DK, HG_DK), jnp.float32),
    )
    out_specs = (tile(d), tile(d), tile(ROUTE_LANES), per_tile(SUBLANES, ROUTE_LANES),
                 pl.BlockSpec((1, ROUTE_LANES), lambda b, j: (0, 0)),
                 per_b(CONV_W - 1, LRU_WIDTH), per_b(1, LRU_WIDTH), per_b(HG_HEADS, HG_DK, HG_DK))
    scratch = [
        pltpu.VMEM((tt + SUBLANES, LRU_WIDTH), jnp.float32),
        pltpu.VMEM((1, LRU_WIDTH), jnp.float32),
        pltpu.VMEM((HG_HEADS, HG_DK, HG_DK), jnp.float32),
        pltpu.VMEM((1, ROUTE_LANES), jnp.float32),
    ]
    return pl.pallas_call(
        functools.partial(_mixer_kernel, tt=tt, chunk=chunk),
        grid=grid, in_specs=in_specs, out_specs=out_specs, out_shape=out_shape, scratch_shapes=scratch,
        compiler_params=pltpu.CompilerParams(dimension_semantics=("arbitrary", "arbitrary"),
                                             vmem_limit_bytes=VMEM_LIMIT),
        name="mixer",
    )(*args)


def _mod_kernel(c_ref, w_ref, b_ref, o_ref):
    s = _silu(c_ref[...])
    o_ref[0] = jnp.dot(s, w_ref[0], preferred_element_type=jnp.float32,
                       precision=lax.Precision.HIGHEST) + b_ref[0]


def _mod_call(c_all, w_ada, b_ada):
    depth, d, n = w_ada.shape
    rows = c_all.shape[0]
    tn = d
    return pl.pallas_call(
        _mod_kernel,
        grid=(depth, n // tn),
        in_specs=[pl.BlockSpec((rows, d), lambda l, i: (0, 0)),
                  pl.BlockSpec((1, d, tn), lambda l, i: (l, 0, i)),
                  pl.BlockSpec((1, 1, tn), lambda l, i: (l, 0, i))],
        out_specs=pl.BlockSpec((1, rows, tn), lambda l, i: (l, 0, i)),
        out_shape=jax.ShapeDtypeStruct((depth, rows, n), jnp.float32),
        compiler_params=pltpu.CompilerParams(dimension_semantics=("arbitrary", "arbitrary"),
                                             vmem_limit_bytes=VMEM_LIMIT),
        name="modulation",
    )(c_all, w_ada, b_ada.reshape(depth, 1, n))


def _for_each_granule(off_ref, tab_ref, fn):
    def per_expert(e, total):
        lane = N_GROUPS + e
        n_gran = lax.shift_right_logical(tab_ref[0, 1, lane], GRANULE.bit_length() - 1)
        src0 = tab_ref[0, 0, lane]
        dst0 = off_ref[e] + tab_ref[0, 2, lane]

        def per_granule(g, c):
            fn(pl.multiple_of(src0 + g * GRANULE, GRANULE), pl.multiple_of(dst0 + g * GRANULE, GRANULE))
            return c

        lax.fori_loop(0, n_gran, per_granule, 0)
        return total + n_gran

    return lax.fori_loop(0, N_EXPERTS, per_expert, 0)


def _dispatch_kernel(off_ref, tab_ref, h_ref, route_ref, hs_in_ref, hs_ref, srt, sem, pend):
    del hs_in_ref
    i = pl.program_id(0)
    cur = lax.rem(i, 2)
    tt, n_slots = h_ref.shape[0], srt.shape[1]

    def granule_copy(buf, src, dst):
        return pltpu.make_async_copy(srt.at[buf, pl.ds(src, GRANULE), :],
                                     hs_ref.at[pl.ds(dst, GRANULE), :], sem.at[buf])

    def drain(buf, n):
        lax.fori_loop(0, n, lambda g, c: (granule_copy(buf, 0, 0).wait(), c)[1], 0)

    @pl.when(i == 0)
    def _():
        pend[0] = 0

    rec = route_ref[...]
    if tt < LANES:
        rec = jnp.concatenate([rec, jnp.zeros((LANES - tt, ROUTE_LANES), jnp.float32)], axis=0)
    rec_t = rec.T
    s1_row, s2_row = rec_t[2:3, :tt], rec_t[3:4, :tt]
    slot_iota = lax.broadcasted_iota(jnp.int32, (n_slots, tt), 0).astype(jnp.float32)
    perm = _bf(jnp.where((slot_iota == s1_row) | (slot_iota == s2_row), 1.0, 0.0))
    srt[cur] = _bf(_dot(perm, h_ref[...]))

    earlier = pend[0]
    pend[0] = _for_each_granule(off_ref, tab_ref, lambda s, d: granule_copy(cur, s, d).start())
    drain(1 - cur, earlier)

    @pl.when(i == pl.num_programs(0) - 1)
    def _():
        drain(cur, pend[0])


def _dispatch_call(off, tab, h2, route, hs, *, n_slots):
    n = tab.shape[0]
    tt = h2.shape[0] // n
    d = h2.shape[1]
    grid_spec = pltpu.PrefetchScalarGridSpec(
        num_scalar_prefetch=1, grid=(n,),
        in_specs=[pl.BlockSpec((1, SUBLANES, ROUTE_LANES), lambda i, off: (i, 0, 0), memory_space=pltpu.SMEM),
                  pl.BlockSpec((tt, d), lambda i, off: (i, 0)),
                  pl.BlockSpec((tt, ROUTE_LANES), lambda i, off: (i, 0)),
                  pl.BlockSpec(memory_space=pl.ANY)],
        out_specs=pl.BlockSpec(memory_space=pl.ANY),
        scratch_shapes=[pltpu.VMEM((2, n_slots, d), jnp.bfloat16), pltpu.SemaphoreType.DMA((2,)),
                        pltpu.SMEM((1,), jnp.int32)])
    return pl.pallas_call(
        _dispatch_kernel,
        grid_spec=grid_spec,
        out_shape=jax.ShapeDtypeStruct(hs.shape, hs.dtype),
        input_output_aliases={4: 0},
        compiler_params=pltpu.CompilerParams(dimension_semantics=("arbitrary",), vmem_limit_bytes=VMEM_LIMIT),
        name="dispatch",
    )(off, tab, h2, route, hs)


def _expert_kernel(te_ref, nt_ref, hs_ref, wg_ref, wu_ref, wd_ref, ys_ref, wgu_bf, wd_bf):
    i = pl.program_id(0)
    active = i < nt_ref[0]

    @pl.when(active & ((i == 0) | (te_ref[i] != te_ref[jnp.maximum(i - 1, 0)])))
    def _():
        wgu_bf[:, :EXPERT_HIDDEN] = _bf(wg_ref[0, 0])
        wgu_bf[:, EXPERT_HIDDEN:] = _bf(wu_ref[0, 0])
        wd_bf[...] = _bf(wd_ref[0, 0])

    @pl.when(active)
    def _():
        gu = _dot(hs_ref[...], wgu_bf[...])
        hid = _silu(gu[:, :EXPERT_HIDDEN]) * gu[:, EXPERT_HIDDEN:]
        ys_ref[...] = _bf(_dot(_bf(hid), wd_bf[...]))

    @pl.when(jnp.logical_not(active))
    def _():
        ys_ref[...] = jnp.zeros_like(ys_ref)


def _expert_call(tile_expert, n_tiles, hs, w_gate, w_up, w_down, *, layer, te_rows):
    npad, d = hs.shape
    wspec = lambda a: pl.BlockSpec((1, 1) + a.shape[2:], lambda i, te, nt: (layer, te[i], 0, 0))
    grid_spec = pltpu.PrefetchScalarGridSpec(
        num_scalar_prefetch=2, grid=(npad // te_rows,),
        in_specs=[pl.BlockSpec((te_rows, d), lambda i, te, nt: (i, 0)), wspec(w_gate), wspec(w_up), wspec(w_down)],
        out_specs=pl.BlockSpec((te_rows, d), lambda i, te, nt: (i, 0)),
        scratch_shapes=[pltpu.VMEM((d, 2 * EXPERT_HIDDEN), jnp.bfloat16),
                        pltpu.VMEM((EXPERT_HIDDEN, d), jnp.bfloat16)])
    return pl.pallas_call(
        _expert_kernel,
        grid_spec=grid_spec,
        out_shape=jax.ShapeDtypeStruct((npad, d), jnp.bfloat16),
        compiler_params=pltpu.CompilerParams(dimension_semantics=("arbitrary",), vmem_limit_bytes=VMEM_LIMIT),
        name="experts",
    )(tile_expert, n_tiles, hs, w_gate, w_up, w_down)


def _combine_kernel(off_ref, tab_ref, tab_next_ref, route_ref, x_ref, mod_ref, fw_ref, ys_ref, o_ref, yloc, sem,
                    pend, *, final):
    i = pl.program_id(0)
    last = pl.num_programs(0) - 1
    cur = lax.rem(i, 2)

    def granule_copy(buf, loc, srt):
        return pltpu.make_async_copy(ys_ref.at[pl.ds(srt, GRANULE), :],
                                     yloc.at[buf, pl.ds(loc, GRANULE), :], sem.at[buf])

    def gather(tab, buf):
        return _for_each_granule(off_ref, tab, lambda s, d: granule_copy(buf, s, d).start())

    @pl.when(i == 0)
    def _():
        yloc[...] = jnp.zeros_like(yloc)
        pend[0] = gather(tab_ref, 0)

    n_cur = pend[0]

    @pl.when(i < last)
    def _():
        pend[0] = gather(tab_next_ref, 1 - cur)

    lax.fori_loop(0, n_cur, lambda g, c: (granule_copy(cur, 0, 0).wait(), c)[1], 0)

    rt = route_ref[...]
    tt, n_slots = rt.shape[0], yloc.shape[1]
    slot = lax.broadcasted_iota(jnp.int32, (tt, n_slots), 1).astype(jnp.float32)
    wc = jnp.where(slot == rt[:, 2:3], rt[:, 0:1], 0.0) + jnp.where(slot == rt[:, 3:4], rt[:, 1:2], 0.0)
    x_new = x_ref[...] + mod_ref[0, 5:6, :] * _dot(_bf(wc), yloc[cur])
    o_ref[...] = _rms_rows(x_new) * fw_ref[...] if final else x_new


def _combine_call(off, tab, route, x_mid, mod, fw, ys, *, n_slots, final):
    n = tab.shape[0]
    tt = route.shape[0] // n
    tiles_per_seq = n // mod.shape[0]
    d = ys.shape[-1]
    tab_spec = lambda imap: pl.BlockSpec((1, SUBLANES, ROUTE_LANES), imap, memory_space=pltpu.SMEM)
    grid_spec = pltpu.PrefetchScalarGridSpec(
        num_scalar_prefetch=1, grid=(n,),
        in_specs=[tab_spec(lambda i, off: (i, 0, 0)),
                  tab_spec(lambda i, off: (jnp.minimum(i + 1, n - 1), 0, 0)),
                  pl.BlockSpec((tt, ROUTE_LANES), lambda i, off: (i, 0)),
                  pl.BlockSpec((tt, d), lambda i, off: (i, 0)),
                  pl.BlockSpec((1,) + mod.shape[1:], lambda i, off: (i // tiles_per_seq, 0, 0)),
                  pl.BlockSpec((1, d), lambda i, off: (0, 0)),
                  pl.BlockSpec(memory_space=pl.ANY)],
        out_specs=pl.BlockSpec((tt, d), lambda i, off: (i, 0)),
        scratch_shapes=[pltpu.VMEM((2, n_slots, d), jnp.bfloat16), pltpu.SemaphoreType.DMA((2,)),
                        pltpu.SMEM((1,), jnp.int32)])
    return pl.pallas_call(
        functools.partial(_combine_kernel, final=final),
        grid_spec=grid_spec,
        out_shape=jax.ShapeDtypeStruct((n * tt, d), jnp.float32),
        compiler_params=pltpu.CompilerParams(dimension_semantics=("arbitrary",), vmem_limit_bytes=VMEM_LIMIT),
        name="combine",
    )(off, tab, tab, route, x_mid, mod, fw, ys)


def _block_diag_halves(w):
    nb, bd, _ = w.shape
    per = nb // 2
    eye = jnp.eye(per, dtype=w.dtype)
    halves = [jnp.einsum('nij,nm->nimj', w[h * per:(h + 1) * per], eye).reshape(per * bd, per * bd) for h in range(2)]
    return jnp.stack(halves)


def _tile_slots(tt):
    return -(-_tile_slots_used(tt) // LANES) * LANES


def _tile_slots_used(tt):
    return 2 * tt + (GRANULE - 1) * min(N_EXPERTS, 2 * tt)


def _tile_rows(t):
    for cand in (512, 256, 128, 64, 32, 16, 8):
        if t % cand == 0:
            return cand
    raise ValueError(f"sequence length {t} is not a multiple of 8")


def kernel(x_prompt, x_sample, state_conv, state_lru, state_hgrn, c_prompt, c_sample, w_ada, b_ada, w_in, conv_w,
           conv_b, lru_wa, lru_ba, lru_wx, lru_bx, lru_lambda, hg_lower, hg_norm_w, w_out, w_grp, b_grp, w_rt, b_rt,
           w_gate, w_up, w_down, final_norm_w):
    depth = w_in.shape[0]
    bp, tp, d = x_prompt.shape
    bs, ts, _ = x_sample.shape
    f32 = jnp.float32

    c_all = jnp.concatenate([c_prompt, c_sample], axis=0)
    rows = -(-c_all.shape[0] // SUBLANES) * SUBLANES
    c_all = jnp.pad(c_all, ((0, rows - c_all.shape[0]), (0, 0)))
    mod_all = _mod_call(c_all, w_ada, b_ada).reshape(depth, rows, 6, d)

    p_low = jax.nn.softmax(hg_lower.astype(f32), axis=0)
    lbs = jnp.cumsum(p_low, axis=0) - p_low[0]
    w_in_bf, w_out_bf = _bf(w_in), _bf(w_out)
    wr = jnp.concatenate([w_grp, w_rt], axis=-1)
    wr = jnp.pad(wr, ((0, 0), (0, 0), (0, ROUTE_LANES - wr.shape[-1])))
    wr_hi = _bf(wr)
    wr_hl = jnp.concatenate([wr_hi, _bf(wr - wr_hi.astype(f32))], axis=-1)
    br = jnp.concatenate([b_grp, b_rt], axis=-1)
    br = jnp.pad(br, ((0, 0), (0, ROUTE_LANES - br.shape[-1])))
    fw = final_norm_w.reshape(1, d)

    trunks = [
        dict(x=x_prompt, b0=0, nb=bp, t=tp,
             conv=jnp.zeros((depth, bp, CONV_W - 1, LRU_WIDTH), f32), lru=jnp.zeros((depth, bp, LRU_WIDTH), f32),
             hg=jnp.zeros((depth, bp, HG_HEADS, HG_DK, HG_DK), f32)),
        dict(x=x_sample, b0=bp, nb=bs, t=ts, conv=state_conv, lru=state_lru, hg=state_hgrn),
    ]
    for tr in trunks:
        tr["tt"] = _tile_rows(tr["t"])
        tr["chunk"] = min(64, tr["tt"])
        tr.update(convs=[], lrus=[], hgs=[])

    n_rows = sum(tr["nb"] * (tr["t"] // tr["tt"]) * _tile_slots_used(tr["tt"]) for tr in trunks)
    max_tiles = (n_rows + N_EXPERTS * (EXPERT_TILE - 1)) // EXPERT_TILE
    hs_zero = jnp.zeros((max_tiles * EXPERT_TILE, d), jnp.bfloat16)
    tile_ids = jnp.arange(max_tiles, dtype=jnp.int32)

    for l in range(depth):
        wts = (w_in_bf[l], conv_w[l], conv_b[l].reshape(1, -1), _bf(_block_diag_halves(lru_wa[l])),
               _bf(_block_diag_halves(lru_wx[l])), lru_ba[l].reshape(1, -1), lru_bx[l].reshape(1, -1),
               lru_lambda[l].reshape(1, -1), lbs[l].reshape(1, -1), hg_norm_w[l].reshape(1, -1), w_out_bf[l],
               wr_hi[l], wr_hl[l], br[l].reshape(1, -1))
        cnt = jnp.zeros((1, ROUTE_LANES), f32)
        for tr in trunks:
            mod = mod_all[l, tr["b0"]:tr["b0"] + tr["nb"]]
            tr["x"], tr["h2"], tr["route"], tr["tab"], cnt, conv_n, lru_n, hg_n = _mixer_call(
                tr["x"], mod, tr["conv"][l], tr["lru"][l], tr["hg"][l], wts, cnt, tt=tr["tt"], chunk=tr["chunk"])
            tr["mod"] = mod
            tr["convs"].append(conv_n)
            tr["lrus"].append(lru_n.reshape(tr["nb"], LRU_WIDTH))
            tr["hgs"].append(hg_n)

        counts = cnt[0, N_GROUPS:N_GROUPS + N_EXPERTS].astype(jnp.int32)
        tiles_e = (counts + (EXPERT_TILE - 1)) // EXPERT_TILE
        ends = jnp.cumsum(tiles_e)
        off = (ends - tiles_e) * EXPERT_TILE
        n_tiles = ends[-1:]
        tile_expert = jnp.sum(jnp.minimum(tile_ids, n_tiles - 1)[:, None] >= ends[None, :], axis=1).astype(jnp.int32)

        hs = hs_zero
        for tr in trunks:
            tr["tab"] = tr["tab"].reshape((-1,) + tr["tab"].shape[2:])
            tr["route"] = tr["route"].reshape(-1, ROUTE_LANES)
            hs = _dispatch_call(off, tr["tab"], tr["h2"].reshape(-1, d), tr["route"], hs,
                                n_slots=_tile_slots(tr["tt"]))
        ys = _expert_call(tile_expert, n_tiles, hs, w_gate, w_up, w_down, layer=l, te_rows=EXPERT_TILE)
        for tr in trunks:
            x_new = _combine_call(off, tr["tab"], tr["route"], tr["x"].reshape(-1, d), tr["mod"], fw, ys,
                                  n_slots=_tile_slots(tr["tt"]), final=(l == depth - 1))
            tr["x"] = x_new.reshape(tr["nb"], tr["t"], d)

    (yp, cp, lp, hp), (ys_, cs, ls, hs_) = [
        (tr["x"], jnp.stack(tr["convs"]), jnp.stack(tr["lrus"]), jnp.stack(tr["hgs"])) for tr in trunks]
    return (yp, ys_, cp, lp, hp, cs, ls, hs_)
```

```python
import functools

import jax
import jax.numpy as jnp
from jax import lax
from jax.experimental import pallas as pl
from jax.experimental.pallas import tpu as pltpu

D_MODEL = 1024
LRU_WIDTH = 512
LRU_BLOCKS = 8
LRU_C = 8.0
CONV_W = 4
HG_WIDTH = 512
HG_HEADS = 4
HG_DK = 128
N_GROUPS = 4
EXPERTS_PER_GROUP = 8
N_EXPERTS = 32
EXPERT_HIDDEN = 256
EPS = 1e-6

SUBLANES = 8
GRANULE = 16
LANES = 128
HG_SUB = 16
EXP_CLAMP = 80.0
ROUTE_LANES = LANES
EXPERT_TILE = 512
VMEM_LIMIT = 56 * 1024 * 1024

_NT = (((1,), (1,)), ((), ()))
_TN = (((0,), (0,)), ((), ()))


def _bf(x):
    return x.astype(jnp.bfloat16)


def _dot(a, b):
    return jnp.dot(a, b, preferred_element_type=jnp.float32)


def _sigmoid(x):
    return 1.0 / (1.0 + jnp.exp(-x))


def _silu(x):
    return x * _sigmoid(x)


def _gelu_tanh(x):
    return 0.5 * x * (1.0 + jnp.tanh(0.7978845608028654 * (x + 0.044715 * (x * x * x))))


def _softplus(x):
    return jnp.maximum(x, 0.0) + jnp.log(1.0 + jnp.exp(-jnp.abs(x)))


def _rms_rows(x):
    return x * lax.rsqrt(jnp.mean(x * x, axis=-1, keepdims=True) + EPS)


def _scan8(a, b):
    rows = lax.broadcasted_iota(jnp.int32, a.shape, 0)
    for d in (1, 2, 4):
        m = rows >= d
        a_sh = pltpu.roll(a, d, 0)
        b_sh = pltpu.roll(b, d, 0)
        b = jnp.where(m, a * b_sh + b, b)
        a = jnp.where(m, a * a_sh, a)
    return a, b


def _cumsum8(x):
    rows = lax.broadcasted_iota(jnp.int32, x.shape, 0)
    for d in (1, 2, 4):
        x = x + jnp.where(rows >= d, pltpu.roll(x, d, 0), 0.0)
    return x


def _block_refs(b, m):
    n, w = b.shape
    starts, ends = [], []
    for i in range(n // m):
        if i == 0:
            starts.append(jnp.zeros((m, w), jnp.float32))
        else:
            starts.append(jnp.broadcast_to(b[i * m - 1:i * m, :], (m, w)))
        ends.append(jnp.broadcast_to(b[(i + 1) * m - 1:(i + 1) * m, :], (m, w)))
    if len(starts) == 1:
        return starts[0], ends[0]
    return jnp.concatenate(starts, axis=0), jnp.concatenate(ends, axis=0)


def _mixer_kernel(*refs, tt, chunk):
    it = iter(refs)
    x_ref = next(it)
    (mod_ref, conv0_ref, lru0_ref, hg0_ref, w_in_ref, conv_w_ref, conv_b_ref, wa_ref, wx_ref, ba_ref, bx_ref,
     lam_ref, lbs_ref, hgn_ref, w_out_ref, wr_hi_ref, wr_hl_ref, br_ref, tri_ref, upper_ref, cnt_in_ref) = (
        next(it) for _ in range(21))
    (xmid_ref, h2_ref, route_ref, tab_ref, cnt_out_ref, conv_out_ref, lru_out_ref,
     hg_out_ref) = (next(it) for _ in range(8))
    conv_scr, lru_scr, st_scr, cnt_scr = (next(it) for _ in range(4))

    bi = pl.program_id(0)
    j = pl.program_id(1)
    last_b = pl.num_programs(0) - 1
    last_j = pl.num_programs(1) - 1
    pad = SUBLANES - (CONV_W - 1)

    @pl.when((bi == 0) & (j == 0))
    def _():
        cnt_scr[...] = cnt_in_ref[...]

    @pl.when(j == 0)
    def _():
        conv_scr[pad:SUBLANES, :] = conv0_ref[0]
        lru_scr[...] = lru0_ref[0]
        for hd in range(HG_HEADS):
            st_scr[hd] = hg0_ref[0, hd].T

    x = x_ref[0]
    mod = mod_ref[0]
    sh1, sc1, g1, sh2, sc2 = (mod[i:i + 1, :] for i in range(5))

    h = _rms_rows(x) * (1.0 + sc1) + sh1
    h_bf = _bf(h)
    lw, hw = LRU_WIDTH, HG_WIDTH
    proj = lambda lo, width: _dot(h_bf, w_in_ref[:, lo:lo + width])
    xb = proj(0, lw)
    gb = proj(lw, lw)

    conv_scr[SUBLANES:SUBLANES + tt, :] = xb
    xc = conv_b_ref[...]
    for k in range(CONV_W):
        xc = xc + conv_w_ref[k:k + 1, :] * conv_scr[pl.ds(pad + k, tt), :]
    new_hist = conv_scr[pl.ds(tt + pad, CONV_W - 1), :]
    conv_scr[pad:SUBLANES, :] = new_hist
    qh = proj(2 * lw, hw)

    xc_bf = _bf(xc)
    half = lw // 2
    r_pre = jnp.concatenate([_dot(xc_bf[:, i * half:(i + 1) * half], wa_ref[i]) for i in range(2)], axis=1)
    i_pre = jnp.concatenate([_dot(xc_bf[:, i * half:(i + 1) * half], wx_ref[i]) for i in range(2)], axis=1)
    r = _sigmoid(r_pre + ba_ref[...])
    ig = _sigmoid(i_pre + bx_ref[...])
    fh = proj(2 * lw + hw, hw)
    log_a = (-LRU_C) * r * _softplus(-lam_ref[...])
    a = jnp.exp(log_a)
    th = jnp.tanh(log_a)
    one_minus_a2 = (-2.0) * th / (1.0 - th)
    bt = jnp.sqrt(one_minus_a2) * ig * xc
    ih = proj(2 * lw + 2 * hw, hw)

    carry = lru_scr[...]
    hs = []
    for g in range(tt // SUBLANES):
        sl = slice(g * SUBLANES, (g + 1) * SUBLANES)
        acum, hloc = _scan8(a[sl], bt[sl])
        hg = acum * carry + hloc
        hs.append(hg)
        carry = hg[SUBLANES - 1:SUBLANES, :]
    lru_scr[...] = carry
    gh = proj(2 * lw + 3 * hw, hw)
    hl = jnp.concatenate(hs, axis=0)
    y_lru = hl * _gelu_tanh(gb)
    mix_lru = _dot(_bf(y_lru), w_out_ref[:lw, :])

    lbs = lbs_ref[...]
    q = _silu(qh) * (HG_DK ** -0.5)
    f = lbs + (1.0 - lbs) * _sigmoid(fh)
    kk = 1.0 - f
    glog = jnp.log(f)

    ti = lax.broadcasted_iota(jnp.int32, (chunk, chunk), 0)
    si = lax.broadcasted_iota(jnp.int32, (chunk, chunk), 1)
    sub_shift = HG_SUB.bit_length() - 1
    mask_diag = ((ti >> sub_shift) == (si >> sub_shift)) & (ti >= si)
    levels = []
    m = HG_SUB
    while m < chunk:
        sh = m.bit_length() - 1
        levels.append((m, ((ti >> (sh + 1)) == (si >> (sh + 1))) & (((ti >> sh) & 1) == 1) & (((si >> sh) & 1) == 0)))
        m *= 2

    o_chunks = []
    for c in range(tt // chunk):
        rs = slice(c * chunk, (c + 1) * chunk)
        gc = glog[rs]
        bs, bcarry = [], None
        for g in range(chunk // SUBLANES):
            cs = _cumsum8(gc[g * SUBLANES:(g + 1) * SUBLANES])
            if bcarry is not None:
                cs = cs + bcarry
            bs.append(cs)
            bcarry = cs[SUBLANES - 1:SUBLANES, :]
        b = jnp.concatenate(bs, axis=0)
        b_last = bcarry
        qc, kc, vc = q[rs], kk[rs], ih[rs]

        st16, en16 = _block_refs(b, HG_SUB)
        q_lv = {HG_SUB: _bf(qc * jnp.exp(b - st16))}
        k_lv = {HG_SUB: _bf(kc * jnp.exp(en16 - b))}
        k_diag = _bf(kc * jnp.exp(jnp.minimum(st16 - b, EXP_CLAMP)))
        for m, _ in levels:
            if m == HG_SUB:
                continue
            st, en = _block_refs(b, m)
            q_lv[m] = _bf(qc * jnp.exp(b - st))
            k_lv[m] = _bf(kc * jnp.exp(en - b))
        q_all = _bf(qc * jnp.exp(b))
        k_all = _bf(kc * jnp.exp(b_last - b))
        v_bf = _bf(vc)
        eb_last = jnp.exp(b_last)

        heads = [slice(hd * HG_DK, (hd + 1) * HG_DK) for hd in range(HG_HEADS)]
        nt_dot = lambda lhs, rhs: lax.dot_general(lhs, rhs, _NT, preferred_element_type=jnp.float32)
        a_diag = [nt_dot(q_lv[HG_SUB][:, hs_], k_diag[:, hs_]) for hs_ in heads]
        a_lvl = [[nt_dot(q_lv[m][:, hs_], k_lv[m][:, hs_]) for m, _ in levels] for hs_ in heads]
        st_old = [st_scr[hd] for hd in range(HG_HEADS)]
        o_inter = [nt_dot(q_all[:, hs_], _bf(st_t)) for hs_, st_t in zip(heads, st_old)]
        kv = [lax.dot_general(v_bf[:, hs_], k_all[:, hs_], _TN, preferred_element_type=jnp.float32)
              for hs_ in heads]
        amats = []
        for hd in range(HG_HEADS):
            amat = jnp.where(mask_diag, a_diag[hd], 0.0)
            for (m, msk), a_m in zip(levels, a_lvl[hd]):
                amat = amat + jnp.where(msk, a_m, 0.0)
            amats.append(_bf(amat))
        o_intra = [_dot(amat, v_bf[:, hs_]) for amat, hs_ in zip(amats, heads)]
        o_heads = []
        for hd, hs_ in enumerate(heads):
            st_scr[hd] = st_old[hd] * eb_last[:, hs_] + kv[hd]
            o_heads.append(_rms_rows(o_inter[hd] + o_intra[hd]) * hgn_ref[...])
        o_chunks.append(jnp.concatenate(o_heads, axis=1))
    o_all = o_chunks[0] if len(o_chunks) == 1 else jnp.concatenate(o_chunks, axis=0)
    y_hg = o_all * _silu(gh)

    mix = mix_lru + _dot(_bf(y_hg), w_out_ref[lw:, :])
    x_mid = x + g1 * mix
    xmid_ref[0] = x_mid

    h2 = _rms_rows(x_mid) * (1.0 + sc2) + sh2
    h2_hi = _bf(h2)
    h2_lo = _bf(h2 - h2_hi.astype(jnp.float32))
    h2_ref[0] = h2_hi
    hi_terms = _dot(h2_hi, wr_hl_ref[...])
    logits = (hi_terms[:, :ROUTE_LANES] + hi_terms[:, ROUTE_LANES:] + _dot(h2_lo, wr_hi_ref[...])
              + br_ref[...])

    lane = lax.broadcasted_iota(jnp.int32, (tt, ROUTE_LANES), 1)
    lane_f = lane.astype(jnp.float32)
    neg, big = -1e30, 1e6
    is_grp = lane < N_GROUPS
    gl = jnp.where(is_grp, logits, neg)
    gmax = jnp.max(gl, axis=-1, keepdims=True)
    gsel = jnp.min(jnp.where(gl == gmax, lane_f, big), axis=-1, keepdims=True)
    gsum = jnp.sum(jnp.where(is_grp, jnp.exp(gl - gmax), 0.0), axis=-1, keepdims=True)
    p_grp = 1.0 / gsum
    lo = N_GROUPS + gsel * EXPERTS_PER_GROUP
    emask = (lane_f >= lo) & (lane_f < lo + EXPERTS_PER_GROUP)
    el = jnp.where(emask, logits, neg)
    tv1 = jnp.max(el, axis=-1, keepdims=True)
    ti1 = jnp.min(jnp.where(emask & (el == tv1), lane_f, big), axis=-1, keepdims=True)
    emask2 = emask & (lane_f != ti1)
    el2 = jnp.where(emask2, logits, neg)
    tv2 = jnp.max(el2, axis=-1, keepdims=True)
    ti2 = jnp.min(jnp.where(emask2 & (el2 == tv2), lane_f, big), axis=-1, keepdims=True)
    e21 = jnp.exp(tv2 - tv1)
    w1 = p_grp / (1.0 + e21)
    w2 = w1 * e21
    hot1 = lane_f == ti1
    hot2 = lane_f == ti2
    hot12 = jnp.where(hot1 | hot2, 1.0, 0.0)
    before = _dot(tri_ref[...], _bf(hot12))
    c_tile = before[tt - 1:tt, :] + hot12[tt - 1:tt, :]
    c_pad = jnp.floor((c_tile + (GRANULE - 1.0)) * (1.0 / GRANULE)) * GRANULE
    lstart = _dot(_bf(jnp.broadcast_to(c_pad, (SUBLANES, ROUTE_LANES))), upper_ref[...])[0:1, :]
    pos = before + lstart
    slot1 = jnp.sum(jnp.where(hot1, pos, 0.0), axis=-1, keepdims=True)
    slot2 = jnp.sum(jnp.where(hot2, pos, 0.0), axis=-1, keepdims=True)
    gbase = cnt_scr[...]
    cnt_scr[...] = gbase + c_pad
    tab_ref[0, 0] = jnp.concatenate(
        [lstart, c_pad, gbase, jnp.zeros((SUBLANES - 3, ROUTE_LANES), jnp.float32)], axis=0).astype(jnp.int32)
    route_ref[0] = jnp.where(lane == 0, w1, jnp.where(lane == 1, w2, jnp.where(lane == 2, slot1,
                                                                                jnp.where(lane == 3, slot2, 0.0))))

    @pl.when(j == last_j)
    def _():
        conv_out_ref[0] = new_hist
        lru_out_ref[0] = carry
        for hd in range(HG_HEADS):
            hg_out_ref[0, hd] = st_scr[hd].T

    @pl.when((bi == last_b) & (j == last_j))
    def _():
        cnt_out_ref[...] = cnt_scr[...]


def _mixer_call(x, mod, conv0, lru0, hg0, wts, cnt_in, *, tt, chunk):
    bsz, t, d = x.shape
    grid = (bsz, t // tt)
    tile = lambda last: pl.BlockSpec((1, tt, last), lambda b, j: (b, j, 0))
    per_b = lambda *shape: pl.BlockSpec((1,) + shape, lambda b, j: (b,) + (0,) * len(shape))
    full = lambda a: pl.BlockSpec(a.shape, lambda b, j: (0,) * a.ndim)
    tri = _bf(jnp.tril(jnp.ones((tt, tt), jnp.float32), -1))
    upper = _bf(jnp.triu(jnp.ones((ROUTE_LANES, ROUTE_LANES), jnp.float32), 1))
    per_tile = lambda *shape: pl.BlockSpec((1, 1) + shape, lambda b, j: (b, j) + (0,) * len(shape))

    args, in_specs = [x], [tile(d)]
    args += [mod, conv0, lru0.reshape(bsz, 1, LRU_WIDTH), hg0]
    in_specs += [per_b(6, d), per_b(CONV_W - 1, LRU_WIDTH), per_b(1, LRU_WIDTH), per_b(HG_HEADS, HG_DK, HG_DK)]
    args += list(wts) + [tri, upper, cnt_in]
    in_specs += [full(w) for w in wts] + [full(tri), full(upper), full(cnt_in)]

    out_shape = (
        jax.ShapeDtypeStruct((bsz, t, d), jnp.float32),
        jax.ShapeDtypeStruct((bsz, t, d), jnp.bfloat16),
        jax.ShapeDtypeStruct((bsz, t, ROUTE_LANES), jnp.float32),
        jax.ShapeDtypeStruct((bsz, t // tt, SUBLANES, ROUTE_LANES), jnp.int32),
        jax.ShapeDtypeStruct((1, ROUTE_LANES), jnp.float32),
        jax.ShapeDtypeStruct((bsz, CONV_W - 1, LRU_WIDTH), jnp.float32),
        jax.ShapeDtypeStruct((bsz, 1, LRU_WIDTH), jnp.float32),
        jax.ShapeDtypeStruct((bsz, HG_HEADS, HG_DK, HG_DK), jnp.float32),
    )
    out_specs = (tile(d), tile(d), tile(ROUTE_LANES), per_tile(SUBLANES, ROUTE_LANES),
                 pl.BlockSpec((1, ROUTE_LANES), lambda b, j: (0, 0)),
                 per_b(CONV_W - 1, LRU_WIDTH), per_b(1, LRU_WIDTH), per_b(HG_HEADS, HG_DK, HG_DK))
    scratch = [
        pltpu.VMEM((tt + SUBLANES, LRU_WIDTH), jnp.float32),
        pltpu.VMEM((1, LRU_WIDTH), jnp.float32),
        pltpu.VMEM((HG_HEADS, HG_DK, HG_DK), jnp.float32),
        pltpu.VMEM((1, ROUTE_LANES), jnp.float32),
    ]
    return pl.pallas_call(
        functools.partial(_mixer_kernel, tt=tt, chunk=chunk),
        grid=grid, in_specs=in_specs, out_specs=out_specs, out_shape=out_shape, scratch_shapes=scratch,
        compiler_params=pltpu.CompilerParams(dimension_semantics=("arbitrary", "arbitrary"),
                                             vmem_limit_bytes=VMEM_LIMIT),
        name="mixer",
    )(*args)


def _mod_kernel(c_ref, w_ref, b_ref, o_ref):
    s = _silu(c_ref[...])
    o_ref[0] = jnp.dot(s, w_ref[0], preferred_element_type=jnp.float32,
                       precision=lax.Precision.HIGHEST) + b_ref[0]


def _mod_call(c_all, w_ada, b_ada):
    depth, d, n = w_ada.shape
    rows = c_all.shape[0]
    tn = d
    return pl.pallas_call(
        _mod_kernel,
        grid=(depth, n // tn),
        in_specs=[pl.BlockSpec((rows, d), lambda l, i: (0, 0)),
                  pl.BlockSpec((1, d, tn), lambda l, i: (l, 0, i)),
                  pl.BlockSpec((1, 1, tn), lambda l, i: (l, 0, i))],
        out_specs=pl.BlockSpec((1, rows, tn), lambda l, i: (l, 0, i)),
        out_shape=jax.ShapeDtypeStruct((depth, rows, n), jnp.float32),
        compiler_params=pltpu.CompilerParams(dimension_semantics=("arbitrary", "arbitrary"),
                                             vmem_limit_bytes=VMEM_LIMIT),
        name="modulation",
    )(c_all, w_ada, b_ada.reshape(depth, 1, n))


def _for_each_granule(off_ref, tab_ref, fn):
    def per_expert(e, total):
        lane = N_GROUPS + e
        n_gran = lax.shift_right_logical(tab_ref[0, 1, lane], GRANULE.bit_length() - 1)
        src0 = tab_ref[0, 0, lane]
        dst0 = off_ref[e] + tab_ref[0, 2, lane]

        def per_granule(g, c):
            fn(pl.multiple_of(src0 + g * GRANULE, GRANULE), pl.multiple_of(dst0 + g * GRANULE, GRANULE))
            return c

        lax.fori_loop(0, n_gran, per_granule, 0)
        return total + n_gran

    return lax.fori_loop(0, N_EXPERTS, per_expert, 0)


def _dispatch_kernel(off_ref, tab_ref, h_ref, route_ref, hs_in_ref, hs_ref, srt, sem, pend):
    del hs_in_ref
    i = pl.program_id(0)
    cur = lax.rem(i, 2)
    tt, n_slots = h_ref.shape[0], srt.shape[1]

    def granule_copy(buf, src, dst):
        return pltpu.make_async_copy(srt.at[buf, pl.ds(src, GRANULE), :],
                                     hs_ref.at[pl.ds(dst, GRANULE), :], sem.at[buf])

    def drain(buf, n):
        lax.fori_loop(0, n, lambda g, c: (granule_copy(buf, 0, 0).wait(), c)[1], 0)

    @pl.when(i == 0)
    def _():
        pend[0] = 0

    rec = route_ref[...]
    if tt < LANES:
        rec = jnp.concatenate([rec, jnp.zeros((LANES - tt, ROUTE_LANES), jnp.float32)], axis=0)
    rec_t = rec.T
    s1_row, s2_row = rec_t[2:3, :tt], rec_t[3:4, :tt]
    slot_iota = lax.broadcasted_iota(jnp.int32, (n_slots, tt), 0).astype(jnp.float32)
    perm = _bf(jnp.where((slot_iota == s1_row) | (slot_iota == s2_row), 1.0, 0.0))
    srt[cur] = _bf(_dot(perm, h_ref[...]))

    earlier = pend[0]
    pend[0] = _for_each_granule(off_ref, tab_ref, lambda s, d: granule_copy(cur, s, d).start())
    drain(1 - cur, earlier)

    @pl.when(i == pl.num_programs(0) - 1)
    def _():
        drain(cur, pend[0])


def _dispatch_call(off, tab, h2, route, hs, *, n_slots):
    n = tab.shape[0]
    tt = h2.shape[0] // n
    d = h2.shape[1]
    grid_spec = pltpu.PrefetchScalarGridSpec(
        num_scalar_prefetch=1, grid=(n,),
        in_specs=[pl.BlockSpec((1, SUBLANES, ROUTE_LANES), lambda i, off: (i, 0, 0), memory_space=pltpu.SMEM),
                  pl.BlockSpec((tt, d), lambda i, off: (i, 0)),
                  pl.BlockSpec((tt, ROUTE_LANES), lambda i, off: (i, 0)),
                  pl.BlockSpec(memory_space=pl.ANY)],
        out_specs=pl.BlockSpec(memory_space=pl.ANY),
        scratch_shapes=[pltpu.VMEM((2, n_slots, d), jnp.bfloat16), pltpu.SemaphoreType.DMA((2,)),
                        pltpu.SMEM((1,), jnp.int32)])
    return pl.pallas_call(
        _dispatch_kernel,
        grid_spec=grid_spec,
        out_shape=jax.ShapeDtypeStruct(hs.shape, hs.dtype),
        input_output_aliases={4: 0},
        compiler_params=pltpu.CompilerParams(dimension_semantics=("arbitrary",), vmem_limit_bytes=VMEM_LIMIT),
        name="dispatch",
    )(off, tab, h2, route, hs)


def _expert_kernel(te_ref, nt_ref, hs_ref, wg_ref, wu_ref, wd_ref, ys_ref, wgu_bf, wd_bf):
    i = pl.program_id(0)
    active = i < nt_ref[0]

    @pl.when(active & ((i == 0) | (te_ref[i] != te_ref[jnp.maximum(i - 1, 0)])))
    def _():
        wgu_bf[:, :EXPERT_HIDDEN] = _bf(wg_ref[0, 0])
        wgu_bf[:, EXPERT_HIDDEN:] = _bf(wu_ref[0, 0])
        wd_bf[...] = _bf(wd_ref[0, 0])

    @pl.when(active)
    def _():
        gu = _dot(hs_ref[...], wgu_bf[...])
        hid = _silu(gu[:, :EXPERT_HIDDEN]) * gu[:, EXPERT_HIDDEN:]
        ys_ref[...] = _bf(_dot(_bf(hid), wd_bf[...]))


def _expert_call(tile_expert, n_tiles, hs, w_gate, w_up, w_down, *, layer, te_rows):
    npad, d = hs.shape
    wspec = lambda a: pl.BlockSpec((1, 1) + a.shape[2:], lambda i, te, nt: (layer, te[i], 0, 0))
    rows = pl.BlockSpec((te_rows, d), lambda i, te, nt: (jnp.minimum(i, nt[0] - 1), 0))
    grid_spec = pltpu.PrefetchScalarGridSpec(
        num_scalar_prefetch=2, grid=(npad // te_rows,),
        in_specs=[rows, wspec(w_gate), wspec(w_up), wspec(w_down)],
        out_specs=rows,
        scratch_shapes=[pltpu.VMEM((d, 2 * EXPERT_HIDDEN), jnp.bfloat16),
                        pltpu.VMEM((EXPERT_HIDDEN, d), jnp.bfloat16)])
    return pl.pallas_call(
        _expert_kernel,
        grid_spec=grid_spec,
        out_shape=jax.ShapeDtypeStruct((npad, d), jnp.bfloat16),
        input_output_aliases={2: 0},
        compiler_params=pltpu.CompilerParams(dimension_semantics=("arbitrary",), vmem_limit_bytes=VMEM_LIMIT),
        name="experts",
    )(tile_expert, n_tiles, hs, w_gate, w_up, w_down)


def _combine_kernel(off_ref, tab_ref, tab_next_ref, route_ref, x_ref, mod_ref, fw_ref, ys_ref, o_ref, yloc, sem,
                    pend, *, final):
    i = pl.program_id(0)
    last = pl.num_programs(0) - 1
    cur = lax.rem(i, 2)

    def granule_copy(buf, loc, srt):
        return pltpu.make_async_copy(ys_ref.at[pl.ds(srt, GRANULE), :],
                                     yloc.at[buf, pl.ds(loc, GRANULE), :], sem.at[buf])

    def gather(tab, buf):
        return _for_each_granule(off_ref, tab, lambda s, d: granule_copy(buf, s, d).start())

    @pl.when(i == 0)
    def _():
        yloc[...] = jnp.zeros_like(yloc)
        pend[0] = gather(tab_ref, 0)

    n_cur = pend[0]

    @pl.when(i < last)
    def _():
        pend[0] = gather(tab_next_ref, 1 - cur)

    lax.fori_loop(0, n_cur, lambda g, c: (granule_copy(cur, 0, 0).wait(), c)[1], 0)

    rt = route_ref[...]
    tt, n_slots = rt.shape[0], yloc.shape[1]
    slot = lax.broadcasted_iota(jnp.int32, (tt, n_slots), 1).astype(jnp.float32)
    wc = jnp.where(slot == rt[:, 2:3], rt[:, 0:1], 0.0) + jnp.where(slot == rt[:, 3:4], rt[:, 1:2], 0.0)
    x_new = x_ref[...] + mod_ref[0, 5:6, :] * _dot(_bf(wc), yloc[cur])
    o_ref[...] = _rms_rows(x_new) * fw_ref[...] if final else x_new


def _combine_call(off, tab, route, x_mid, mod, fw, ys, *, n_slots, final):
    n = tab.shape[0]
    tt = route.shape[0] // n
    tiles_per_seq = n // mod.shape[0]
    d = ys.shape[-1]
    tab_spec = lambda imap: pl.BlockSpec((1, SUBLANES, ROUTE_LANES), imap, memory_space=pltpu.SMEM)
    grid_spec = pltpu.PrefetchScalarGridSpec(
        num_scalar_prefetch=1, grid=(n,),
        in_specs=[tab_spec(lambda i, off: (i, 0, 0)),
                  tab_spec(lambda i, off: (jnp.minimum(i + 1, n - 1), 0, 0)),
                  pl.BlockSpec((tt, ROUTE_LANES), lambda i, off: (i, 0)),
                  pl.BlockSpec((tt, d), lambda i, off: (i, 0)),
                  pl.BlockSpec((1,) + mod.shape[1:], lambda i, off: (i // tiles_per_seq, 0, 0)),
                  pl.BlockSpec((1, d), lambda i, off: (0, 0)),
                  pl.BlockSpec(memory_space=pl.ANY)],
        out_specs=pl.BlockSpec((tt, d), lambda i, off: (i, 0)),
        scratch_shapes=[pltpu.VMEM((2, n_slots, d), jnp.bfloat16), pltpu.SemaphoreType.DMA((2,)),
                        pltpu.SMEM((1,), jnp.int32)])
    return pl.pallas_call(
        functools.partial(_combine_kernel, final=final),
        grid_spec=grid_spec,
        out_shape=jax.ShapeDtypeStruct((n * tt, d), jnp.float32),
        compiler_params=pltpu.CompilerParams(dimension_semantics=("arbitrary",), vmem_limit_bytes=VMEM_LIMIT),
        name="combine",
    )(off, tab, tab, route, x_mid, mod, fw, ys)


def _block_diag_halves(w):
    nb, bd, _ = w.shape
    per = nb // 2
    eye = jnp.eye(per, dtype=w.dtype)
    halves = [jnp.einsum('nij,nm->nimj', w[h * per:(h + 1) * per], eye).reshape(per * bd, per * bd) for h in range(2)]
    return jnp.stack(halves)


def _tile_slots(tt):
    return -(-_tile_slots_used(tt) // LANES) * LANES


def _tile_slots_used(tt):
    return 2 * tt + (GRANULE - 1) * min(N_EXPERTS, 2 * tt)


def _tile_rows(t):
    for cand in (512, 256, 128, 64, 32, 16, 8):
        if t % cand == 0:
            return cand
    raise ValueError(f"sequence length {t} is not a multiple of 8")


def kernel(x_prompt, x_sample, state_conv, state_lru, state_hgrn, c_prompt, c_sample, w_ada, b_ada, w_in, conv_w,
           conv_b, lru_wa, lru_ba, lru_wx, lru_bx, lru_lambda, hg_lower, hg_norm_w, w_out, w_grp, b_grp, w_rt, b_rt,
           w_gate, w_up, w_down, final_norm_w):
    depth = w_in.shape[0]
    bp, tp, d = x_prompt.shape
    bs, ts, _ = x_sample.shape
    f32 = jnp.float32

    c_all = jnp.concatenate([c_prompt, c_sample], axis=0)
    rows = -(-c_all.shape[0] // SUBLANES) * SUBLANES
    c_all = jnp.pad(c_all, ((0, rows - c_all.shape[0]), (0, 0)))
    mod_all = _mod_call(c_all, w_ada, b_ada).reshape(depth, rows, 6, d)

    p_low = jax.nn.softmax(hg_lower.astype(f32), axis=0)
    lbs = jnp.cumsum(p_low, axis=0) - p_low[0]
    w_in_bf, w_out_bf = _bf(w_in), _bf(w_out)
    wr = jnp.concatenate([w_grp, w_rt], axis=-1)
    wr = jnp.pad(wr, ((0, 0), (0, 0), (0, ROUTE_LANES - wr.shape[-1])))
    wr_hi = _bf(wr)
    wr_hl = jnp.concatenate([wr_hi, _bf(wr - wr_hi.astype(f32))], axis=-1)
    br = jnp.concatenate([b_grp, b_rt], axis=-1)
    br = jnp.pad(br, ((0, 0), (0, ROUTE_LANES - br.shape[-1])))
    fw = final_norm_w.reshape(1, d)

    trunks = [
        dict(x=x_prompt, b0=0, nb=bp, t=tp,
             conv=jnp.zeros((depth, bp, CONV_W - 1, LRU_WIDTH), f32), lru=jnp.zeros((depth, bp, LRU_WIDTH), f32),
             hg=jnp.zeros((depth, bp, HG_HEADS, HG_DK, HG_DK), f32)),
        dict(x=x_sample, b0=bp, nb=bs, t=ts, conv=state_conv, lru=state_lru, hg=state_hgrn),
    ]
    for tr in trunks:
        tr["tt"] = _tile_rows(tr["t"])
        tr["chunk"] = min(64, tr["tt"])
        tr.update(convs=[], lrus=[], hgs=[])

    n_rows = sum(tr["nb"] * (tr["t"] // tr["tt"]) * _tile_slots_used(tr["tt"]) for tr in trunks)
    max_tiles = (n_rows + N_EXPERTS * (EXPERT_TILE - 1)) // EXPERT_TILE
    hs_zero = jnp.zeros((max_tiles * EXPERT_TILE, d), jnp.bfloat16)
    tile_ids = jnp.arange(max_tiles, dtype=jnp.int32)

    for l in range(depth):
        wts = (w_in_bf[l], conv_w[l], conv_b[l].reshape(1, -1), _bf(_block_diag_halves(lru_wa[l])),
               _bf(_block_diag_halves(lru_wx[l])), lru_ba[l].reshape(1, -1), lru_bx[l].reshape(1, -1),
               lru_lambda[l].reshape(1, -1), lbs[l].reshape(1, -1), hg_norm_w[l].reshape(1, -1), w_out_bf[l],
               wr_hi[l], wr_hl[l], br[l].reshape(1, -1))
        cnt = jnp.zeros((1, ROUTE_LANES), f32)
        for tr in trunks:
            mod = mod_all[l, tr["b0"]:tr["b0"] + tr["nb"]]
            tr["x"], tr["h2"], tr["route"], tr["tab"], cnt, conv_n, lru_n, hg_n = _mixer_call(
                tr["x"], mod, tr["conv"][l], tr["lru"][l], tr["hg"][l], wts, cnt, tt=tr["tt"], chunk=tr["chunk"])
            tr["mod"] = mod
            tr["convs"].append(conv_n)
            tr["lrus"].append(lru_n.reshape(tr["nb"], LRU_WIDTH))
            tr["hgs"].append(hg_n)

        counts = cnt[0, N_GROUPS:N_GROUPS + N_EXPERTS].astype(jnp.int32)
        tiles_e = (counts + (EXPERT_TILE - 1)) // EXPERT_TILE
        ends = jnp.cumsum(tiles_e)
        off = (ends - tiles_e) * EXPERT_TILE
        n_tiles = ends[-1:]
        tile_expert = jnp.sum(jnp.minimum(tile_ids, n_tiles - 1)[:, None] >= ends[None, :], axis=1).astype(jnp.int32)

        hs = hs_zero
        for tr in trunks:
            tr["tab"] = tr["tab"].reshape((-1,) + tr["tab"].shape[2:])
            tr["route"] = tr["route"].reshape(-1, ROUTE_LANES)
            hs = _dispatch_call(off, tr["tab"], tr["h2"].reshape(-1, d), tr["route"], hs,
                                n_slots=_tile_slots(tr["tt"]))
        ys = _expert_call(tile_expert, n_tiles, hs, w_gate, w_up, w_down, layer=l, te_rows=EXPERT_TILE)
        for tr in trunks:
            x_new = _combine_call(off, tr["tab"], tr["route"], tr["x"].reshape(-1, d), tr["mod"], fw, ys,
                                  n_slots=_tile_slots(tr["tt"]), final=(l == depth - 1))
            tr["x"] = x_new.reshape(tr["nb"], tr["t"], d)

    (yp, cp, lp, hp), (ys_, cs, ls, hs_) = [
        (tr["x"], jnp.stack(tr["convs"]), jnp.stack(tr["lrus"]), jnp.stack(tr["hgs"])) for tr in trunks]
    return (yp, ys_, cp, lp, hp, cs, ls, hs_)
```

```python
import functools

import jax
import jax.numpy as jnp
from jax import lax
from jax.experimental import pallas as pl
from jax.experimental.pallas import tpu as pltpu

D_MODEL = 1024
LRU_WIDTH = 512
LRU_BLOCKS = 8
LRU_C = 8.0
CONV_W = 4
HG_WIDTH = 512
HG_HEADS = 4
HG_DK = 128
N_GROUPS = 4
EXPERTS_PER_GROUP = 8
N_EXPERTS = 32
EXPERT_HIDDEN = 256
EPS = 1e-6

SUBLANES = 8
GRANULE = 16
LANES = 128
HG_SUB = 16
EXP_CLAMP = 80.0
ROUTE_LANES = LANES
EXPERT_TILE = 512
VMEM_LIMIT = 56 * 1024 * 1024

_NT = (((1,), (1,)), ((), ()))
_TN = (((0,), (0,)), ((), ()))


def _bf(x):
    return x.astype(jnp.bfloat16)


def _dot(a, b):
    return jnp.dot(a, b, preferred_element_type=jnp.float32)


def _sigmoid(x):
    return 1.0 / (1.0 + jnp.exp(-x))


def _silu(x):
    return x * _sigmoid(x)


def _gelu_tanh(x):
    return 0.5 * x * (1.0 + jnp.tanh(0.7978845608028654 * (x + 0.044715 * (x * x * x))))


def _softplus(x):
    return jnp.maximum(x, 0.0) + jnp.log(1.0 + jnp.exp(-jnp.abs(x)))


def _rms_rows(x):
    return x * lax.rsqrt(jnp.mean(x * x, axis=-1, keepdims=True) + EPS)


def _scan8(a, b):
    rows = lax.broadcasted_iota(jnp.int32, a.shape, 0)
    for d in (1, 2, 4):
        m = rows >= d
        a_sh = pltpu.roll(a, d, 0)
        b_sh = pltpu.roll(b, d, 0)
        b = jnp.where(m, a * b_sh + b, b)
        a = jnp.where(m, a * a_sh, a)
    return a, b


def _cumsum8(x):
    rows = lax.broadcasted_iota(jnp.int32, x.shape, 0)
    for d in (1, 2, 4):
        x = x + jnp.where(rows >= d, pltpu.roll(x, d, 0), 0.0)
    return x


def _block_refs(b, m):
    n, w = b.shape
    starts, ends = [], []
    for i in range(n // m):
        if i == 0:
            starts.append(jnp.zeros((m, w), jnp.float32))
        else:
            starts.append(jnp.broadcast_to(b[i * m - 1:i * m, :], (m, w)))
        ends.append(jnp.broadcast_to(b[(i + 1) * m - 1:(i + 1) * m, :], (m, w)))
    if len(starts) == 1:
        return starts[0], ends[0]
    return jnp.concatenate(starts, axis=0), jnp.concatenate(ends, axis=0)


def _mixer_kernel(*refs, tt, chunk):
    it = iter(refs)
    x_ref = next(it)
    (mod_ref, conv0_ref, lru0_ref, hg0_ref, w_in_ref, conv_w_ref, conv_b_ref, wa_ref, wx_ref, ba_ref, bx_ref,
     lam_ref, lbs_ref, hgn_ref, w_out_ref, wr_hi_ref, wr_hl_ref, br_ref, tri_ref, upper_ref, cnt_in_ref) = (
        next(it) for _ in range(21))
    (xmid_ref, h2_ref, route_ref, tab_ref, cnt_out_ref, conv_out_ref, lru_out_ref,
     hg_out_ref) = (next(it) for _ in range(8))
    conv_scr, lru_scr, st_scr, cnt_scr = (next(it) for _ in range(4))

    bi = pl.program_id(0)
    j = pl.program_id(1)
    last_b = pl.num_programs(0) - 1
    last_j = pl.num_programs(1) - 1
    pad = SUBLANES - (CONV_W - 1)

    @pl.when((bi == 0) & (j == 0))
    def _():
        cnt_scr[...] = cnt_in_ref[...]

    @pl.when(j == 0)
    def _():
        conv_scr[pad:SUBLANES, :] = conv0_ref[0]
        lru_scr[...] = lru0_ref[0]
        for hd in range(HG_HEADS):
            st_scr[hd] = hg0_ref[0, hd].T

    x = x_ref[0]
    mod = mod_ref[0]
    sh1, sc1, g1, sh2, sc2 = (mod[i:i + 1, :] for i in range(5))

    h = _rms_rows(x) * (1.0 + sc1) + sh1
    h_bf = _bf(h)
    lw, hw = LRU_WIDTH, HG_WIDTH
    proj = lambda lo, width: _dot(h_bf, w_in_ref[:, lo:lo + width])
    xb = proj(0, lw)
    gb = proj(lw, lw)

    conv_scr[SUBLANES:SUBLANES + tt, :] = xb
    xc = conv_b_ref[...]
    for k in range(CONV_W):
        xc = xc + conv_w_ref[k:k + 1, :] * conv_scr[pl.ds(pad + k, tt), :]
    new_hist = conv_scr[pl.ds(tt + pad, CONV_W - 1), :]
    conv_scr[pad:SUBLANES, :] = new_hist
    qh = proj(2 * lw, hw)

    xc_bf = _bf(xc)
    half = lw // 2
    r_pre = jnp.concatenate([_dot(xc_bf[:, i * half:(i + 1) * half], wa_ref[i]) for i in range(2)], axis=1)
    i_pre = jnp.concatenate([_dot(xc_bf[:, i * half:(i + 1) * half], wx_ref[i]) for i in range(2)], axis=1)
    r = _sigmoid(r_pre + ba_ref[...])
    ig = _sigmoid(i_pre + bx_ref[...])
    fh = proj(2 * lw + hw, hw)
    log_a = (-LRU_C) * r * _softplus(-lam_ref[...])
    a = jnp.exp(log_a)
    th = jnp.tanh(log_a)
    one_minus_a2 = (-2.0) * th / (1.0 - th)
    bt = jnp.sqrt(one_minus_a2) * ig * xc
    ih = proj(2 * lw + 2 * hw, hw)

    carry = lru_scr[...]
    hs = []
    for g in range(tt // SUBLANES):
        sl = slice(g * SUBLANES, (g + 1) * SUBLANES)
        acum, hloc = _scan8(a[sl], bt[sl])
        hg = acum * carry + hloc
        hs.append(hg)
        carry = hg[SUBLANES - 1:SUBLANES, :]
    lru_scr[...] = carry
    gh = proj(2 * lw + 3 * hw, hw)
    hl = jnp.concatenate(hs, axis=0)
    y_lru = hl * _gelu_tanh(gb)
    mix_lru = _dot(_bf(y_lru), w_out_ref[:lw, :])

    lbs = lbs_ref[...]
    q = _silu(qh) * (HG_DK ** -0.5)
    f = lbs + (1.0 - lbs) * _sigmoid(fh)
    kk = 1.0 - f
    glog = jnp.log(f)

    ti = lax.broadcasted_iota(jnp.int32, (chunk, chunk), 0)
    si = lax.broadcasted_iota(jnp.int32, (chunk, chunk), 1)
    sub_shift = HG_SUB.bit_length() - 1
    mask_diag = ((ti >> sub_shift) == (si >> sub_shift)) & (ti >= si)
    levels = []
    m = HG_SUB
    while m < chunk:
        sh = m.bit_length() - 1
        levels.append((m, ((ti >> (sh + 1)) == (si >> (sh + 1))) & (((ti >> sh) & 1) == 1) & (((si >> sh) & 1) == 0)))
        m *= 2

    o_chunks = []
    for c in range(tt // chunk):
        rs = slice(c * chunk, (c + 1) * chunk)
        gc = glog[rs]
        bs, bcarry = [], None
        for g in range(chunk // SUBLANES):
            cs = _cumsum8(gc[g * SUBLANES:(g + 1) * SUBLANES])
            if bcarry is not None:
                cs = cs + bcarry
            bs.append(cs)
            bcarry = cs[SUBLANES - 1:SUBLANES, :]
        b = jnp.concatenate(bs, axis=0)
        b_last = bcarry
        qc, kc, vc = q[rs], kk[rs], ih[rs]

        st16, en16 = _block_refs(b, HG_SUB)
        q_lv = {HG_SUB: _bf(qc * jnp.exp(b - st16))}
        k_lv = {HG_SUB: _bf(kc * jnp.exp(en16 - b))}
        k_diag = _bf(kc * jnp.exp(jnp.minimum(st16 - b, EXP_CLAMP)))
        for m, _ in levels:
            if m == HG_SUB:
                continue
            st, en = _block_refs(b, m)
            q_lv[m] = _bf(qc * jnp.exp(b - st))
            k_lv[m] = _bf(kc * jnp.exp(en - b))
        q_all = _bf(qc * jnp.exp(b))
        k_all = _bf(kc * jnp.exp(b_last - b))
        v_bf = _bf(vc)
        eb_last = jnp.exp(b_last)

        heads = [slice(hd * HG_DK, (hd + 1) * HG_DK) for hd in range(HG_HEADS)]
        nt_dot = lambda lhs, rhs: lax.dot_general(lhs, rhs, _NT, preferred_element_type=jnp.float32)
        a_diag = [nt_dot(q_lv[HG_SUB][:, hs_], k_diag[:, hs_]) for hs_ in heads]
        a_lvl = [[nt_dot(q_lv[m][:, hs_], k_lv[m][:, hs_]) for m, _ in levels] for hs_ in heads]
        st_old = [st_scr[hd] for hd in range(HG_HEADS)]
        o_inter = [nt_dot(q_all[:, hs_], _bf(st_t)) for hs_, st_t in zip(heads, st_old)]
        kv = [lax.dot_general(v_bf[:, hs_], k_all[:, hs_], _TN, preferred_element_type=jnp.float32)
              for hs_ in heads]
        amats = []
        for hd in range(HG_HEADS):
            amat = jnp.where(mask_diag, a_diag[hd], 0.0)
            for (m, msk), a_m in zip(levels, a_lvl[hd]):
                amat = amat + jnp.where(msk, a_m, 0.0)
            amats.append(_bf(amat))
        o_intra = [_dot(amat, v_bf[:, hs_]) for amat, hs_ in zip(amats, heads)]
        o_heads = []
        for hd, hs_ in enumerate(heads):
            st_scr[hd] = st_old[hd] * eb_last[:, hs_] + kv[hd]
            o_heads.append(_rms_rows(o_inter[hd] + o_intra[hd]) * hgn_ref[...])
        o_chunks.append(jnp.concatenate(o_heads, axis=1))
    o_all = o_chunks[0] if len(o_chunks) == 1 else jnp.concatenate(o_chunks, axis=0)
    y_hg = o_all * _silu(gh)

    mix = mix_lru + _dot(_bf(y_hg), w_out_ref[lw:, :])
    x_mid = x + g1 * mix
    xmid_ref[0] = x_mid

    h2 = _rms_rows(x_mid) * (1.0 + sc2) + sh2
    h2_hi = _bf(h2)
    h2_lo = _bf(h2 - h2_hi.astype(jnp.float32))
    h2_ref[0] = h2_hi
    hi_terms = _dot(h2_hi, wr_hl_ref[...])
    logits = (hi_terms[:, :ROUTE_LANES] + hi_terms[:, ROUTE_LANES:] + _dot(h2_lo, wr_hi_ref[...])
              + br_ref[...])

    lane = lax.broadcasted_iota(jnp.int32, (tt, ROUTE_LANES), 1)
    lane_f = lane.astype(jnp.float32)
    neg, big = -1e30, 1e6
    is_grp = lane < N_GROUPS
    gl = jnp.where(is_grp, logits, neg)
    gmax = jnp.max(gl, axis=-1, keepdims=True)
    gsel = jnp.min(jnp.where(gl == gmax, lane_f, big), axis=-1, keepdims=True)
    gsum = jnp.sum(jnp.where(is_grp, jnp.exp(gl - gmax), 0.0), axis=-1, keepdims=True)
    p_grp = 1.0 / gsum
    lo = N_GROUPS + gsel * EXPERTS_PER_GROUP
    emask = (lane_f >= lo) & (lane_f < lo + EXPERTS_PER_GROUP)
    el = jnp.where(emask, logits, neg)
    tv1 = jnp.max(el, axis=-1, keepdims=True)
    ti1 = jnp.min(jnp.where(emask & (el == tv1), lane_f, big), axis=-1, keepdims=True)
    emask2 = emask & (lane_f != ti1)
    el2 = jnp.where(emask2, logits, neg)
    tv2 = jnp.max(el2, axis=-1, keepdims=True)
    ti2 = jnp.min(jnp.where(emask2 & (el2 == tv2), lane_f, big), axis=-1, keepdims=True)
    e21 = jnp.exp(tv2 - tv1)
    w1 = p_grp / (1.0 + e21)
    w2 = w1 * e21
    hot1 = lane_f == ti1
    hot2 = lane_f == ti2
    hot12 = jnp.where(hot1 | hot2, 1.0, 0.0)
    before = _dot(tri_ref[...], _bf(hot12))
    c_tile = before[tt - 1:tt, :] + hot12[tt - 1:tt, :]
    c_pad = jnp.floor((c_tile + (GRANULE - 1.0)) * (1.0 / GRANULE)) * GRANULE
    lstart = _dot(_bf(jnp.broadcast_to(c_pad, (SUBLANES, ROUTE_LANES))), upper_ref[...])[0:1, :]
    pos = before + lstart
    slot1 = jnp.sum(jnp.where(hot1, pos, 0.0), axis=-1, keepdims=True)
    slot2 = jnp.sum(jnp.where(hot2, pos, 0.0), axis=-1, keepdims=True)
    gbase = cnt_scr[...]
    cnt_scr[...] = gbase + c_pad

    per_expert = jnp.concatenate([lstart, c_pad, gbase, jnp.zeros((LANES - 3, ROUTE_LANES), jnp.float32)], axis=0).T
    lstart_c, c_pad_c, gbase_c = per_expert[:, 0:1], per_expert[:, 1:2], per_expert[:, 2:3]
    erow = lax.broadcasted_iota(jnp.int32, (LANES, ROUTE_LANES), 0)
    is_expert = (erow >= N_GROUPS) & (erow < N_GROUPS + N_EXPERTS)
    ended = is_expert & ((lstart_c + c_pad_c) * (1.0 / GRANULE) <= lane_f[0:1, :])
    owner = jnp.sum(jnp.where(ended, 1.0, 0.0), axis=0, keepdims=True)
    mine = (erow - N_GROUPS).astype(jnp.float32) == owner
    dst_row = jnp.sum(jnp.where(mine, gbase_c - lstart_c, 0.0), axis=0, keepdims=True) + GRANULE * lane_f[0:1, :]
    n_gran = jnp.broadcast_to(jnp.sum(c_pad, axis=-1, keepdims=True) * (1.0 / GRANULE), (1, ROUTE_LANES))
    tab_ref[0, 0] = jnp.concatenate(
        [owner, dst_row, n_gran, jnp.zeros((SUBLANES - 3, ROUTE_LANES), jnp.float32)], axis=0).astype(jnp.int32)
    route_ref[0] = jnp.where(lane == 0, w1, jnp.where(lane == 1, w2, jnp.where(lane == 2, slot1,
                                                                                jnp.where(lane == 3, slot2, 0.0))))

    @pl.when(j == last_j)
    def _():
        conv_out_ref[0] = new_hist
        lru_out_ref[0] = carry
        for hd in range(HG_HEADS):
            hg_out_ref[0, hd] = st_scr[hd].T

    @pl.when((bi == last_b) & (j == last_j))
    def _():
        cnt_out_ref[...] = cnt_scr[...]


def _mixer_call(x, mod, conv0, lru0, hg0, wts, cnt_in, *, tt, chunk):
    bsz, t, d = x.shape
    grid = (bsz, t // tt)
    tile = lambda last: pl.BlockSpec((1, tt, last), lambda b, j: (b, j, 0))
    per_b = lambda *shape: pl.BlockSpec((1,) + shape, lambda b, j: (b,) + (0,) * len(shape))
    full = lambda a: pl.BlockSpec(a.shape, lambda b, j: (0,) * a.ndim)
    tri = _bf(jnp.tril(jnp.ones((tt, tt), jnp.float32), -1))
    upper = _bf(jnp.triu(jnp.ones((ROUTE_LANES, ROUTE_LANES), jnp.float32), 1))
    per_tile = lambda *shape: pl.BlockSpec((1, 1) + shape, lambda b, j: (b, j) + (0,) * len(shape))

    args, in_specs = [x], [tile(d)]
    args += [mod, conv0, lru0.reshape(bsz, 1, LRU_WIDTH), hg0]
    in_specs += [per_b(6, d), per_b(CONV_W - 1, LRU_WIDTH), per_b(1, LRU_WIDTH), per_b(HG_HEADS, HG_DK, HG_DK)]
    args += list(wts) + [tri, upper, cnt_in]
    in_specs += [full(w) for w in wts] + [full(tri), full(upper), full(cnt_in)]

    out_shape = (
        jax.ShapeDtypeStruct((bsz, t, d), jnp.float32),
        jax.ShapeDtypeStruct((bsz, t, d), jnp.bfloat16),
        jax.ShapeDtypeStruct((bsz, t, ROUTE_LANES), jnp.float32),
        jax.ShapeDtypeStruct((bsz, t // tt, SUBLANES, ROUTE_LANES), jnp.int32),
        jax.ShapeDtypeStruct((1, ROUTE_LANES), jnp.float32),
        jax.ShapeDtypeStruct((bsz, CONV_W - 1, LRU_WIDTH), jnp.float32),
        jax.ShapeDtypeStruct((bsz, 1, LRU_WIDTH), jnp.float32),
        jax.ShapeDtypeStruct((bsz, HG_HEADS, HG_DK, HG_DK), jnp.float32),
    )
    out_specs = (tile(d), tile(d), tile(ROUTE_LANES), per_tile(SUBLANES, ROUTE_LANES),
                 pl.BlockSpec((1, ROUTE_LANES), lambda b, j: (0, 0)),
                 per_b(CONV_W - 1, LRU_WIDTH), per_b(1, LRU_WIDTH), per_b(HG_HEADS, HG_DK, HG_DK))
    scratch = [
        pltpu.VMEM((tt + SUBLANES, LRU_WIDTH), jnp.float32),
        pltpu.VMEM((1, LRU_WIDTH), jnp.float32),
        pltpu.VMEM((HG_HEADS, HG_DK, HG_DK), jnp.float32),
        pltpu.VMEM((1, ROUTE_LANES), jnp.float32),
    ]
    return pl.pallas_call(
        functools.partial(_mixer_kernel, tt=tt, chunk=chunk),
        grid=grid, in_specs=in_specs, out_specs=out_specs, out_shape=out_shape, scratch_shapes=scratch,
        compiler_params=pltpu.CompilerParams(dimension_semantics=("arbitrary", "arbitrary"),
                                             vmem_limit_bytes=VMEM_LIMIT),
        name="mixer",
    )(*args)


def _mod_kernel(c_ref, w_ref, b_ref, o_ref):
    s = _silu(c_ref[...])
    o_ref[0] = jnp.dot(s, w_ref[0], preferred_element_type=jnp.float32,
                       precision=lax.Precision.HIGHEST) + b_ref[0]


def _mod_call(c_all, w_ada, b_ada):
    depth, d, n = w_ada.shape
    rows = c_all.shape[0]
    tn = d
    return pl.pallas_call(
        _mod_kernel,
        grid=(depth, n // tn),
        in_specs=[pl.BlockSpec((rows, d), lambda l, i: (0, 0)),
                  pl.BlockSpec((1, d, tn), lambda l, i: (l, 0, i)),
                  pl.BlockSpec((1, 1, tn), lambda l, i: (l, 0, i))],
        out_specs=pl.BlockSpec((1, rows, tn), lambda l, i: (l, 0, i)),
        out_shape=jax.ShapeDtypeStruct((depth, rows, n), jnp.float32),
        compiler_params=pltpu.CompilerParams(dimension_semantics=("arbitrary", "arbitrary"),
                                             vmem_limit_bytes=VMEM_LIMIT),
        name="modulation",
    )(c_all, w_ada, b_ada.reshape(depth, 1, n))


def _for_each_granule(off_ref, tab_ref, fn):
    n_gran = tab_ref[0, 2, 0]

    def per_granule(g, c):
        fn(pl.multiple_of(g * GRANULE, GRANULE),
           pl.multiple_of(off_ref[tab_ref[0, 0, g]] + tab_ref[0, 1, g], GRANULE))
        return c

    lax.fori_loop(0, n_gran, per_granule, 0)
    return n_gran


def _dispatch_kernel(off_ref, tab_ref, h_ref, route_ref, hs_in_ref, hs_ref, srt, sem, pend):
    del hs_in_ref
    i = pl.program_id(0)
    cur = lax.rem(i, 2)
    tt, n_slots = h_ref.shape[0], srt.shape[1]

    def granule_copy(buf, src, dst):
        return pltpu.make_async_copy(srt.at[buf, pl.ds(src, GRANULE), :],
                                     hs_ref.at[pl.ds(dst, GRANULE), :], sem.at[buf])

    def drain(buf, n):
        lax.fori_loop(0, n, lambda g, c: (granule_copy(buf, 0, 0).wait(), c)[1], 0)

    @pl.when(i == 0)
    def _():
        pend[0] = 0

    rec = route_ref[...]
    if tt < LANES:
        rec = jnp.concatenate([rec, jnp.zeros((LANES - tt, ROUTE_LANES), jnp.float32)], axis=0)
    rec_t = rec.T
    s1_row, s2_row = rec_t[2:3, :tt], rec_t[3:4, :tt]
    slot_iota = lax.broadcasted_iota(jnp.int32, (n_slots, tt), 0).astype(jnp.float32)
    perm = _bf(jnp.where((slot_iota == s1_row) | (slot_iota == s2_row), 1.0, 0.0))
    srt[cur] = _bf(_dot(perm, h_ref[...]))

    earlier = pend[0]
    pend[0] = _for_each_granule(off_ref, tab_ref, lambda s, d: granule_copy(cur, s, d).start())
    drain(1 - cur, earlier)

    @pl.when(i == pl.num_programs(0) - 1)
    def _():
        drain(cur, pend[0])


def _dispatch_call(off, tab, h2, route, hs, *, n_slots):
    n = tab.shape[0]
    tt = h2.shape[0] // n
    d = h2.shape[1]
    grid_spec = pltpu.PrefetchScalarGridSpec(
        num_scalar_prefetch=1, grid=(n,),
        in_specs=[pl.BlockSpec((1, SUBLANES, ROUTE_LANES), lambda i, off: (i, 0, 0), memory_space=pltpu.SMEM),
                  pl.BlockSpec((tt, d), lambda i, off: (i, 0)),
                  pl.BlockSpec((tt, ROUTE_LANES), lambda i, off: (i, 0)),
                  pl.BlockSpec(memory_space=pl.ANY)],
        out_specs=pl.BlockSpec(memory_space=pl.ANY),
        scratch_shapes=[pltpu.VMEM((2, n_slots, d), jnp.bfloat16), pltpu.SemaphoreType.DMA((2,)),
                        pltpu.SMEM((1,), jnp.int32)])
    return pl.pallas_call(
        _dispatch_kernel,
        grid_spec=grid_spec,
        out_shape=jax.ShapeDtypeStruct(hs.shape, hs.dtype),
        input_output_aliases={4: 0},
        compiler_params=pltpu.CompilerParams(dimension_semantics=("arbitrary",), vmem_limit_bytes=VMEM_LIMIT),
        name="dispatch",
    )(off, tab, h2, route, hs)


def _expert_kernel(te_ref, nt_ref, hs_ref, wg_ref, wu_ref, wd_ref, ys_ref, wgu_bf, wd_bf):
    i = pl.program_id(0)
    active = i < nt_ref[0]

    @pl.when(active & ((i == 0) | (te_ref[i] != te_ref[jnp.maximum(i - 1, 0)])))
    def _():
        wgu_bf[:, :EXPERT_HIDDEN] = _bf(wg_ref[0, 0])
        wgu_bf[:, EXPERT_HIDDEN:] = _bf(wu_ref[0, 0])
        wd_bf[...] = _bf(wd_ref[0, 0])

    @pl.when(active)
    def _():
        gu = _dot(hs_ref[...], wgu_bf[...])
        hid = _silu(gu[:, :EXPERT_HIDDEN]) * gu[:, EXPERT_HIDDEN:]
        ys_ref[...] = _bf(_dot(_bf(hid), wd_bf[...]))


def _expert_call(tile_expert, n_tiles, hs, w_gate, w_up, w_down, *, layer, te_rows):
    npad, d = hs.shape
    wspec = lambda a: pl.BlockSpec((1, 1) + a.shape[2:], lambda i, te, nt: (layer, te[i], 0, 0))
    rows = pl.BlockSpec((te_rows, d), lambda i, te, nt: (jnp.minimum(i, nt[0] - 1), 0))
    grid_spec = pltpu.PrefetchScalarGridSpec(
        num_scalar_prefetch=2, grid=(npad // te_rows,),
        in_specs=[rows, wspec(w_gate), wspec(w_up), wspec(w_down)],
        out_specs=rows,
        scratch_shapes=[pltpu.VMEM((d, 2 * EXPERT_HIDDEN), jnp.bfloat16),
                        pltpu.VMEM((EXPERT_HIDDEN, d), jnp.bfloat16)])
    return pl.pallas_call(
        _expert_kernel,
        grid_spec=grid_spec,
        out_shape=jax.ShapeDtypeStruct((npad, d), jnp.bfloat16),
        input_output_aliases={2: 0},
        compiler_params=pltpu.CompilerParams(dimension_semantics=("arbitrary",), vmem_limit_bytes=VMEM_LIMIT),
        name="experts",
    )(tile_expert, n_tiles, hs, w_gate, w_up, w_down)


def _combine_kernel(off_ref, tab_ref, tab_next_ref, route_ref, x_ref, mod_ref, fw_ref, ys_ref, o_ref, yloc, sem,
                    pend, *, final):
    i = pl.program_id(0)
    last = pl.num_programs(0) - 1
    cur = lax.rem(i, 2)

    def granule_copy(buf, loc, srt):
        return pltpu.make_async_copy(ys_ref.at[pl.ds(srt, GRANULE), :],
                                     yloc.at[buf, pl.ds(loc, GRANULE), :], sem.at[buf])

    def gather(tab, buf):
        return _for_each_granule(off_ref, tab, lambda s, d: granule_copy(buf, s, d).start())

    @pl.when(i == 0)
    def _():
        yloc[...] = jnp.zeros_like(yloc)
        pend[0] = gather(tab_ref, 0)

    n_cur = pend[0]

    @pl.when(i < last)
    def _():
        pend[0] = gather(tab_next_ref, 1 - cur)

    lax.fori_loop(0, n_cur, lambda g, c: (granule_copy(cur, 0, 0).wait(), c)[1], 0)

    rt = route_ref[...]
    tt, n_slots = rt.shape[0], yloc.shape[1]
    slot = lax.broadcasted_iota(jnp.int32, (tt, n_slots), 1).astype(jnp.float32)
    wc = jnp.where(slot == rt[:, 2:3], rt[:, 0:1], 0.0) + jnp.where(slot == rt[:, 3:4], rt[:, 1:2], 0.0)
    x_new = x_ref[...] + mod_ref[0, 5:6, :] * _dot(_bf(wc), yloc[cur])
    o_ref[...] = _rms_rows(x_new) * fw_ref[...] if final else x_new


def _combine_call(off, tab, route, x_mid, mod, fw, ys, *, n_slots, final):
    n = tab.shape[0]
    tt = route.shape[0] // n
    tiles_per_seq = n // mod.shape[0]
    d = ys.shape[-1]
    tab_spec = lambda imap: pl.BlockSpec((1, SUBLANES, ROUTE_LANES), imap, memory_space=pltpu.SMEM)
    grid_spec = pltpu.PrefetchScalarGridSpec(
        num_scalar_prefetch=1, grid=(n,),
        in_specs=[tab_spec(lambda i, off: (i, 0, 0)),
                  tab_spec(lambda i, off: (jnp.minimum(i + 1, n - 1), 0, 0)),
                  pl.BlockSpec((tt, ROUTE_LANES), lambda i, off: (i, 0)),
                  pl.BlockSpec((tt, d), lambda i, off: (i, 0)),
                  pl.BlockSpec((1,) + mod.shape[1:], lambda i, off: (i // tiles_per_seq, 0, 0)),
                  pl.BlockSpec((1, d), lambda i, off: (0, 0)),
                  pl.BlockSpec(memory_space=pl.ANY)],
        out_specs=pl.BlockSpec((tt, d), lambda i, off: (i, 0)),
        scratch_shapes=[pltpu.VMEM((2, n_slots, d), jnp.bfloat16), pltpu.SemaphoreType.DMA((2,)),
                        pltpu.SMEM((1,), jnp.int32)])
    return pl.pallas_call(
        functools.partial(_combine_kernel, final=final),
        grid_spec=grid_spec,
        out_shape=jax.ShapeDtypeStruct((n * tt, d), jnp.float32),
        compiler_params=pltpu.CompilerParams(dimension_semantics=("arbitrary",), vmem_limit_bytes=VMEM_LIMIT),
        name="combine",
    )(off, tab, tab, route, x_mid, mod, fw, ys)


def _block_diag_halves(w):
    nb, bd, _ = w.shape
    per = nb // 2
    eye = jnp.eye(per, dtype=w.dtype)
    halves = [jnp.einsum('nij,nm->nimj', w[h * per:(h + 1) * per], eye).reshape(per * bd, per * bd) for h in range(2)]
    return jnp.stack(halves)


def _tile_slots(tt):
    return -(-_tile_slots_used(tt) // LANES) * LANES


def _tile_slots_used(tt):
    return 2 * tt + (GRANULE - 1) * min(N_EXPERTS, 2 * tt)


def _tile_rows(t):
    for cand in (512, 256, 128, 64, 32, 16, 8):
        if t % cand == 0:
            return cand
    raise ValueError(f"sequence length {t} is not a multiple of 8")


def kernel(x_prompt, x_sample, state_conv, state_lru, state_hgrn, c_prompt, c_sample, w_ada, b_ada, w_in, conv_w,
           conv_b, lru_wa, lru_ba, lru_wx, lru_bx, lru_lambda, hg_lower, hg_norm_w, w_out, w_grp, b_grp, w_rt, b_rt,
           w_gate, w_up, w_down, final_norm_w):
    depth = w_in.shape[0]
    bp, tp, d = x_prompt.shape
    bs, ts, _ = x_sample.shape
    f32 = jnp.float32

    c_all = jnp.concatenate([c_prompt, c_sample], axis=0)
    rows = -(-c_all.shape[0] // SUBLANES) * SUBLANES
    c_all = jnp.pad(c_all, ((0, rows - c_all.shape[0]), (0, 0)))
    mod_all = _mod_call(c_all, w_ada, b_ada).reshape(depth, rows, 6, d)

    p_low = jax.nn.softmax(hg_lower.astype(f32), axis=0)
    lbs = jnp.cumsum(p_low, axis=0) - p_low[0]
    w_in_bf, w_out_bf = _bf(w_in), _bf(w_out)
    wr = jnp.concatenate([w_grp, w_rt], axis=-1)
    wr = jnp.pad(wr, ((0, 0), (0, 0), (0, ROUTE_LANES - wr.shape[-1])))
    wr_hi = _bf(wr)
    wr_hl = jnp.concatenate([wr_hi, _bf(wr - wr_hi.astype(f32))], axis=-1)
    br = jnp.concatenate([b_grp, b_rt], axis=-1)
    br = jnp.pad(br, ((0, 0), (0, ROUTE_LANES - br.shape[-1])))
    fw = final_norm_w.reshape(1, d)

    trunks = [
        dict(x=x_prompt, b0=0, nb=bp, t=tp,
             conv=jnp.zeros((depth, bp, CONV_W - 1, LRU_WIDTH), f32), lru=jnp.zeros((depth, bp, LRU_WIDTH), f32),
             hg=jnp.zeros((depth, bp, HG_HEADS, HG_DK, HG_DK), f32)),
        dict(x=x_sample, b0=bp, nb=bs, t=ts, conv=state_conv, lru=state_lru, hg=state_hgrn),
    ]
    for tr in trunks:
        tr["tt"] = _tile_rows(tr["t"])
        tr["chunk"] = min(64, tr["tt"])
        tr.update(convs=[], lrus=[], hgs=[])

    n_rows = sum(tr["nb"] * (tr["t"] // tr["tt"]) * _tile_slots_used(tr["tt"]) for tr in trunks)
    max_tiles = (n_rows + N_EXPERTS * (EXPERT_TILE - 1)) // EXPERT_TILE
    hs_zero = jnp.zeros((max_tiles * EXPERT_TILE, d), jnp.bfloat16)
    tile_ids = jnp.arange(max_tiles, dtype=jnp.int32)

    for l in range(depth):
        wts = (w_in_bf[l], conv_w[l], conv_b[l].reshape(1, -1), _bf(_block_diag_halves(lru_wa[l])),
               _bf(_block_diag_halves(lru_wx[l])), lru_ba[l].reshape(1, -1), lru_bx[l].reshape(1, -1),
               lru_lambda[l].reshape(1, -1), lbs[l].reshape(1, -1), hg_norm_w[l].reshape(1, -1), w_out_bf[l],
               wr_hi[l], wr_hl[l], br[l].reshape(1, -1))
        cnt = jnp.zeros((1, ROUTE_LANES), f32)
        for tr in trunks:
            mod = mod_all[l, tr["b0"]:tr["b0"] + tr["nb"]]
            tr["x"], tr["h2"], tr["route"], tr["tab"], cnt, conv_n, lru_n, hg_n = _mixer_call(
                tr["x"], mod, tr["conv"][l], tr["lru"][l], tr["hg"][l], wts, cnt, tt=tr["tt"], chunk=tr["chunk"])
            tr["mod"] = mod
            tr["convs"].append(conv_n)
            tr["lrus"].append(lru_n.reshape(tr["nb"], LRU_WIDTH))
            tr["hgs"].append(hg_n)

        counts = cnt[0, N_GROUPS:N_GROUPS + N_EXPERTS].astype(jnp.int32)
        tiles_e = (counts + (EXPERT_TILE - 1)) // EXPERT_TILE
        ends = jnp.cumsum(tiles_e)
        off = (ends - tiles_e) * EXPERT_TILE
        n_tiles = ends[-1:]
        tile_expert = jnp.sum(jnp.minimum(tile_ids, n_tiles - 1)[:, None] >= ends[None, :], axis=1).astype(jnp.int32)

        hs = hs_zero
        for tr in trunks:
            tr["tab"] = tr["tab"].reshape((-1,) + tr["tab"].shape[2:])
            tr["route"] = tr["route"].reshape(-1, ROUTE_LANES)
            hs = _dispatch_call(off, tr["tab"], tr["h2"].reshape(-1, d), tr["route"], hs,
                                n_slots=_tile_slots(tr["tt"]))
        ys = _expert_call(tile_expert, n_tiles, hs, w_gate, w_up, w_down, layer=l, te_rows=EXPERT_TILE)
        for tr in trunks:
            x_new = _combine_call(off, tr["tab"], tr["route"], tr["x"].reshape(-1, d), tr["mod"], fw, ys,
                                  n_slots=_tile_slots(tr["tt"]), final=(l == depth - 1))
            tr["x"] = x_new.reshape(tr["nb"], tr["t"], d)

    (yp, cp, lp, hp), (ys_, cs, ls, hs_) = [
        (tr["x"], jnp.stack(tr["convs"]), jnp.stack(tr["lrus"]), jnp.stack(tr["hgs"])) for tr in trunks]
    return (yp, ys_, cp, lp, hp, cs, ls, hs_)
```

```python
import functools

import jax
import jax.numpy as jnp
from jax import lax
from jax.experimental import pallas as pl
from jax.experimental.pallas import tpu as pltpu

D_MODEL = 1024
LRU_WIDTH = 512
LRU_BLOCKS = 8
LRU_C = 8.0
CONV_W = 4
HG_WIDTH = 512
HG_HEADS = 4
HG_DK = 128
N_GROUPS = 4
EXPERTS_PER_GROUP = 8
N_EXPERTS = 32
EXPERT_HIDDEN = 256
EPS = 1e-6

SUBLANES = 8
GRANULE = 16
LANES = 128
HG_SUB = 16
EXP_CLAMP = 80.0
ROUTE_LANES = LANES
EXPERT_TILE = 512
VMEM_LIMIT = 56 * 1024 * 1024

_NT = (((1,), (1,)), ((), ()))
_TN = (((0,), (0,)), ((), ()))


def _bf(x):
    return x.astype(jnp.bfloat16)


def _dot(a, b):
    return jnp.dot(a, b, preferred_element_type=jnp.float32)


def _sigmoid(x):
    return 1.0 / (1.0 + jnp.exp(-x))


def _silu(x):
    return x * _sigmoid(x)


def _gelu_tanh(x):
    return 0.5 * x * (1.0 + jnp.tanh(0.7978845608028654 * (x + 0.044715 * (x * x * x))))


def _softplus(x):
    return jnp.maximum(x, 0.0) + jnp.log(1.0 + jnp.exp(-jnp.abs(x)))


def _rms_rows(x):
    return x * lax.rsqrt(jnp.mean(x * x, axis=-1, keepdims=True) + EPS)


def _scan8(a, b):
    rows = lax.broadcasted_iota(jnp.int32, a.shape, 0)
    for d in (1, 2, 4):
        m = rows >= d
        a_sh = pltpu.roll(a, d, 0)
        b_sh = pltpu.roll(b, d, 0)
        b = jnp.where(m, a * b_sh + b, b)
        a = jnp.where(m, a * a_sh, a)
    return a, b


def _cumsum8(x):
    rows = lax.broadcasted_iota(jnp.int32, x.shape, 0)
    for d in (1, 2, 4):
        x = x + jnp.where(rows >= d, pltpu.roll(x, d, 0), 0.0)
    return x


def _block_refs(b, m):
    n, w = b.shape
    starts, ends = [], []
    for i in range(n // m):
        if i == 0:
            starts.append(jnp.zeros((m, w), jnp.float32))
        else:
            starts.append(jnp.broadcast_to(b[i * m - 1:i * m, :], (m, w)))
        ends.append(jnp.broadcast_to(b[(i + 1) * m - 1:(i + 1) * m, :], (m, w)))
    if len(starts) == 1:
        return starts[0], ends[0]
    return jnp.concatenate(starts, axis=0), jnp.concatenate(ends, axis=0)


def _mixer_kernel(*refs, tt, chunk):
    it = iter(refs)
    x_ref = next(it)
    (mod_ref, conv0_ref, lru0_ref, hg0_ref, w_in_ref, conv_w_ref, conv_b_ref, wa_ref, wx_ref, ba_ref, bx_ref,
     lam_ref, lbs_ref, hgn_ref, w_out_ref, wr_hi_ref, wr_hl_ref, br_ref, tri_ref, upper_ref, cnt_in_ref) = (
        next(it) for _ in range(21))
    (xmid_ref, h2_ref, route_ref, tab_ref, cnt_out_ref, conv_out_ref, lru_out_ref,
     hg_out_ref) = (next(it) for _ in range(8))
    conv_scr, lru_scr, st_scr, cnt_scr = (next(it) for _ in range(4))

    bi = pl.program_id(0)
    j = pl.program_id(1)
    last_b = pl.num_programs(0) - 1
    last_j = pl.num_programs(1) - 1
    pad = SUBLANES - (CONV_W - 1)

    @pl.when((bi == 0) & (j == 0))
    def _():
        cnt_scr[...] = cnt_in_ref[...]

    @pl.when(j == 0)
    def _():
        conv_scr[pad:SUBLANES, :] = conv0_ref[0]
        lru_scr[...] = lru0_ref[0]
        for hd in range(HG_HEADS):
            st_scr[hd] = hg0_ref[0, hd].T

    x = x_ref[0]
    mod = mod_ref[0]
    sh1, sc1, g1, sh2, sc2 = (mod[i:i + 1, :] for i in range(5))

    h = _rms_rows(x) * (1.0 + sc1) + sh1
    h_bf = _bf(h)
    lw, hw = LRU_WIDTH, HG_WIDTH
    proj = lambda lo, width: _dot(h_bf, w_in_ref[:, lo:lo + width])
    xb = proj(0, lw)
    gb = proj(lw, lw)

    conv_scr[SUBLANES:SUBLANES + tt, :] = xb
    xc = conv_b_ref[...]
    for k in range(CONV_W):
        xc = xc + conv_w_ref[k:k + 1, :] * conv_scr[pl.ds(pad + k, tt), :]
    new_hist = conv_scr[pl.ds(tt + pad, CONV_W - 1), :]
    conv_scr[pad:SUBLANES, :] = new_hist
    qh = proj(2 * lw, hw)

    xc_bf = _bf(xc)
    half = lw // 2
    r_pre = jnp.concatenate([_dot(xc_bf[:, i * half:(i + 1) * half], wa_ref[i]) for i in range(2)], axis=1)
    i_pre = jnp.concatenate([_dot(xc_bf[:, i * half:(i + 1) * half], wx_ref[i]) for i in range(2)], axis=1)
    r = _sigmoid(r_pre + ba_ref[...])
    ig = _sigmoid(i_pre + bx_ref[...])
    fh = proj(2 * lw + hw, hw)
    log_a = (-LRU_C) * r * _softplus(-lam_ref[...])
    a = jnp.exp(log_a)
    th = jnp.tanh(log_a)
    one_minus_a2 = (-2.0) * th / (1.0 - th)
    bt = jnp.sqrt(one_minus_a2) * ig * xc
    ih = proj(2 * lw + 2 * hw, hw)

    carry = lru_scr[...]
    hs = []
    for g in range(tt // SUBLANES):
        sl = slice(g * SUBLANES, (g + 1) * SUBLANES)
        acum, hloc = _scan8(a[sl], bt[sl])
        hg = acum * carry + hloc
        hs.append(hg)
        carry = hg[SUBLANES - 1:SUBLANES, :]
    lru_scr[...] = carry
    gh = proj(2 * lw + 3 * hw, hw)
    hl = jnp.concatenate(hs, axis=0)
    y_lru = hl * _gelu_tanh(gb)
    mix_lru = _dot(_bf(y_lru), w_out_ref[:lw, :])

    lbs = lbs_ref[...]
    q = _silu(qh) * (HG_DK ** -0.5)
    f = lbs + (1.0 - lbs) * _sigmoid(fh)
    kk = 1.0 - f
    glog = jnp.log(f)

    ti = lax.broadcasted_iota(jnp.int32, (chunk, chunk), 0)
    si = lax.broadcasted_iota(jnp.int32, (chunk, chunk), 1)
    sub_shift = HG_SUB.bit_length() - 1
    mask_diag = ((ti >> sub_shift) == (si >> sub_shift)) & (ti >= si)
    levels = []
    m = HG_SUB
    while m < chunk:
        sh = m.bit_length() - 1
        levels.append((m, ((ti >> (sh + 1)) == (si >> (sh + 1))) & (((ti >> sh) & 1) == 1) & (((si >> sh) & 1) == 0)))
        m *= 2

    o_chunks = []
    for c in range(tt // chunk):
        rs = slice(c * chunk, (c + 1) * chunk)
        gc = glog[rs]
        bs, bcarry = [], None
        for g in range(chunk // SUBLANES):
            cs = _cumsum8(gc[g * SUBLANES:(g + 1) * SUBLANES])
            if bcarry is not None:
                cs = cs + bcarry
            bs.append(cs)
            bcarry = cs[SUBLANES - 1:SUBLANES, :]
        b = jnp.concatenate(bs, axis=0)
        b_last = bcarry
        qc, kc, vc = q[rs], kk[rs], ih[rs]

        st16, en16 = _block_refs(b, HG_SUB)
        q_lv = {HG_SUB: _bf(qc * jnp.exp(b - st16))}
        k_lv = {HG_SUB: _bf(kc * jnp.exp(en16 - b))}
        k_diag = _bf(kc * jnp.exp(jnp.minimum(st16 - b, EXP_CLAMP)))
        for m, _ in levels:
            if m == HG_SUB:
                continue
            st, en = _block_refs(b, m)
            q_lv[m] = _bf(qc * jnp.exp(b - st))
            k_lv[m] = _bf(kc * jnp.exp(en - b))
        q_all = _bf(qc * jnp.exp(b))
        k_all = _bf(kc * jnp.exp(b_last - b))
        v_bf = _bf(vc)
        eb_last = jnp.exp(b_last)

        heads = [slice(hd * HG_DK, (hd + 1) * HG_DK) for hd in range(HG_HEADS)]
        nt_dot = lambda lhs, rhs: lax.dot_general(lhs, rhs, _NT, preferred_element_type=jnp.float32)
        a_diag = [nt_dot(q_lv[HG_SUB][:, hs_], k_diag[:, hs_]) for hs_ in heads]
        a_lvl = [[nt_dot(q_lv[m][:, hs_], k_lv[m][:, hs_]) for m, _ in levels] for hs_ in heads]
        st_old = [st_scr[hd] for hd in range(HG_HEADS)]
        o_inter = [nt_dot(q_all[:, hs_], _bf(st_t)) for hs_, st_t in zip(heads, st_old)]
        kv = [lax.dot_general(v_bf[:, hs_], k_all[:, hs_], _TN, preferred_element_type=jnp.float32)
              for hs_ in heads]
        amats = []
        for hd in range(HG_HEADS):
            amat = jnp.where(mask_diag, a_diag[hd], 0.0)
            for (m, msk), a_m in zip(levels, a_lvl[hd]):
                amat = amat + jnp.where(msk, a_m, 0.0)
            amats.append(_bf(amat))
        o_intra = [_dot(amat, v_bf[:, hs_]) for amat, hs_ in zip(amats, heads)]
        o_heads = []
        for hd, hs_ in enumerate(heads):
            st_scr[hd] = st_old[hd] * eb_last[:, hs_] + kv[hd]
            o_heads.append(_rms_rows(o_inter[hd] + o_intra[hd]) * hgn_ref[...])
        o_chunks.append(jnp.concatenate(o_heads, axis=1))
    o_all = o_chunks[0] if len(o_chunks) == 1 else jnp.concatenate(o_chunks, axis=0)
    y_hg = o_all * _silu(gh)

    mix = mix_lru + _dot(_bf(y_hg), w_out_ref[lw:, :])
    x_mid = x + g1 * mix
    xmid_ref[0] = x_mid

    h2 = _rms_rows(x_mid) * (1.0 + sc2) + sh2
    h2_hi = _bf(h2)
    h2_lo = _bf(h2 - h2_hi.astype(jnp.float32))
    h2_ref[0] = h2_hi
    hi_terms = _dot(h2_hi, wr_hl_ref[...])
    logits = (hi_terms[:, :ROUTE_LANES] + hi_terms[:, ROUTE_LANES:] + _dot(h2_lo, wr_hi_ref[...])
              + br_ref[...])

    lane = lax.broadcasted_iota(jnp.int32, (tt, ROUTE_LANES), 1)
    lane_f = lane.astype(jnp.float32)
    neg, big = -1e30, 1e6
    is_grp = lane < N_GROUPS
    gl = jnp.where(is_grp, logits, neg)
    gmax = jnp.max(gl, axis=-1, keepdims=True)
    gsel = jnp.min(jnp.where(gl == gmax, lane_f, big), axis=-1, keepdims=True)
    gsum = jnp.sum(jnp.where(is_grp, jnp.exp(gl - gmax), 0.0), axis=-1, keepdims=True)
    p_grp = 1.0 / gsum
    lo = N_GROUPS + gsel * EXPERTS_PER_GROUP
    emask = (lane_f >= lo) & (lane_f < lo + EXPERTS_PER_GROUP)
    el = jnp.where(emask, logits, neg)
    tv1 = jnp.max(el, axis=-1, keepdims=True)
    ti1 = jnp.min(jnp.where(emask & (el == tv1), lane_f, big), axis=-1, keepdims=True)
    emask2 = emask & (lane_f != ti1)
    el2 = jnp.where(emask2, logits, neg)
    tv2 = jnp.max(el2, axis=-1, keepdims=True)
    ti2 = jnp.min(jnp.where(emask2 & (el2 == tv2), lane_f, big), axis=-1, keepdims=True)
    e21 = jnp.exp(tv2 - tv1)
    w1 = p_grp / (1.0 + e21)
    w2 = w1 * e21
    hot1 = lane_f == ti1
    hot2 = lane_f == ti2
    hot12 = jnp.where(hot1 | hot2, 1.0, 0.0)
    before = _dot(tri_ref[...], _bf(hot12))
    c_tile = before[tt - 1:tt, :] + hot12[tt - 1:tt, :]
    c_pad = jnp.floor((c_tile + (GRANULE - 1.0)) * (1.0 / GRANULE)) * GRANULE
    lstart = _dot(_bf(jnp.broadcast_to(c_pad, (SUBLANES, ROUTE_LANES))), upper_ref[...])[0:1, :]
    pos = before + lstart
    slot1 = jnp.sum(jnp.where(hot1, pos, 0.0), axis=-1, keepdims=True)
    slot2 = jnp.sum(jnp.where(hot2, pos, 0.0), axis=-1, keepdims=True)
    gbase = cnt_scr[...]
    cnt_scr[...] = gbase + c_pad

    per_expert = jnp.concatenate([lstart, c_pad, gbase, jnp.zeros((LANES - 3, ROUTE_LANES), jnp.float32)], axis=0).T
    lstart_c, c_pad_c, gbase_c = per_expert[:, 0:1], per_expert[:, 1:2], per_expert[:, 2:3]
    erow = lax.broadcasted_iota(jnp.int32, (LANES, ROUTE_LANES), 0)
    is_expert = (erow >= N_GROUPS) & (erow < N_GROUPS + N_EXPERTS)
    ended = is_expert & ((lstart_c + c_pad_c) * (1.0 / GRANULE) <= lane_f[0:1, :])
    owner = jnp.sum(jnp.where(ended, 1.0, 0.0), axis=0, keepdims=True)
    mine = (erow - N_GROUPS).astype(jnp.float32) == owner
    dst_row = jnp.sum(jnp.where(mine, gbase_c - lstart_c, 0.0), axis=0, keepdims=True) + GRANULE * lane_f[0:1, :]
    n_gran = jnp.broadcast_to(jnp.sum(c_pad, axis=-1, keepdims=True) * (1.0 / GRANULE), (1, ROUTE_LANES))
    tab_ref[0, 0] = jnp.concatenate(
        [owner, dst_row, n_gran, jnp.zeros((SUBLANES - 3, ROUTE_LANES), jnp.float32)], axis=0).astype(jnp.int32)
    route_ref[0] = jnp.where(lane == 0, w1, jnp.where(lane == 1, w2, jnp.where(lane == 2, slot1,
                                                                                jnp.where(lane == 3, slot2, 0.0))))

    @pl.when(j == last_j)
    def _():
        conv_out_ref[0] = new_hist
        lru_out_ref[0] = carry
        for hd in range(HG_HEADS):
            hg_out_ref[0, hd] = st_scr[hd].T

    @pl.when((bi == last_b) & (j == last_j))
    def _():
        cnt_out_ref[...] = cnt_scr[...]


def _mixer_call(x, mod, conv0, lru0, hg0, wts, cnt_in, *, tt, chunk):
    bsz, t, d = x.shape
    grid = (bsz, t // tt)
    tile = lambda last: pl.BlockSpec((1, tt, last), lambda b, j: (b, j, 0))
    per_b = lambda *shape: pl.BlockSpec((1,) + shape, lambda b, j: (b,) + (0,) * len(shape))
    full = lambda a: pl.BlockSpec(a.shape, lambda b, j: (0,) * a.ndim)
    tri = _bf(jnp.tril(jnp.ones((tt, tt), jnp.float32), -1))
    upper = _bf(jnp.triu(jnp.ones((ROUTE_LANES, ROUTE_LANES), jnp.float32), 1))
    per_tile = lambda *shape: pl.BlockSpec((1, 1) + shape, lambda b, j: (b, j) + (0,) * len(shape))

    args, in_specs = [x], [tile(d)]
    args += [mod, conv0, lru0.reshape(bsz, 1, LRU_WIDTH), hg0]
    in_specs += [per_b(6, d), per_b(CONV_W - 1, LRU_WIDTH), per_b(1, LRU_WIDTH), per_b(HG_HEADS, HG_DK, HG_DK)]
    args += list(wts) + [tri, upper, cnt_in]
    in_specs += [full(w) for w in wts] + [full(tri), full(upper), full(cnt_in)]

    out_shape = (
        jax.ShapeDtypeStruct((bsz, t, d), jnp.float32),
        jax.ShapeDtypeStruct((bsz, t, d), jnp.bfloat16),
        jax.ShapeDtypeStruct((bsz, t, ROUTE_LANES), jnp.float32),
        jax.ShapeDtypeStruct((bsz, t // tt, SUBLANES, ROUTE_LANES), jnp.int32),
        jax.ShapeDtypeStruct((1, ROUTE_LANES), jnp.float32),
        jax.ShapeDtypeStruct((bsz, CONV_W - 1, LRU_WIDTH), jnp.float32),
        jax.ShapeDtypeStruct((bsz, 1, LRU_WIDTH), jnp.float32),
        jax.ShapeDtypeStruct((bsz, HG_HEADS, HG_DK, HG_DK), jnp.float32),
    )
    out_specs = (tile(d), tile(d), tile(ROUTE_LANES), per_tile(SUBLANES, ROUTE_LANES),
                 pl.BlockSpec((1, ROUTE_LANES), lambda b, j: (0, 0)),
                 per_b(CONV_W - 1, LRU_WIDTH), per_b(1, LRU_WIDTH), per_b(HG_HEADS, HG_DK, HG_DK))
    scratch = [
        pltpu.VMEM((tt + SUBLANES, LRU_WIDTH), jnp.float32),
        pltpu.VMEM((1, LRU_WIDTH), jnp.float32),
        pltpu.VMEM((HG_HEADS, HG_DK, HG_DK), jnp.float32),
        pltpu.VMEM((1, ROUTE_LANES), jnp.float32),
    ]
    return pl.pallas_call(
        functools.partial(_mixer_kernel, tt=tt, chunk=chunk),
        grid=grid, in_specs=in_specs, out_specs=out_specs, out_shape=out_shape, scratch_shapes=scratch,
        compiler_params=pltpu.CompilerParams(dimension_semantics=("arbitrary", "arbitrary"),
                                             vmem_limit_bytes=VMEM_LIMIT),
        name="mixer",
    )(*args)


def _mod_kernel(c_ref, w_ref, b_ref, o_ref):
    s = _silu(c_ref[...])
    o_ref[0] = jnp.dot(s, w_ref[0], preferred_element_type=jnp.float32,
                       precision=lax.Precision.HIGHEST) + b_ref[0]


def _mod_call(c_all, w_ada, b_ada):
    depth, d, n = w_ada.shape
    rows = c_all.shape[0]
    tn = d
    return pl.pallas_call(
        _mod_kernel,
        grid=(depth, n // tn),
        in_specs=[pl.BlockSpec((rows, d), lambda l, i: (0, 0)),
                  pl.BlockSpec((1, d, tn), lambda l, i: (l, 0, i)),
                  pl.BlockSpec((1, 1, tn), lambda l, i: (l, 0, i))],
        out_specs=pl.BlockSpec((1, rows, tn), lambda l, i: (l, 0, i)),
        out_shape=jax.ShapeDtypeStruct((depth, rows, n), jnp.float32),
        compiler_params=pltpu.CompilerParams(dimension_semantics=("arbitrary", "arbitrary"),
                                             vmem_limit_bytes=VMEM_LIMIT),
        name="modulation",
    )(c_all, w_ada, b_ada.reshape(depth, 1, n))


def _for_each_granule(off_ref, tab_ref, fn):
    n_gran = tab_ref[0, 2, 0]

    def per_granule(g, c):
        fn(pl.multiple_of(g * GRANULE, GRANULE),
           pl.multiple_of(off_ref[tab_ref[0, 0, g]] + tab_ref[0, 1, g], GRANULE))
        return c

    lax.fori_loop(0, n_gran, per_granule, 0)
    return n_gran


def _wait_granules(n, n_max, wait_fn):
    bit = 1 << (n_max.bit_length() - 1)
    while bit:
        pl.when((n & bit) != 0)(functools.partial(wait_fn, bit))
        bit >>= 1


def _dispatch_kernel(off_ref, tab_ref, h_ref, route_ref, hs_in_ref, hs_ref, srt, sem, pend):
    del hs_in_ref
    i = pl.program_id(0)
    cur = lax.rem(i, 2)
    tt, n_slots = h_ref.shape[0], srt.shape[1]

    def granule_copy(buf, src, dst, rows=GRANULE):
        return pltpu.make_async_copy(srt.at[buf, pl.ds(src, rows), :],
                                     hs_ref.at[pl.ds(dst, rows), :], sem.at[buf])

    def drain(buf, n):
        _wait_granules(n, n_slots // GRANULE, lambda count: granule_copy(buf, 0, 0, count * GRANULE).wait())

    @pl.when(i == 0)
    def _():
        pend[0] = 0

    rec = route_ref[...]
    if tt < LANES:
        rec = jnp.concatenate([rec, jnp.zeros((LANES - tt, ROUTE_LANES), jnp.float32)], axis=0)
    rec_t = rec.T
    s1_row, s2_row = rec_t[2:3, :tt], rec_t[3:4, :tt]
    slot_iota = lax.broadcasted_iota(jnp.int32, (n_slots, tt), 0).astype(jnp.float32)
    perm = _bf(jnp.where((slot_iota == s1_row) | (slot_iota == s2_row), 1.0, 0.0))
    srt[cur] = _bf(_dot(perm, h_ref[...]))

    earlier = pend[0]
    pend[0] = _for_each_granule(off_ref, tab_ref, lambda s, d: granule_copy(cur, s, d).start())
    drain(1 - cur, earlier)

    @pl.when(i == pl.num_programs(0) - 1)
    def _():
        drain(cur, pend[0])


def _dispatch_call(off, tab, h2, route, hs, *, n_slots):
    n = tab.shape[0]
    tt = h2.shape[0] // n
    d = h2.shape[1]
    grid_spec = pltpu.PrefetchScalarGridSpec(
        num_scalar_prefetch=1, grid=(n,),
        in_specs=[pl.BlockSpec((1, SUBLANES, ROUTE_LANES), lambda i, off: (i, 0, 0), memory_space=pltpu.SMEM),
                  pl.BlockSpec((tt, d), lambda i, off: (i, 0)),
                  pl.BlockSpec((tt, ROUTE_LANES), lambda i, off: (i, 0)),
                  pl.BlockSpec(memory_space=pl.ANY)],
        out_specs=pl.BlockSpec(memory_space=pl.ANY),
        scratch_shapes=[pltpu.VMEM((2, n_slots, d), jnp.bfloat16), pltpu.SemaphoreType.DMA((2,)),
                        pltpu.SMEM((1,), jnp.int32)])
    return pl.pallas_call(
        _dispatch_kernel,
        grid_spec=grid_spec,
        out_shape=jax.ShapeDtypeStruct(hs.shape, hs.dtype),
        input_output_aliases={4: 0},
        compiler_params=pltpu.CompilerParams(dimension_semantics=("arbitrary",), vmem_limit_bytes=VMEM_LIMIT),
        name="dispatch",
    )(off, tab, h2, route, hs)


def _expert_kernel(te_ref, nt_ref, hs_ref, wg_ref, wu_ref, wd_ref, ys_ref, wgu_bf, wd_bf):
    i = pl.program_id(0)
    active = i < nt_ref[0]

    @pl.when(active & ((i == 0) | (te_ref[i] != te_ref[jnp.maximum(i - 1, 0)])))
    def _():
        wgu_bf[:, :EXPERT_HIDDEN] = _bf(wg_ref[0, 0])
        wgu_bf[:, EXPERT_HIDDEN:] = _bf(wu_ref[0, 0])
        wd_bf[...] = _bf(wd_ref[0, 0])

    @pl.when(active)
    def _():
        gu = _dot(hs_ref[...], wgu_bf[...])
        hid = _silu(gu[:, :EXPERT_HIDDEN]) * gu[:, EXPERT_HIDDEN:]
        ys_ref[...] = _bf(_dot(_bf(hid), wd_bf[...]))


def _expert_call(tile_expert, n_tiles, hs, w_gate, w_up, w_down, *, layer, te_rows):
    npad, d = hs.shape
    wspec = lambda a: pl.BlockSpec((1, 1) + a.shape[2:], lambda i, te, nt: (layer, te[i], 0, 0))
    rows = pl.BlockSpec((te_rows, d), lambda i, te, nt: (jnp.minimum(i, nt[0] - 1), 0))
    grid_spec = pltpu.PrefetchScalarGridSpec(
        num_scalar_prefetch=2, grid=(npad // te_rows,),
        in_specs=[rows, wspec(w_gate), wspec(w_up), wspec(w_down)],
        out_specs=rows,
        scratch_shapes=[pltpu.VMEM((d, 2 * EXPERT_HIDDEN), jnp.bfloat16),
                        pltpu.VMEM((EXPERT_HIDDEN, d), jnp.bfloat16)])
    return pl.pallas_call(
        _expert_kernel,
        grid_spec=grid_spec,
        out_shape=jax.ShapeDtypeStruct((npad, d), jnp.bfloat16),
        input_output_aliases={2: 0},
        compiler_params=pltpu.CompilerParams(dimension_semantics=("arbitrary",), vmem_limit_bytes=VMEM_LIMIT),
        name="experts",
    )(tile_expert, n_tiles, hs, w_gate, w_up, w_down)


def _combine_kernel(off_ref, tab_ref, tab_next_ref, route_ref, x_ref, mod_ref, fw_ref, ys_ref, o_ref, yloc, sem,
                    pend, *, final):
    i = pl.program_id(0)
    last = pl.num_programs(0) - 1
    cur = lax.rem(i, 2)

    def granule_copy(buf, loc, srt, rows=GRANULE):
        return pltpu.make_async_copy(ys_ref.at[pl.ds(srt, rows), :],
                                     yloc.at[buf, pl.ds(loc, rows), :], sem.at[buf])

    def gather(tab, buf):
        return _for_each_granule(off_ref, tab, lambda s, d: granule_copy(buf, s, d).start())

    @pl.when(i == 0)
    def _():
        yloc[...] = jnp.zeros_like(yloc)
        pend[0] = gather(tab_ref, 0)

    n_cur = pend[0]

    @pl.when(i < last)
    def _():
        pend[0] = gather(tab_next_ref, 1 - cur)

    _wait_granules(n_cur, yloc.shape[1] // GRANULE, lambda count: granule_copy(cur, 0, 0, count * GRANULE).wait())

    rt = route_ref[...]
    tt, n_slots = rt.shape[0], yloc.shape[1]
    slot = lax.broadcasted_iota(jnp.int32, (tt, n_slots), 1).astype(jnp.float32)
    wc = jnp.where(slot == rt[:, 2:3], rt[:, 0:1], 0.0) + jnp.where(slot == rt[:, 3:4], rt[:, 1:2], 0.0)
    x_new = x_ref[...] + mod_ref[0, 5:6, :] * _dot(_bf(wc), yloc[cur])
    o_ref[...] = _rms_rows(x_new) * fw_ref[...] if final else x_new


def _combine_call(off, tab, route, x_mid, mod, fw, ys, *, n_slots, final):
    n = tab.shape[0]
    tt = route.shape[0] // n
    tiles_per_seq = n // mod.shape[0]
    d = ys.shape[-1]
    tab_spec = lambda imap: pl.BlockSpec((1, SUBLANES, ROUTE_LANES), imap, memory_space=pltpu.SMEM)
    grid_spec = pltpu.PrefetchScalarGridSpec(
        num_scalar_prefetch=1, grid=(n,),
        in_specs=[tab_spec(lambda i, off: (i, 0, 0)),
                  tab_spec(lambda i, off: (jnp.minimum(i + 1, n - 1), 0, 0)),
                  pl.BlockSpec((tt, ROUTE_LANES), lambda i, off: (i, 0)),
                  pl.BlockSpec((tt, d), lambda i, off: (i, 0)),
                  pl.BlockSpec((1,) + mod.shape[1:], lambda i, off: (i // tiles_per_seq, 0, 0)),
                  pl.BlockSpec((1, d), lambda i, off: (0, 0)),
                  pl.BlockSpec(memory_space=pl.ANY)],
        out_specs=pl.BlockSpec((tt, d), lambda i, off: (i, 0)),
        scratch_shapes=[pltpu.VMEM((2, n_slots, d), jnp.bfloat16), pltpu.SemaphoreType.DMA((2,)),
                        pltpu.SMEM((1,), jnp.int32)])
    return pl.pallas_call(
        functools.partial(_combine_kernel, final=final),
        grid_spec=grid_spec,
        out_shape=jax.ShapeDtypeStruct((n * tt, d), jnp.float32),
        compiler_params=pltpu.CompilerParams(dimension_semantics=("arbitrary",), vmem_limit_bytes=VMEM_LIMIT),
        name="combine",
    )(off, tab, tab, route, x_mid, mod, fw, ys)


def _block_diag_halves(w):
    nb, bd, _ = w.shape
    per = nb // 2
    eye = jnp.eye(per, dtype=w.dtype)
    halves = [jnp.einsum('nij,nm->nimj', w[h * per:(h + 1) * per], eye).reshape(per * bd, per * bd) for h in range(2)]
    return jnp.stack(halves)


def _tile_slots(tt):
    return -(-_tile_slots_used(tt) // LANES) * LANES


def _tile_slots_used(tt):
    return 2 * tt + (GRANULE - 1) * min(N_EXPERTS, 2 * tt)


def _tile_rows(t):
    for cand in (512, 256, 128, 64, 32, 16, 8):
        if t % cand == 0:
            return cand
    raise ValueError(f"sequence length {t} is not a multiple of 8")


def kernel(x_prompt, x_sample, state_conv, state_lru, state_hgrn, c_prompt, c_sample, w_ada, b_ada, w_in, conv_w,
           conv_b, lru_wa, lru_ba, lru_wx, lru_bx, lru_lambda, hg_lower, hg_norm_w, w_out, w_grp, b_grp, w_rt, b_rt,
           w_gate, w_up, w_down, final_norm_w):
    depth = w_in.shape[0]
    bp, tp, d = x_prompt.shape
    bs, ts, _ = x_sample.shape
    f32 = jnp.float32

    c_all = jnp.concatenate([c_prompt, c_sample], axis=0)
    rows = -(-c_all.shape[0] // SUBLANES) * SUBLANES
    c_all = jnp.pad(c_all, ((0, rows - c_all.shape[0]), (0, 0)))
    mod_all = _mod_call(c_all, w_ada, b_ada).reshape(depth, rows, 6, d)

    p_low = jax.nn.softmax(hg_lower.astype(f32), axis=0)
    lbs = jnp.cumsum(p_low, axis=0) - p_low[0]
    w_in_bf, w_out_bf = _bf(w_in), _bf(w_out)
    wr = jnp.concatenate([w_grp, w_rt], axis=-1)
    wr = jnp.pad(wr, ((0, 0), (0, 0), (0, ROUTE_LANES - wr.shape[-1])))
    wr_hi = _bf(wr)
    wr_hl = jnp.concatenate([wr_hi, _bf(wr - wr_hi.astype(f32))], axis=-1)
    br = jnp.concatenate([b_grp, b_rt], axis=-1)
    br = jnp.pad(br, ((0, 0), (0, ROUTE_LANES - br.shape[-1])))
    fw = final_norm_w.reshape(1, d)

    trunks = [
        dict(x=x_prompt, b0=0, nb=bp, t=tp,
             conv=jnp.zeros((depth, bp, CONV_W - 1, LRU_WIDTH), f32), lru=jnp.zeros((depth, bp, LRU_WIDTH), f32),
             hg=jnp.zeros((depth, bp, HG_HEADS, HG_DK, HG_DK), f32)),
        dict(x=x_sample, b0=bp, nb=bs, t=ts, conv=state_conv, lru=state_lru, hg=state_hgrn),
    ]
    for tr in trunks:
        tr["tt"] = _tile_rows(tr["t"])
        tr["chunk"] = min(64, tr["tt"])
        tr.update(convs=[], lrus=[], hgs=[])

    n_rows = sum(tr["nb"] * (tr["t"] // tr["tt"]) * _tile_slots_used(tr["tt"]) for tr in trunks)
    max_tiles = (n_rows + N_EXPERTS * (EXPERT_TILE - 1)) // EXPERT_TILE
    hs_zero = jnp.zeros((max_tiles * EXPERT_TILE, d), jnp.bfloat16)
    tile_ids = jnp.arange(max_tiles, dtype=jnp.int32)

    for l in range(depth):
        wts = (w_in_bf[l], conv_w[l], conv_b[l].reshape(1, -1), _bf(_block_diag_halves(lru_wa[l])),
               _bf(_block_diag_halves(lru_wx[l])), lru_ba[l].reshape(1, -1), lru_bx[l].reshape(1, -1),
               lru_lambda[l].reshape(1, -1), lbs[l].reshape(1, -1), hg_norm_w[l].reshape(1, -1), w_out_bf[l],
               wr_hi[l], wr_hl[l], br[l].reshape(1, -1))
        cnt = jnp.zeros((1, ROUTE_LANES), f32)
        for tr in trunks:
            mod = mod_all[l, tr["b0"]:tr["b0"] + tr["nb"]]
            tr["x"], tr["h2"], tr["route"], tr["tab"], cnt, conv_n, lru_n, hg_n = _mixer_call(
                tr["x"], mod, tr["conv"][l], tr["lru"][l], tr["hg"][l], wts, cnt, tt=tr["tt"], chunk=tr["chunk"])
            tr["mod"] = mod
            tr["convs"].append(conv_n)
            tr["lrus"].append(lru_n.reshape(tr["nb"], LRU_WIDTH))
            tr["hgs"].append(hg_n)

        counts = cnt[0, N_GROUPS:N_GROUPS + N_EXPERTS].astype(jnp.int32)
        tiles_e = (counts + (EXPERT_TILE - 1)) // EXPERT_TILE
        ends = jnp.cumsum(tiles_e)
        off = (ends - tiles_e) * EXPERT_TILE
        n_tiles = ends[-1:]
        tile_expert = jnp.sum(jnp.minimum(tile_ids, n_tiles - 1)[:, None] >= ends[None, :], axis=1).astype(jnp.int32)

        hs = hs_zero
        for tr in trunks:
            tr["tab"] = tr["tab"].reshape((-1,) + tr["tab"].shape[2:])
            tr["route"] = tr["route"].reshape(-1, ROUTE_LANES)
            hs = _dispatch_call(off, tr["tab"], tr["h2"].reshape(-1, d), tr["route"], hs,
                                n_slots=_tile_slots(tr["tt"]))
        ys = _expert_call(tile_expert, n_tiles, hs, w_gate, w_up, w_down, layer=l, te_rows=EXPERT_TILE)
        for tr in trunks:
            x_new = _combine_call(off, tr["tab"], tr["route"], tr["x"].reshape(-1, d), tr["mod"], fw, ys,
                                  n_slots=_tile_slots(tr["tt"]), final=(l == depth - 1))
            tr["x"] = x_new.reshape(tr["nb"], tr["t"], d)

    (yp, cp, lp, hp), (ys_, cs, ls, hs_) = [
        (tr["x"], jnp.stack(tr["convs"]), jnp.stack(tr["lrus"]), jnp.stack(tr["hgs"])) for tr in trunks]
    return (yp, ys_, cp, lp, hp, cs, ls, hs_)
```

```python
import functools

import jax
import jax.numpy as jnp
from jax import lax
from jax.experimental import pallas as pl
from jax.experimental.pallas import tpu as pltpu

D_MODEL = 1024
LRU_WIDTH = 512
LRU_BLOCKS = 8
LRU_C = 8.0
CONV_W = 4
HG_WIDTH = 512
HG_HEADS = 4
HG_DK = 128
N_GROUPS = 4
EXPERTS_PER_GROUP = 8
N_EXPERTS = 32
EXPERT_HIDDEN = 256
EPS = 1e-6

SUBLANES = 8
GRANULE = 16
LANES = 128
HG_SUB = 16
EXP_CLAMP = 80.0
ROUTE_LANES = LANES
EXPERT_TILE = 512
VMEM_LIMIT = 56 * 1024 * 1024

_NT = (((1,), (1,)), ((), ()))
_TN = (((0,), (0,)), ((), ()))


def _bf(x):
    return x.astype(jnp.bfloat16)


def _dot(a, b):
    return jnp.dot(a, b, preferred_element_type=jnp.float32)


def _sigmoid(x):
    return 1.0 / (1.0 + jnp.exp(-x))


def _silu(x):
    return x * _sigmoid(x)


def _gelu_tanh(x):
    return 0.5 * x * (1.0 + jnp.tanh(0.7978845608028654 * (x + 0.044715 * (x * x * x))))


def _softplus(x):
    return jnp.maximum(x, 0.0) + jnp.log(1.0 + jnp.exp(-jnp.abs(x)))


def _rms_rows(x):
    return x * lax.rsqrt(jnp.mean(x * x, axis=-1, keepdims=True) + EPS)


def _scan8(a, b):
    rows = lax.broadcasted_iota(jnp.int32, a.shape, 0)
    for d in (1, 2, 4):
        m = rows >= d
        a_sh = pltpu.roll(a, d, 0)
        b_sh = pltpu.roll(b, d, 0)
        b = jnp.where(m, a * b_sh + b, b)
        a = jnp.where(m, a * a_sh, a)
    return a, b


def _cumsum8(x):
    rows = lax.broadcasted_iota(jnp.int32, x.shape, 0)
    for d in (1, 2, 4):
        x = x + jnp.where(rows >= d, pltpu.roll(x, d, 0), 0.0)
    return x


def _block_refs(b, m):
    n, w = b.shape
    starts, ends = [], []
    for i in range(n // m):
        if i == 0:
            starts.append(jnp.zeros((m, w), jnp.float32))
        else:
            starts.append(jnp.broadcast_to(b[i * m - 1:i * m, :], (m, w)))
        ends.append(jnp.broadcast_to(b[(i + 1) * m - 1:(i + 1) * m, :], (m, w)))
    if len(starts) == 1:
        return starts[0], ends[0]
    return jnp.concatenate(starts, axis=0), jnp.concatenate(ends, axis=0)


def _mixer_kernel(*refs, tt, chunk):
    it = iter(refs)
    x_ref = next(it)
    (mod_ref, conv0_ref, lru0_ref, hg0_ref, w_in_ref, conv_w_ref, conv_b_ref, wa_ref, wx_ref, ba_ref, bx_ref,
     lam_ref, lbs_ref, hgn_ref, w_out_ref, wr_hi_ref, wr_hl_ref, br_ref, tri_ref, upper_ref, cnt_in_ref) = (
        next(it) for _ in range(21))
    (xmid_ref, h2_ref, route_ref, tab_ref, cnt_out_ref, conv_out_ref, lru_out_ref,
     hg_out_ref) = (next(it) for _ in range(8))
    conv_scr, lru_scr, st_scr, cnt_scr = (next(it) for _ in range(4))

    bi = pl.program_id(0)
    j = pl.program_id(1)
    last_b = pl.num_programs(0) - 1
    last_j = pl.num_programs(1) - 1
    pad = SUBLANES - (CONV_W - 1)

    @pl.when((bi == 0) & (j == 0))
    def _():
        cnt_scr[...] = cnt_in_ref[...]

    @pl.when(j == 0)
    def _():
        conv_scr[pad:SUBLANES, :] = conv0_ref[0]
        lru_scr[...] = lru0_ref[0]
        for hd in range(HG_HEADS):
            st_scr[hd] = hg0_ref[0, hd].T

    x = x_ref[0]
    mod = mod_ref[0]
    sh1, sc1, g1, sh2, sc2 = (mod[i:i + 1, :] for i in range(5))

    h = _rms_rows(x) * (1.0 + sc1) + sh1
    h_bf = _bf(h)
    lw, hw = LRU_WIDTH, HG_WIDTH
    proj = lambda lo, width: _dot(h_bf, w_in_ref[:, lo:lo + width])
    xb = proj(0, lw)
    gb = proj(lw, lw)

    conv_scr[SUBLANES:SUBLANES + tt, :] = xb
    xc = conv_b_ref[...]
    for k in range(CONV_W):
        xc = xc + conv_w_ref[k:k + 1, :] * conv_scr[pl.ds(pad + k, tt), :]
    new_hist = conv_scr[pl.ds(tt + pad, CONV_W - 1), :]
    conv_scr[pad:SUBLANES, :] = new_hist
    qh = proj(2 * lw, hw)

    xc_bf = _bf(xc)
    half = lw // 2
    r_pre = jnp.concatenate([_dot(xc_bf[:, i * half:(i + 1) * half], wa_ref[i]) for i in range(2)], axis=1)
    i_pre = jnp.concatenate([_dot(xc_bf[:, i * half:(i + 1) * half], wx_ref[i]) for i in range(2)], axis=1)
    r = _sigmoid(r_pre + ba_ref[...])
    ig = _sigmoid(i_pre + bx_ref[...])
    fh = proj(2 * lw + hw, hw)
    log_a = (-LRU_C) * r * _softplus(-lam_ref[...])
    a = jnp.exp(log_a)
    th = jnp.tanh(log_a)
    one_minus_a2 = (-2.0) * th / (1.0 - th)
    bt = jnp.sqrt(one_minus_a2) * ig * xc
    ih = proj(2 * lw + 2 * hw, hw)

    carry = lru_scr[...]
    hs = []
    for g in range(tt // SUBLANES):
        sl = slice(g * SUBLANES, (g + 1) * SUBLANES)
        acum, hloc = _scan8(a[sl], bt[sl])
        hg = acum * carry + hloc
        hs.append(hg)
        carry = hg[SUBLANES - 1:SUBLANES, :]
    lru_scr[...] = carry
    gh = proj(2 * lw + 3 * hw, hw)
    hl = jnp.concatenate(hs, axis=0)
    y_lru = hl * _gelu_tanh(gb)
    mix_lru = _dot(_bf(y_lru), w_out_ref[:lw, :])

    lbs = lbs_ref[...]
    q = _silu(qh) * (HG_DK ** -0.5)
    f = lbs + (1.0 - lbs) * _sigmoid(fh)
    kk = 1.0 - f
    glog = jnp.log(f)

    ti = lax.broadcasted_iota(jnp.int32, (chunk, chunk), 0)
    si = lax.broadcasted_iota(jnp.int32, (chunk, chunk), 1)
    sub_shift = HG_SUB.bit_length() - 1
    mask_diag = ((ti >> sub_shift) == (si >> sub_shift)) & (ti >= si)
    levels = []
    m = HG_SUB
    while m < chunk:
        sh = m.bit_length() - 1
        levels.append((m, ((ti >> (sh + 1)) == (si >> (sh + 1))) & (((ti >> sh) & 1) == 1) & (((si >> sh) & 1) == 0)))
        m *= 2

    o_chunks = []
    for c in range(tt // chunk):
        rs = slice(c * chunk, (c + 1) * chunk)
        gc = glog[rs]
        bs, bcarry = [], None
        for g in range(chunk // SUBLANES):
            cs = _cumsum8(gc[g * SUBLANES:(g + 1) * SUBLANES])
            if bcarry is not None:
                cs = cs + bcarry
            bs.append(cs)
            bcarry = cs[SUBLANES - 1:SUBLANES, :]
        b = jnp.concatenate(bs, axis=0)
        b_last = bcarry
        qc, kc, vc = q[rs], kk[rs], ih[rs]

        st16, en16 = _block_refs(b, HG_SUB)
        q_lv = {HG_SUB: _bf(qc * jnp.exp(b - st16))}
        k_lv = {HG_SUB: _bf(kc * jnp.exp(en16 - b))}
        k_diag = _bf(kc * jnp.exp(jnp.minimum(st16 - b, EXP_CLAMP)))
        for m, _ in levels:
            if m == HG_SUB:
                continue
            st, en = _block_refs(b, m)
            q_lv[m] = _bf(qc * jnp.exp(b - st))
            k_lv[m] = _bf(kc * jnp.exp(en - b))
        q_all = _bf(qc * jnp.exp(b))
        k_all = _bf(kc * jnp.exp(b_last - b))
        v_bf = _bf(vc)
        eb_last = jnp.exp(b_last)

        heads = [slice(hd * HG_DK, (hd + 1) * HG_DK) for hd in range(HG_HEADS)]
        nt_dot = lambda lhs, rhs: lax.dot_general(lhs, rhs, _NT, preferred_element_type=jnp.float32)
        a_diag = [nt_dot(q_lv[HG_SUB][:, hs_], k_diag[:, hs_]) for hs_ in heads]
        a_lvl = [[nt_dot(q_lv[m][:, hs_], k_lv[m][:, hs_]) for m, _ in levels] for hs_ in heads]
        st_old = [st_scr[hd] for hd in range(HG_HEADS)]
        o_inter = [nt_dot(q_all[:, hs_], _bf(st_t)) for hs_, st_t in zip(heads, st_old)]
        kv = [lax.dot_general(v_bf[:, hs_], k_all[:, hs_], _TN, preferred_element_type=jnp.float32)
              for hs_ in heads]
        amats = []
        for hd in range(HG_HEADS):
            amat = jnp.where(mask_diag, a_diag[hd], 0.0)
            for (m, msk), a_m in zip(levels, a_lvl[hd]):
                amat = amat + jnp.where(msk, a_m, 0.0)
            amats.append(_bf(amat))
        o_intra = [_dot(amat, v_bf[:, hs_]) for amat, hs_ in zip(amats, heads)]
        o_heads = []
        for hd, hs_ in enumerate(heads):
            st_scr[hd] = st_old[hd] * eb_last[:, hs_] + kv[hd]
            o_heads.append(_rms_rows(o_inter[hd] + o_intra[hd]) * hgn_ref[...])
        o_chunks.append(jnp.concatenate(o_heads, axis=1))
    o_all = o_chunks[0] if len(o_chunks) == 1 else jnp.concatenate(o_chunks, axis=0)
    y_hg = o_all * _silu(gh)

    mix = mix_lru + _dot(_bf(y_hg), w_out_ref[lw:, :])
    x_mid = x + g1 * mix
    xmid_ref[0] = x_mid

    h2 = _rms_rows(x_mid) * (1.0 + sc2) + sh2
    h2_hi = _bf(h2)
    h2_lo = _bf(h2 - h2_hi.astype(jnp.float32))
    h2_ref[0] = h2_hi
    hi_terms = _dot(h2_hi, wr_hl_ref[...])
    logits = (hi_terms[:, :ROUTE_LANES] + hi_terms[:, ROUTE_LANES:] + _dot(h2_lo, wr_hi_ref[...])
              + br_ref[...])

    lane = lax.broadcasted_iota(jnp.int32, (tt, ROUTE_LANES), 1)
    lane_f = lane.astype(jnp.float32)
    neg, big = -1e30, 1e6
    is_grp = lane < N_GROUPS
    gl = jnp.where(is_grp, logits, neg)
    gmax = jnp.max(gl, axis=-1, keepdims=True)
    gsel = jnp.min(jnp.where(gl == gmax, lane_f, big), axis=-1, keepdims=True)
    gsum = jnp.sum(jnp.where(is_grp, jnp.exp(gl - gmax), 0.0), axis=-1, keepdims=True)
    p_grp = 1.0 / gsum
    lo = N_GROUPS + gsel * EXPERTS_PER_GROUP
    emask = (lane_f >= lo) & (lane_f < lo + EXPERTS_PER_GROUP)
    el = jnp.where(emask, logits, neg)
    tv1 = jnp.max(el, axis=-1, keepdims=True)
    ti1 = jnp.min(jnp.where(emask & (el == tv1), lane_f, big), axis=-1, keepdims=True)
    emask2 = emask & (lane_f != ti1)
    el2 = jnp.where(emask2, logits, neg)
    tv2 = jnp.max(el2, axis=-1, keepdims=True)
    ti2 = jnp.min(jnp.where(emask2 & (el2 == tv2), lane_f, big), axis=-1, keepdims=True)
    e21 = jnp.exp(tv2 - tv1)
    w1 = p_grp / (1.0 + e21)
    w2 = w1 * e21
    hot1 = lane_f == ti1
    hot2 = lane_f == ti2
    hot12 = jnp.where(hot1 | hot2, 1.0, 0.0)
    before = _dot(tri_ref[...], _bf(hot12))
    c_tile = before[tt - 1:tt, :] + hot12[tt - 1:tt, :]
    c_pad = jnp.floor((c_tile + (GRANULE - 1.0)) * (1.0 / GRANULE)) * GRANULE
    lstart = _dot(_bf(jnp.broadcast_to(c_pad, (SUBLANES, ROUTE_LANES))), upper_ref[...])[0:1, :]
    pos = before + lstart
    slot1 = jnp.sum(jnp.where(hot1, pos, 0.0), axis=-1, keepdims=True)
    slot2 = jnp.sum(jnp.where(hot2, pos, 0.0), axis=-1, keepdims=True)
    gbase = cnt_scr[...]
    cnt_scr[...] = gbase + c_pad

    per_expert = jnp.concatenate([lstart, c_pad, gbase, jnp.zeros((LANES - 3, ROUTE_LANES), jnp.float32)], axis=0).T
    lstart_c, c_pad_c, gbase_c = per_expert[:, 0:1], per_expert[:, 1:2], per_expert[:, 2:3]
    erow = lax.broadcasted_iota(jnp.int32, (LANES, ROUTE_LANES), 0)
    is_expert = (erow >= N_GROUPS) & (erow < N_GROUPS + N_EXPERTS)
    ended = is_expert & ((lstart_c + c_pad_c) * (1.0 / GRANULE) <= lane_f[0:1, :])
    owner = jnp.sum(jnp.where(ended, 1.0, 0.0), axis=0, keepdims=True)
    mine = (erow - N_GROUPS).astype(jnp.float32) == owner
    dst_row = jnp.sum(jnp.where(mine, gbase_c - lstart_c, 0.0), axis=0, keepdims=True) + GRANULE * lane_f[0:1, :]
    n_gran = jnp.broadcast_to(jnp.sum(c_pad, axis=-1, keepdims=True) * (1.0 / GRANULE), (1, ROUTE_LANES))
    tab_ref[0, 0] = jnp.concatenate(
        [owner, dst_row, n_gran, jnp.zeros((SUBLANES - 3, ROUTE_LANES), jnp.float32)], axis=0).astype(jnp.int32)
    route_ref[0] = jnp.where(lane == 0, w1, jnp.where(lane == 1, w2, jnp.where(lane == 2, slot1,
                                                                                jnp.where(lane == 3, slot2, 0.0))))

    @pl.when(j == last_j)
    def _():
        conv_out_ref[0] = new_hist
        lru_out_ref[0] = carry
        for hd in range(HG_HEADS):
            hg_out_ref[0, hd] = st_scr[hd].T

    @pl.when((bi == last_b) & (j == last_j))
    def _():
        cnt_out_ref[...] = cnt_scr[...]


def _mixer_call(x, mod, conv0, lru0, hg0, wts, cnt_in, *, tt, chunk):
    bsz, t, d = x.shape
    grid = (bsz, t // tt)
    tile = lambda last: pl.BlockSpec((1, tt, last), lambda b, j: (b, j, 0))
    per_b = lambda *shape: pl.BlockSpec((1,) + shape, lambda b, j: (b,) + (0,) * len(shape))
    full = lambda a: pl.BlockSpec(a.shape, lambda b, j: (0,) * a.ndim)
    tri = _bf(jnp.tril(jnp.ones((tt, tt), jnp.float32), -1))
    upper = _bf(jnp.triu(jnp.ones((ROUTE_LANES, ROUTE_LANES), jnp.float32), 1))
    per_tile = lambda *shape: pl.BlockSpec((1, 1) + shape, lambda b, j: (b, j) + (0,) * len(shape))

    args, in_specs = [x], [tile(d)]
    args += [mod, conv0, lru0.reshape(bsz, 1, LRU_WIDTH), hg0]
    in_specs += [per_b(6, d), per_b(CONV_W - 1, LRU_WIDTH), per_b(1, LRU_WIDTH), per_b(HG_HEADS, HG_DK, HG_DK)]
    args += list(wts) + [tri, upper, cnt_in]
    in_specs += [full(w) for w in wts] + [full(tri), full(upper), full(cnt_in)]

    out_shape = (
        jax.ShapeDtypeStruct((bsz, t, d), jnp.float32),
        jax.ShapeDtypeStruct((bsz, t, d), jnp.bfloat16),
        jax.ShapeDtypeStruct((bsz, t, ROUTE_LANES), jnp.float32),
        jax.ShapeDtypeStruct((bsz, t // tt, SUBLANES, ROUTE_LANES), jnp.int32),
        jax.ShapeDtypeStruct((1, ROUTE_LANES), jnp.float32),
        jax.ShapeDtypeStruct((bsz, CONV_W - 1, LRU_WIDTH), jnp.float32),
        jax.ShapeDtypeStruct((bsz, 1, LRU_WIDTH), jnp.float32),
        jax.ShapeDtypeStruct((bsz, HG_HEADS, HG_DK, HG_DK), jnp.float32),
    )
    out_specs = (tile(d), tile(d), tile(ROUTE_LANES), per_tile(SUBLANES, ROUTE_LANES),
                 pl.BlockSpec((1, ROUTE_LANES), lambda b, j: (0, 0)),
                 per_b(CONV_W - 1, LRU_WIDTH), per_b(1, LRU_WIDTH), per_b(HG_HEADS, HG_DK, HG_DK))
    scratch = [
        pltpu.VMEM((tt + SUBLANES, LRU_WIDTH), jnp.float32),
        pltpu.VMEM((1, LRU_WIDTH), jnp.float32),
        pltpu.VMEM((HG_HEADS, HG_DK, HG_DK), jnp.float32),
        pltpu.VMEM((1, ROUTE_LANES), jnp.float32),
    ]
    return pl.pallas_call(
        functools.partial(_mixer_kernel, tt=tt, chunk=chunk),
        grid=grid, in_specs=in_specs, out_specs=out_specs, out_shape=out_shape, scratch_shapes=scratch,
        compiler_params=pltpu.CompilerParams(dimension_semantics=("arbitrary", "arbitrary"),
                                             vmem_limit_bytes=VMEM_LIMIT),
        name="mixer",
    )(*args)


def _mod_kernel(c_ref, w_ref, b_ref, o_ref):
    s = _silu(c_ref[...])
    o_ref[0] = jnp.dot(s, w_ref[0], preferred_element_type=jnp.float32,
                       precision=lax.Precision.HIGHEST) + b_ref[0]


def _mod_call(c_all, w_ada, b_ada):
    depth, d, n = w_ada.shape
    rows = c_all.shape[0]
    tn = d
    return pl.pallas_call(
        _mod_kernel,
        grid=(depth, n // tn),
        in_specs=[pl.BlockSpec((rows, d), lambda l, i: (0, 0)),
                  pl.BlockSpec((1, d, tn), lambda l, i: (l, 0, i)),
                  pl.BlockSpec((1, 1, tn), lambda l, i: (l, 0, i))],
        out_specs=pl.BlockSpec((1, rows, tn), lambda l, i: (l, 0, i)),
        out_shape=jax.ShapeDtypeStruct((depth, rows, n), jnp.float32),
        compiler_params=pltpu.CompilerParams(dimension_semantics=("arbitrary", "arbitrary"),
                                             vmem_limit_bytes=VMEM_LIMIT),
        name="modulation",
    )(c_all, w_ada, b_ada.reshape(depth, 1, n))


def _for_each_granule(tab_ref, fn):
    n_gran = tab_ref[0, 2, 0]

    def per_granule(g, c):
        fn(pl.multiple_of(g * GRANULE, GRANULE), pl.multiple_of(tab_ref[0, 1, g], GRANULE))
        return c

    lax.fori_loop(0, n_gran, per_granule, 0)
    return n_gran


def _wait_granules(n, n_max, wait_fn):
    bit = 1 << (n_max.bit_length() - 1)
    while bit:
        pl.when((n & bit) != 0)(functools.partial(wait_fn, bit))
        bit >>= 1


def _dispatch_kernel(tab_ref, h_ref, route_ref, hs_in_ref, hs_ref, srt, sem, pend):
    del hs_in_ref
    i = pl.program_id(0)
    cur = lax.rem(i, 2)
    tt, n_slots = h_ref.shape[0], srt.shape[1]

    def granule_copy(buf, src, dst, rows=GRANULE):
        return pltpu.make_async_copy(srt.at[buf, pl.ds(src, rows), :],
                                     hs_ref.at[pl.ds(dst, rows), :], sem.at[buf])

    def drain(buf, n):
        _wait_granules(n, n_slots // GRANULE, lambda count: granule_copy(buf, 0, 0, count * GRANULE).wait())

    @pl.when(i == 0)
    def _():
        pend[0] = 0

    rec = route_ref[...]
    if tt < LANES:
        rec = jnp.concatenate([rec, jnp.zeros((LANES - tt, ROUTE_LANES), jnp.float32)], axis=0)
    rec_t = rec.T
    s1_row, s2_row = rec_t[2:3, :tt], rec_t[3:4, :tt]
    slot_iota = lax.broadcasted_iota(jnp.int32, (n_slots, tt), 0).astype(jnp.float32)
    perm = _bf(jnp.where((slot_iota == s1_row) | (slot_iota == s2_row), 1.0, 0.0))
    srt[cur] = _bf(_dot(perm, h_ref[...]))

    earlier = pend[0]
    pend[0] = _for_each_granule(tab_ref, lambda s, d: granule_copy(cur, s, d).start())
    drain(1 - cur, earlier)

    @pl.when(i == pl.num_programs(0) - 1)
    def _():
        drain(cur, pend[0])


def _dispatch_call(tab, h2, route, hs, *, n_slots):
    n = tab.shape[0]
    tt = h2.shape[0] // n
    d = h2.shape[1]
    return pl.pallas_call(
        _dispatch_kernel,
        grid=(n,),
        in_specs=[pl.BlockSpec((1, SUBLANES, ROUTE_LANES), lambda i: (i, 0, 0), memory_space=pltpu.SMEM),
                  pl.BlockSpec((tt, d), lambda i: (i, 0)),
                  pl.BlockSpec((tt, ROUTE_LANES), lambda i: (i, 0)),
                  pl.BlockSpec(memory_space=pl.ANY)],
        out_specs=pl.BlockSpec(memory_space=pl.ANY),
        scratch_shapes=[pltpu.VMEM((2, n_slots, d), jnp.bfloat16), pltpu.SemaphoreType.DMA((2,)),
                        pltpu.SMEM((1,), jnp.int32)],
        out_shape=jax.ShapeDtypeStruct(hs.shape, hs.dtype),
        input_output_aliases={3: 0},
        compiler_params=pltpu.CompilerParams(dimension_semantics=("arbitrary",), vmem_limit_bytes=VMEM_LIMIT),
        name="dispatch",
    )(tab, h2, route, hs)


def _expert_kernel(te_ref, nt_ref, hs_ref, wg_ref, wu_ref, wd_ref, ys_ref, wgu_bf, wd_bf):
    i = pl.program_id(0)
    active = i < nt_ref[0]

    @pl.when(active & ((i == 0) | (te_ref[i] != te_ref[jnp.maximum(i - 1, 0)])))
    def _():
        wgu_bf[:, :EXPERT_HIDDEN] = _bf(wg_ref[0, 0])
        wgu_bf[:, EXPERT_HIDDEN:] = _bf(wu_ref[0, 0])
        wd_bf[...] = _bf(wd_ref[0, 0])

    @pl.when(active)
    def _():
        gu = _dot(hs_ref[...], wgu_bf[...])
        hid = _silu(gu[:, :EXPERT_HIDDEN]) * gu[:, EXPERT_HIDDEN:]
        ys_ref[...] = _bf(_dot(_bf(hid), wd_bf[...]))


def _expert_call(tile_expert, n_tiles, hs, w_gate, w_up, w_down, *, layer, te_rows):
    npad, d = hs.shape
    wspec = lambda a: pl.BlockSpec((1, 1) + a.shape[2:], lambda i, te, nt: (layer, te[i], 0, 0))
    rows = pl.BlockSpec((te_rows, d), lambda i, te, nt: (jnp.minimum(i, nt[0] - 1), 0))
    grid_spec = pltpu.PrefetchScalarGridSpec(
        num_scalar_prefetch=2, grid=(npad // te_rows,),
        in_specs=[rows, wspec(w_gate), wspec(w_up), wspec(w_down)],
        out_specs=rows,
        scratch_shapes=[pltpu.VMEM((d, 2 * EXPERT_HIDDEN), jnp.bfloat16),
                        pltpu.VMEM((EXPERT_HIDDEN, d), jnp.bfloat16)])
    return pl.pallas_call(
        _expert_kernel,
        grid_spec=grid_spec,
        out_shape=jax.ShapeDtypeStruct((npad, d), jnp.bfloat16),
        input_output_aliases={2: 0},
        compiler_params=pltpu.CompilerParams(dimension_semantics=("arbitrary",), vmem_limit_bytes=VMEM_LIMIT),
        name="experts",
    )(tile_expert, n_tiles, hs, w_gate, w_up, w_down)


def _combine_kernel(tab_ref, tab_next_ref, route_ref, x_ref, mod_ref, fw_ref, ys_ref, o_ref, yloc, sem, pend, *,
                    final):
    i = pl.program_id(0)
    last = pl.num_programs(0) - 1
    cur = lax.rem(i, 2)

    def granule_copy(buf, loc, srt, rows=GRANULE):
        return pltpu.make_async_copy(ys_ref.at[pl.ds(srt, rows), :],
                                     yloc.at[buf, pl.ds(loc, rows), :], sem.at[buf])

    def gather(tab, buf):
        return _for_each_granule(tab, lambda s, d: granule_copy(buf, s, d).start())

    @pl.when(i == 0)
    def _():
        yloc[...] = jnp.zeros_like(yloc)
        pend[0] = gather(tab_ref, 0)

    n_cur = pend[0]

    @pl.when(i < last)
    def _():
        pend[0] = gather(tab_next_ref, 1 - cur)

    _wait_granules(n_cur, yloc.shape[1] // GRANULE, lambda count: granule_copy(cur, 0, 0, count * GRANULE).wait())

    rt = route_ref[...]
    tt, n_slots = rt.shape[0], yloc.shape[1]
    slot = lax.broadcasted_iota(jnp.int32, (tt, n_slots), 1).astype(jnp.float32)
    wc = jnp.where(slot == rt[:, 2:3], rt[:, 0:1], 0.0) + jnp.where(slot == rt[:, 3:4], rt[:, 1:2], 0.0)
    x_new = x_ref[...] + mod_ref[0, 5:6, :] * _dot(_bf(wc), yloc[cur])
    o_ref[...] = _rms_rows(x_new) * fw_ref[...] if final else x_new


def _combine_call(tab, route, x_mid, mod, fw, ys, *, n_slots, final):
    n = tab.shape[0]
    tt = route.shape[0] // n
    tiles_per_seq = n // mod.shape[0]
    d = ys.shape[-1]
    tab_spec = lambda imap: pl.BlockSpec((1, SUBLANES, ROUTE_LANES), imap, memory_space=pltpu.SMEM)
    return pl.pallas_call(
        functools.partial(_combine_kernel, final=final),
        grid=(n,),
        in_specs=[tab_spec(lambda i: (i, 0, 0)),
                  tab_spec(lambda i: (jnp.minimum(i + 1, n - 1), 0, 0)),
                  pl.BlockSpec((tt, ROUTE_LANES), lambda i: (i, 0)),
                  pl.BlockSpec((tt, d), lambda i: (i, 0)),
                  pl.BlockSpec((1,) + mod.shape[1:], lambda i: (i // tiles_per_seq, 0, 0)),
                  pl.BlockSpec((1, d), lambda i: (0, 0)),
                  pl.BlockSpec(memory_space=pl.ANY)],
        out_specs=pl.BlockSpec((tt, d), lambda i: (i, 0)),
        scratch_shapes=[pltpu.VMEM((2, n_slots, d), jnp.bfloat16), pltpu.SemaphoreType.DMA((2,)),
                        pltpu.SMEM((1,), jnp.int32)],
        out_shape=jax.ShapeDtypeStruct((n * tt, d), jnp.float32),
        compiler_params=pltpu.CompilerParams(dimension_semantics=("arbitrary",), vmem_limit_bytes=VMEM_LIMIT),
        name="combine",
    )(tab, tab, route, x_mid, mod, fw, ys)


def _block_diag_halves(w):
    nb, bd, _ = w.shape
    per = nb // 2
    eye = jnp.eye(per, dtype=w.dtype)
    halves = [jnp.einsum('nij,nm->nimj', w[h * per:(h + 1) * per], eye).reshape(per * bd, per * bd) for h in range(2)]
    return jnp.stack(halves)


def _tile_slots(tt):
    return -(-_tile_slots_used(tt) // LANES) * LANES


def _tile_slots_used(tt):
    return 2 * tt + (GRANULE - 1) * min(N_EXPERTS, 2 * tt)


def _tile_rows(t):
    for cand in (512, 256, 128, 64, 32, 16, 8):
        if t % cand == 0:
            return cand
    raise ValueError(f"sequence length {t} is not a multiple of 8")


def kernel(x_prompt, x_sample, state_conv, state_lru, state_hgrn, c_prompt, c_sample, w_ada, b_ada, w_in, conv_w,
           conv_b, lru_wa, lru_ba, lru_wx, lru_bx, lru_lambda, hg_lower, hg_norm_w, w_out, w_grp, b_grp, w_rt, b_rt,
           w_gate, w_up, w_down, final_norm_w):
    depth = w_in.shape[0]
    bp, tp, d = x_prompt.shape
    bs, ts, _ = x_sample.shape
    f32 = jnp.float32

    c_all = jnp.concatenate([c_prompt, c_sample], axis=0)
    rows = -(-c_all.shape[0] // SUBLANES) * SUBLANES
    c_all = jnp.pad(c_all, ((0, rows - c_all.shape[0]), (0, 0)))
    mod_all = _mod_call(c_all, w_ada, b_ada).reshape(depth, rows, 6, d)

    p_low = jax.nn.softmax(hg_lower.astype(f32), axis=0)
    lbs = jnp.cumsum(p_low, axis=0) - p_low[0]
    w_in_bf, w_out_bf = _bf(w_in), _bf(w_out)
    wr = jnp.concatenate([w_grp, w_rt], axis=-1)
    wr = jnp.pad(wr, ((0, 0), (0, 0), (0, ROUTE_LANES - wr.shape[-1])))
    wr_hi = _bf(wr)
    wr_hl = jnp.concatenate([wr_hi, _bf(wr - wr_hi.astype(f32))], axis=-1)
    br = jnp.concatenate([b_grp, b_rt], axis=-1)
    br = jnp.pad(br, ((0, 0), (0, ROUTE_LANES - br.shape[-1])))
    fw = final_norm_w.reshape(1, d)

    trunks = [
        dict(x=x_prompt, b0=0, nb=bp, t=tp,
             conv=jnp.zeros((depth, bp, CONV_W - 1, LRU_WIDTH), f32), lru=jnp.zeros((depth, bp, LRU_WIDTH), f32),
             hg=jnp.zeros((depth, bp, HG_HEADS, HG_DK, HG_DK), f32)),
        dict(x=x_sample, b0=bp, nb=bs, t=ts, conv=state_conv, lru=state_lru, hg=state_hgrn),
    ]
    for tr in trunks:
        tr["tt"] = _tile_rows(tr["t"])
        tr["chunk"] = min(64, tr["tt"])
        tr.update(convs=[], lrus=[], hgs=[])

    n_rows = sum(tr["nb"] * (tr["t"] // tr["tt"]) * _tile_slots_used(tr["tt"]) for tr in trunks)
    max_tiles = (n_rows + N_EXPERTS * (EXPERT_TILE - 1)) // EXPERT_TILE
    hs_zero = jnp.zeros((max_tiles * EXPERT_TILE, d), jnp.bfloat16)
    tile_ids = jnp.arange(max_tiles, dtype=jnp.int32)

    for l in range(depth):
        wts = (w_in_bf[l], conv_w[l], conv_b[l].reshape(1, -1), _bf(_block_diag_halves(lru_wa[l])),
               _bf(_block_diag_halves(lru_wx[l])), lru_ba[l].reshape(1, -1), lru_bx[l].reshape(1, -1),
               lru_lambda[l].reshape(1, -1), lbs[l].reshape(1, -1), hg_norm_w[l].reshape(1, -1), w_out_bf[l],
               wr_hi[l], wr_hl[l], br[l].reshape(1, -1))
        cnt = jnp.zeros((1, ROUTE_LANES), f32)
        for tr in trunks:
            mod = mod_all[l, tr["b0"]:tr["b0"] + tr["nb"]]
            tr["x"], tr["h2"], tr["route"], tr["tab"], cnt, conv_n, lru_n, hg_n = _mixer_call(
                tr["x"], mod, tr["conv"][l], tr["lru"][l], tr["hg"][l], wts, cnt, tt=tr["tt"], chunk=tr["chunk"])
            tr["mod"] = mod
            tr["convs"].append(conv_n)
            tr["lrus"].append(lru_n.reshape(tr["nb"], LRU_WIDTH))
            tr["hgs"].append(hg_n)

        counts = cnt[0, N_GROUPS:N_GROUPS + N_EXPERTS].astype(jnp.int32)
        tiles_e = (counts + (EXPERT_TILE - 1)) // EXPERT_TILE
        ends = jnp.cumsum(tiles_e)
        off = (ends - tiles_e) * EXPERT_TILE
        n_tiles = ends[-1:]
        tile_expert = jnp.sum(jnp.minimum(tile_ids, n_tiles - 1)[:, None] >= ends[None, :], axis=1).astype(jnp.int32)

        hs = hs_zero
        for tr in trunks:
            tab = tr["tab"].reshape((-1,) + tr["tab"].shape[2:])
            region = off[jnp.minimum(tab[:, 0, :], N_EXPERTS - 1)]
            tr["tab"] = tab.at[:, 1, :].add(region)
            tr["route"] = tr["route"].reshape(-1, ROUTE_LANES)
            hs = _dispatch_call(tr["tab"], tr["h2"].reshape(-1, d), tr["route"], hs, n_slots=_tile_slots(tr["tt"]))
        ys = _expert_call(tile_expert, n_tiles, hs, w_gate, w_up, w_down, layer=l, te_rows=EXPERT_TILE)
        for tr in trunks:
            x_new = _combine_call(tr["tab"], tr["route"], tr["x"].reshape(-1, d), tr["mod"], fw, ys,
                                  n_slots=_tile_slots(tr["tt"]), final=(l == depth - 1))
            tr["x"] = x_new.reshape(tr["nb"], tr["t"], d)

    (yp, cp, lp, hp), (ys_, cs, ls, hs_) = [
        (tr["x"], jnp.stack(tr["convs"]), jnp.stack(tr["lrus"]), jnp.stack(tr["hgs"])) for tr in trunks]
    return (yp, ys_, cp, lp, hp, cs, ls, hs_)
```

```python
import functools

import jax
import jax.numpy as jnp
from jax import lax
from jax.experimental import pallas as pl
from jax.experimental.pallas import tpu as pltpu

D_MODEL = 1024
LRU_WIDTH = 512
LRU_BLOCKS = 8
LRU_C = 8.0
CONV_W = 4
HG_WIDTH = 512
HG_HEADS = 4
HG_DK = 128
N_GROUPS = 4
EXPERTS_PER_GROUP = 8
N_EXPERTS = 32
EXPERT_HIDDEN = 256
EPS = 1e-6

SUBLANES = 8
GRANULE = 16
LANES = 128
HG_SUB = 16
EXP_CLAMP = 80.0
ROUTE_LANES = LANES
EXPERT_TILE = 512
VMEM_LIMIT = 56 * 1024 * 1024

_NT = (((1,), (1,)), ((), ()))
_TN = (((0,), (0,)), ((), ()))


def _bf(x):
    return x.astype(jnp.bfloat16)


def _dot(a, b):
    return jnp.dot(a, b, preferred_element_type=jnp.float32)


def _sigmoid(x):
    return 1.0 / (1.0 + jnp.exp(-x))


def _silu(x):
    return x * _sigmoid(x)


def _gelu_tanh(x):
    return 0.5 * x * (1.0 + jnp.tanh(0.7978845608028654 * (x + 0.044715 * (x * x * x))))


def _softplus(x):
    return jnp.maximum(x, 0.0) + jnp.log(1.0 + jnp.exp(-jnp.abs(x)))


def _rms_rows(x):
    return x * lax.rsqrt(jnp.mean(x * x, axis=-1, keepdims=True) + EPS)


def _scan8(a, b):
    rows = lax.broadcasted_iota(jnp.int32, a.shape, 0)
    for d in (1, 2, 4):
        m = rows >= d
        a_sh = pltpu.roll(a, d, 0)
        b_sh = pltpu.roll(b, d, 0)
        b = jnp.where(m, a * b_sh + b, b)
        a = jnp.where(m, a * a_sh, a)
    return a, b


def _cumsum8(x):
    rows = lax.broadcasted_iota(jnp.int32, x.shape, 0)
    for d in (1, 2, 4):
        x = x + jnp.where(rows >= d, pltpu.roll(x, d, 0), 0.0)
    return x


def _block_refs(b, m):
    n, w = b.shape
    starts, ends = [], []
    for i in range(n // m):
        if i == 0:
            starts.append(jnp.zeros((m, w), jnp.float32))
        else:
            starts.append(jnp.broadcast_to(b[i * m - 1:i * m, :], (m, w)))
        ends.append(jnp.broadcast_to(b[(i + 1) * m - 1:(i + 1) * m, :], (m, w)))
    if len(starts) == 1:
        return starts[0], ends[0]
    return jnp.concatenate(starts, axis=0), jnp.concatenate(ends, axis=0)


def _mixer_kernel(*refs, tt, chunk):
    it = iter(refs)
    x_ref = next(it)
    (mod_ref, conv0_ref, lru0_ref, hg0_ref, w_in_ref, conv_w_ref, conv_b_ref, wa_ref, wx_ref, ba_ref, bx_ref,
     lam_ref, lbs_ref, hgn_ref, w_out_ref, wr_hi_ref, wr_hl_ref, br_ref, tri_ref, upper_ref, cnt_in_ref) = (
        next(it) for _ in range(21))
    (xmid_ref, h2_ref, route_ref, tab_ref, cnt_out_ref, conv_out_ref, lru_out_ref,
     hg_out_ref) = (next(it) for _ in range(8))
    conv_scr, lru_scr, st_scr, cnt_scr = (next(it) for _ in range(4))

    bi = pl.program_id(0)
    j = pl.program_id(1)
    last_b = pl.num_programs(0) - 1
    last_j = pl.num_programs(1) - 1
    pad = SUBLANES - (CONV_W - 1)

    @pl.when((bi == 0) & (j == 0))
    def _():
        cnt_scr[...] = cnt_in_ref[...]

    @pl.when(j == 0)
    def _():
        conv_scr[pad:SUBLANES, :] = conv0_ref[0]
        lru_scr[...] = lru0_ref[0]
        for hd in range(HG_HEADS):
            st_scr[hd] = hg0_ref[0, hd].T

    x = x_ref[0]
    mod = mod_ref[0]
    sh1, sc1, g1, sh2, sc2 = (mod[i:i + 1, :] for i in range(5))

    h = _rms_rows(x) * (1.0 + sc1) + sh1
    h_bf = _bf(h)
    lw, hw = LRU_WIDTH, HG_WIDTH
    proj = lambda lo, width: _dot(h_bf, w_in_ref[:, lo:lo + width])
    xb = proj(0, lw)
    gb = proj(lw, lw)

    conv_scr[SUBLANES:SUBLANES + tt, :] = xb
    xc = conv_b_ref[...]
    for k in range(CONV_W):
        xc = xc + conv_w_ref[k:k + 1, :] * conv_scr[pl.ds(pad + k, tt), :]
    new_hist = conv_scr[pl.ds(tt + pad, CONV_W - 1), :]
    conv_scr[pad:SUBLANES, :] = new_hist
    qh = proj(2 * lw, hw)

    xc_bf = _bf(xc)
    half = lw // 2
    r_pre = jnp.concatenate([_dot(xc_bf[:, i * half:(i + 1) * half], wa_ref[i]) for i in range(2)], axis=1)
    i_pre = jnp.concatenate([_dot(xc_bf[:, i * half:(i + 1) * half], wx_ref[i]) for i in range(2)], axis=1)
    r = _sigmoid(r_pre + ba_ref[...])
    ig = _sigmoid(i_pre + bx_ref[...])
    fh = proj(2 * lw + hw, hw)
    log_a = (-LRU_C) * r * _softplus(-lam_ref[...])
    a = jnp.exp(log_a)
    th = jnp.tanh(log_a)
    one_minus_a2 = (-2.0) * th / (1.0 - th)
    bt = jnp.sqrt(one_minus_a2) * ig * xc
    ih = proj(2 * lw + 2 * hw, hw)

    carry = lru_scr[...]
    hs = []
    for g in range(tt // SUBLANES):
        sl = slice(g * SUBLANES, (g + 1) * SUBLANES)
        acum, hloc = _scan8(a[sl], bt[sl])
        hg = acum * carry + hloc
        hs.append(hg)
        carry = hg[SUBLANES - 1:SUBLANES, :]
    lru_scr[...] = carry
    gh = proj(2 * lw + 3 * hw, hw)
    hl = jnp.concatenate(hs, axis=0)
    y_lru = hl * _gelu_tanh(gb)
    mix_lru = _dot(_bf(y_lru), w_out_ref[:lw, :])

    lbs = lbs_ref[...]
    q = _silu(qh) * (HG_DK ** -0.5)
    f = lbs + (1.0 - lbs) * _sigmoid(fh)
    kk = 1.0 - f
    glog = jnp.log(f)

    ti = lax.broadcasted_iota(jnp.int32, (chunk, chunk), 0)
    si = lax.broadcasted_iota(jnp.int32, (chunk, chunk), 1)
    sub_shift = HG_SUB.bit_length() - 1
    mask_diag = ((ti >> sub_shift) == (si >> sub_shift)) & (ti >= si)
    levels = []
    m = HG_SUB
    while m < chunk:
        sh = m.bit_length() - 1
        levels.append((m, ((ti >> (sh + 1)) == (si >> (sh + 1))) & (((ti >> sh) & 1) == 1) & (((si >> sh) & 1) == 0)))
        m *= 2

    o_chunks = []
    for c in range(tt // chunk):
        rs = slice(c * chunk, (c + 1) * chunk)
        gc = glog[rs]
        bs, bcarry = [], None
        for g in range(chunk // SUBLANES):
            cs = _cumsum8(gc[g * SUBLANES:(g + 1) * SUBLANES])
            if bcarry is not None:
                cs = cs + bcarry
            bs.append(cs)
            bcarry = cs[SUBLANES - 1:SUBLANES, :]
        b = jnp.concatenate(bs, axis=0)
        b_last = bcarry
        qc, kc, vc = q[rs], kk[rs], ih[rs]

        st16, en16 = _block_refs(b, HG_SUB)
        q_lv = {HG_SUB: _bf(qc * jnp.exp(b - st16))}
        k_lv = {HG_SUB: _bf(kc * jnp.exp(en16 - b))}
        k_diag = _bf(kc * jnp.exp(jnp.minimum(st16 - b, EXP_CLAMP)))
        for m, _ in levels:
            if m == HG_SUB:
                continue
            st, en = _block_refs(b, m)
            q_lv[m] = _bf(qc * jnp.exp(b - st))
            k_lv[m] = _bf(kc * jnp.exp(en - b))
        q_all = _bf(qc * jnp.exp(b))
        k_all = _bf(kc * jnp.exp(b_last - b))
        v_bf = _bf(vc)
        eb_last = jnp.exp(b_last)

        heads = [slice(hd * HG_DK, (hd + 1) * HG_DK) for hd in range(HG_HEADS)]
        nt_dot = lambda lhs, rhs: lax.dot_general(lhs, rhs, _NT, preferred_element_type=jnp.float32)
        a_diag = [nt_dot(q_lv[HG_SUB][:, hs_], k_diag[:, hs_]) for hs_ in heads]
        a_lvl = [[nt_dot(q_lv[m][:, hs_], k_lv[m][:, hs_]) for m, _ in levels] for hs_ in heads]
        st_old = [st_scr[hd] for hd in range(HG_HEADS)]
        o_inter = [nt_dot(q_all[:, hs_], _bf(st_t)) for hs_, st_t in zip(heads, st_old)]
        kv = [lax.dot_general(v_bf[:, hs_], k_all[:, hs_], _TN, preferred_element_type=jnp.float32)
              for hs_ in heads]
        amats = []
        for hd in range(HG_HEADS):
            amat = jnp.where(mask_diag, a_diag[hd], 0.0)
            for (m, msk), a_m in zip(levels, a_lvl[hd]):
                amat = amat + jnp.where(msk, a_m, 0.0)
            amats.append(_bf(amat))
        o_intra = [_dot(amat, v_bf[:, hs_]) for amat, hs_ in zip(amats, heads)]
        o_heads = []
        for hd, hs_ in enumerate(heads):
            st_scr[hd] = st_old[hd] * eb_last[:, hs_] + kv[hd]
            o_heads.append(_rms_rows(o_inter[hd] + o_intra[hd]) * hgn_ref[...])
        o_chunks.append(jnp.concatenate(o_heads, axis=1))
    o_all = o_chunks[0] if len(o_chunks) == 1 else jnp.concatenate(o_chunks, axis=0)
    y_hg = o_all * _silu(gh)

    mix = mix_lru + _dot(_bf(y_hg), w_out_ref[lw:, :])
    x_mid = x + g1 * mix
    xmid_ref[0] = x_mid

    h2 = _rms_rows(x_mid) * (1.0 + sc2) + sh2
    h2_hi = _bf(h2)
    h2_lo = _bf(h2 - h2_hi.astype(jnp.float32))
    h2_ref[0] = h2_hi
    hi_terms = _dot(h2_hi, wr_hl_ref[...])
    logits = (hi_terms[:, :ROUTE_LANES] + hi_terms[:, ROUTE_LANES:] + _dot(h2_lo, wr_hi_ref[...])
              + br_ref[...])

    lane = lax.broadcasted_iota(jnp.int32, (tt, ROUTE_LANES), 1)
    lane_f = lane.astype(jnp.float32)
    neg, big = -1e30, 1e6
    is_grp = lane < N_GROUPS
    gl = jnp.where(is_grp, logits, neg)
    gmax = jnp.max(gl, axis=-1, keepdims=True)
    gsel = jnp.min(jnp.where(gl == gmax, lane_f, big), axis=-1, keepdims=True)
    gsum = jnp.sum(jnp.where(is_grp, jnp.exp(gl - gmax), 0.0), axis=-1, keepdims=True)
    p_grp = 1.0 / gsum
    lo = N_GROUPS + gsel * EXPERTS_PER_GROUP
    emask = (lane_f >= lo) & (lane_f < lo + EXPERTS_PER_GROUP)
    el = jnp.where(emask, logits, neg)
    tv1 = jnp.max(el, axis=-1, keepdims=True)
    ti1 = jnp.min(jnp.where(emask & (el == tv1), lane_f, big), axis=-1, keepdims=True)
    emask2 = emask & (lane_f != ti1)
    el2 = jnp.where(emask2, logits, neg)
    tv2 = jnp.max(el2, axis=-1, keepdims=True)
    ti2 = jnp.min(jnp.where(emask2 & (el2 == tv2), lane_f, big), axis=-1, keepdims=True)
    e21 = jnp.exp(tv2 - tv1)
    w1 = p_grp / (1.0 + e21)
    w2 = w1 * e21
    hot1 = lane_f == ti1
    hot2 = lane_f == ti2
    hot12 = jnp.where(hot1 | hot2, 1.0, 0.0)
    before = _dot(tri_ref[...], _bf(hot12))
    c_tile = before[tt - 1:tt, :] + hot12[tt - 1:tt, :]
    c_pad = jnp.floor((c_tile + (GRANULE - 1.0)) * (1.0 / GRANULE)) * GRANULE
    lstart = _dot(_bf(jnp.broadcast_to(c_pad, (SUBLANES, ROUTE_LANES))), upper_ref[...])[0:1, :]
    pos = before + lstart
    slot1 = jnp.sum(jnp.where(hot1, pos, 0.0), axis=-1, keepdims=True)
    slot2 = jnp.sum(jnp.where(hot2, pos, 0.0), axis=-1, keepdims=True)
    gbase = cnt_scr[...]
    cnt_scr[...] = gbase + c_pad

    per_expert = jnp.concatenate([lstart, c_pad, gbase, jnp.zeros((LANES - 3, ROUTE_LANES), jnp.float32)], axis=0).T
    lstart_c, c_pad_c, gbase_c = per_expert[:, 0:1], per_expert[:, 1:2], per_expert[:, 2:3]
    erow = lax.broadcasted_iota(jnp.int32, (LANES, ROUTE_LANES), 0)
    is_expert = (erow >= N_GROUPS) & (erow < N_GROUPS + N_EXPERTS)
    ended = is_expert & ((lstart_c + c_pad_c) * (1.0 / GRANULE) <= lane_f[0:1, :])
    owner = jnp.sum(jnp.where(ended, 1.0, 0.0), axis=0, keepdims=True)
    mine = (erow - N_GROUPS).astype(jnp.float32) == owner
    dst_row = jnp.sum(jnp.where(mine, gbase_c - lstart_c, 0.0), axis=0, keepdims=True) + GRANULE * lane_f[0:1, :]
    n_gran = jnp.broadcast_to(jnp.sum(c_pad, axis=-1, keepdims=True) * (1.0 / GRANULE), (1, ROUTE_LANES))
    tab_ref[0, 0] = jnp.concatenate(
        [owner, dst_row, n_gran, jnp.zeros((SUBLANES - 3, ROUTE_LANES), jnp.float32)], axis=0).astype(jnp.int32)
    route_ref[0] = jnp.where(lane == 0, w1, jnp.where(lane == 1, w2, jnp.where(lane == 2, slot1,
                                                                                jnp.where(lane == 3, slot2, 0.0))))

    @pl.when(j == last_j)
    def _():
        conv_out_ref[0] = new_hist
        lru_out_ref[0] = carry
        for hd in range(HG_HEADS):
            hg_out_ref[0, hd] = st_scr[hd].T

    @pl.when((bi == last_b) & (j == last_j))
    def _():
        cnt_out_ref[...] = cnt_scr[...]


def _mixer_call(x, mod, conv0, lru0, hg0, wts, cnt_in, *, tt, chunk):
    bsz, t, d = x.shape
    grid = (bsz, t // tt)
    tile = lambda last: pl.BlockSpec((1, tt, last), lambda b, j: (b, j, 0))
    per_b = lambda *shape: pl.BlockSpec((1,) + shape, lambda b, j: (b,) + (0,) * len(shape))
    full = lambda a: pl.BlockSpec(a.shape, lambda b, j: (0,) * a.ndim)
    tri = _bf(jnp.tril(jnp.ones((tt, tt), jnp.float32), -1))
    upper = _bf(jnp.triu(jnp.ones((ROUTE_LANES, ROUTE_LANES), jnp.float32), 1))
    per_tile = lambda *shape: pl.BlockSpec((1, 1) + shape, lambda b, j: (b, j) + (0,) * len(shape))

    args, in_specs = [x], [tile(d)]
    args += [mod, conv0, lru0.reshape(bsz, 1, LRU_WIDTH), hg0]
    in_specs += [per_b(6, d), per_b(CONV_W - 1, LRU_WIDTH), per_b(1, LRU_WIDTH), per_b(HG_HEADS, HG_DK, HG_DK)]
    args += list(wts) + [tri, upper, cnt_in]
    in_specs += [full(w) for w in wts] + [full(tri), full(upper), full(cnt_in)]

    out_shape = (
        jax.ShapeDtypeStruct((bsz, t, d), jnp.float32),
        jax.ShapeDtypeStruct((bsz, t, d), jnp.bfloat16),
        jax.ShapeDtypeStruct((bsz, t, ROUTE_LANES), jnp.float32),
        jax.ShapeDtypeStruct((bsz, t // tt, SUBLANES, ROUTE_LANES), jnp.int32),
        jax.ShapeDtypeStruct((1, ROUTE_LANES), jnp.float32),
        jax.ShapeDtypeStruct((bsz, CONV_W - 1, LRU_WIDTH), jnp.float32),
        jax.ShapeDtypeStruct((bsz, 1, LRU_WIDTH), jnp.float32),
        jax.ShapeDtypeStruct((bsz, HG_HEADS, HG_DK, HG_DK), jnp.float32),
    )
    out_specs = (tile(d), tile(d), tile(ROUTE_LANES), per_tile(SUBLANES, ROUTE_LANES),
                 pl.BlockSpec((1, ROUTE_LANES), lambda b, j: (0, 0)),
                 per_b(CONV_W - 1, LRU_WIDTH), per_b(1, LRU_WIDTH), per_b(HG_HEADS, HG_DK, HG_DK))
    scratch = [
        pltpu.VMEM((tt + SUBLANES, LRU_WIDTH), jnp.float32),
        pltpu.VMEM((1, LRU_WIDTH), jnp.float32),
        pltpu.VMEM((HG_HEADS, HG_DK, HG_DK), jnp.float32),
        pltpu.VMEM((1, ROUTE_LANES), jnp.float32),
    ]
    return pl.pallas_call(
        functools.partial(_mixer_kernel, tt=tt, chunk=chunk),
        grid=grid, in_specs=in_specs, out_specs=out_specs, out_shape=out_shape, scratch_shapes=scratch,
        compiler_params=pltpu.CompilerParams(dimension_semantics=("arbitrary", "arbitrary"),
                                             vmem_limit_bytes=VMEM_LIMIT),
        name="mixer",
    )(*args)


def _mod_kernel(c_ref, w_ref, b_ref, o_ref):
    s = _silu(c_ref[...])
    o_ref[0] = jnp.dot(s, w_ref[0], preferred_element_type=jnp.float32,
                       precision=lax.Precision.HIGHEST) + b_ref[0]


def _mod_call(c_all, w_ada, b_ada):
    depth, d, n = w_ada.shape
    rows = c_all.shape[0]
    tn = d
    return pl.pallas_call(
        _mod_kernel,
        grid=(depth, n // tn),
        in_specs=[pl.BlockSpec((rows, d), lambda l, i: (0, 0)),
                  pl.BlockSpec((1, d, tn), lambda l, i: (l, 0, i)),
                  pl.BlockSpec((1, 1, tn), lambda l, i: (l, 0, i))],
        out_specs=pl.BlockSpec((1, rows, tn), lambda l, i: (l, 0, i)),
        out_shape=jax.ShapeDtypeStruct((depth, rows, n), jnp.float32),
        compiler_params=pltpu.CompilerParams(dimension_semantics=("arbitrary", "arbitrary"),
                                             vmem_limit_bytes=VMEM_LIMIT),
        name="modulation",
    )(c_all, w_ada, b_ada.reshape(depth, 1, n))


def _for_each_granule(tab_ref, fn):
    n_gran = tab_ref[0, 2, 0]

    def per_granule(g, c):
        fn(pl.multiple_of(g * GRANULE, GRANULE), pl.multiple_of(tab_ref[0, 1, g], GRANULE))
        return c

    lax.fori_loop(0, n_gran, per_granule, 0)
    return n_gran


def _wait_granules(n, n_max, wait_fn):
    bit = 1 << (n_max.bit_length() - 1)
    while bit:
        pl.when((n & bit) != 0)(functools.partial(wait_fn, bit))
        bit >>= 1


def _dispatch_kernel(tab_ref, h_ref, route_ref, hs_in_ref, hs_ref, srt, sem, pend):
    del hs_in_ref
    i = pl.program_id(0)
    cur = lax.rem(i, 2)
    tt, n_slots = h_ref.shape[0], srt.shape[1]

    def granule_copy(buf, src, dst, rows=GRANULE):
        return pltpu.make_async_copy(srt.at[buf, pl.ds(src, rows), :],
                                     hs_ref.at[pl.ds(dst, rows), :], sem.at[buf])

    def drain(buf, n):
        _wait_granules(n, n_slots // GRANULE, lambda count: granule_copy(buf, 0, 0, count * GRANULE).wait())

    @pl.when(i == 0)
    def _():
        pend[0] = 0

    rec = route_ref[...]
    if tt < LANES:
        rec = jnp.concatenate([rec, jnp.zeros((LANES - tt, ROUTE_LANES), jnp.float32)], axis=0)
    rec_t = rec.T
    s1_row, s2_row = rec_t[2:3, :tt], rec_t[3:4, :tt]
    slot_iota = lax.broadcasted_iota(jnp.int32, (n_slots, tt), 0).astype(jnp.float32)
    perm = _bf(jnp.where((slot_iota == s1_row) | (slot_iota == s2_row), 1.0, 0.0))
    srt[cur] = _bf(_dot(perm, h_ref[...]))

    earlier = pend[0]
    pend[0] = _for_each_granule(tab_ref, lambda s, d: granule_copy(cur, s, d).start())
    drain(1 - cur, earlier)

    @pl.when(i == pl.num_programs(0) - 1)
    def _():
        drain(cur, pend[0])


def _dispatch_call(tab, h2, route, hs, *, n_slots):
    n = tab.shape[0]
    tt = h2.shape[0] // n
    d = h2.shape[1]
    return pl.pallas_call(
        _dispatch_kernel,
        grid=(n,),
        in_specs=[pl.BlockSpec((1, SUBLANES, ROUTE_LANES), lambda i: (i, 0, 0), memory_space=pltpu.SMEM),
                  pl.BlockSpec((tt, d), lambda i: (i, 0)),
                  pl.BlockSpec((tt, ROUTE_LANES), lambda i: (i, 0)),
                  pl.BlockSpec(memory_space=pl.ANY)],
        out_specs=pl.BlockSpec(memory_space=pl.ANY),
        scratch_shapes=[pltpu.VMEM((2, n_slots, d), jnp.bfloat16), pltpu.SemaphoreType.DMA((2,)),
                        pltpu.SMEM((1,), jnp.int32)],
        out_shape=jax.ShapeDtypeStruct(hs.shape, hs.dtype),
        input_output_aliases={3: 0},
        compiler_params=pltpu.CompilerParams(dimension_semantics=("arbitrary",), vmem_limit_bytes=VMEM_LIMIT),
        name="dispatch",
    )(tab, h2, route, hs)


def _expert_kernel(te_ref, nt_ref, hs_ref, wg_ref, wu_ref, wd_ref, ys_ref, wgu_bf, wd_bf):
    i = pl.program_id(0)
    active = i < nt_ref[0]

    @pl.when(active & ((i == 0) | (te_ref[i] != te_ref[jnp.maximum(i - 1, 0)])))
    def _():
        wgu_bf[:, :EXPERT_HIDDEN] = _bf(wg_ref[0, 0])
        wgu_bf[:, EXPERT_HIDDEN:] = _bf(wu_ref[0, 0])
        wd_bf[...] = _bf(wd_ref[0, 0])

    @pl.when(active)
    def _():
        gu = _dot(hs_ref[...], wgu_bf[...])
        hid = _silu(gu[:, :EXPERT_HIDDEN]) * gu[:, EXPERT_HIDDEN:]
        ys_ref[...] = _bf(_dot(_bf(hid), wd_bf[...]))


def _expert_call(tile_expert, n_tiles, hs, w_gate, w_up, w_down, *, layer, te_rows):
    npad, d = hs.shape
    wspec = lambda a: pl.BlockSpec((1, 1) + a.shape[2:], lambda i, te, nt: (layer, te[i], 0, 0))
    rows = pl.BlockSpec((te_rows, d), lambda i, te, nt: (jnp.minimum(i, nt[0] - 1), 0))
    grid_spec = pltpu.PrefetchScalarGridSpec(
        num_scalar_prefetch=2, grid=(npad // te_rows,),
        in_specs=[rows, wspec(w_gate), wspec(w_up), wspec(w_down)],
        out_specs=rows,
        scratch_shapes=[pltpu.VMEM((d, 2 * EXPERT_HIDDEN), jnp.bfloat16),
                        pltpu.VMEM((EXPERT_HIDDEN, d), jnp.bfloat16)])
    return pl.pallas_call(
        _expert_kernel,
        grid_spec=grid_spec,
        out_shape=jax.ShapeDtypeStruct((npad, d), jnp.bfloat16),
        input_output_aliases={2: 0},
        compiler_params=pltpu.CompilerParams(dimension_semantics=("arbitrary",), vmem_limit_bytes=VMEM_LIMIT),
        name="experts",
    )(tile_expert, n_tiles, hs, w_gate, w_up, w_down)


def _combine_kernel(tab_ref, tab_next_ref, route_ref, x_ref, mod_ref, fw_ref, ys_ref, o_ref, yloc, sem, pend, *,
                    final):
    i = pl.program_id(0)
    last = pl.num_programs(0) - 1
    cur = lax.rem(i, 2)

    def granule_copy(buf, loc, srt, rows=GRANULE):
        return pltpu.make_async_copy(ys_ref.at[pl.ds(srt, rows), :],
                                     yloc.at[buf, pl.ds(loc, rows), :], sem.at[buf])

    def gather(tab, buf):
        return _for_each_granule(tab, lambda s, d: granule_copy(buf, s, d).start())

    @pl.when(i == 0)
    def _():
        yloc[...] = jnp.zeros_like(yloc)
        pend[0] = gather(tab_ref, 0)

    n_cur = pend[0]

    @pl.when(i < last)
    def _():
        pend[0] = gather(tab_next_ref, 1 - cur)

    _wait_granules(n_cur, yloc.shape[1] // GRANULE, lambda count: granule_copy(cur, 0, 0, count * GRANULE).wait())

    rt = route_ref[...]
    tt, n_slots = rt.shape[0], yloc.shape[1]
    slot = lax.broadcasted_iota(jnp.int32, (tt, n_slots), 1).astype(jnp.float32)
    wc = jnp.where(slot == rt[:, 2:3], rt[:, 0:1], 0.0) + jnp.where(slot == rt[:, 3:4], rt[:, 1:2], 0.0)
    x_new = x_ref[...] + mod_ref[0, 5:6, :] * _dot(_bf(wc), yloc[cur])
    o_ref[...] = _rms_rows(x_new) * fw_ref[...] if final else x_new


def _combine_call(tab, route, x_mid, mod, fw, ys, *, n_slots, final):
    n = tab.shape[0]
    tt = route.shape[0] // n
    tiles_per_seq = n // mod.shape[0]
    d = ys.shape[-1]
    tab_spec = lambda imap: pl.BlockSpec((1, SUBLANES, ROUTE_LANES), imap, memory_space=pltpu.SMEM)
    return pl.pallas_call(
        functools.partial(_combine_kernel, final=final),
        grid=(n,),
        in_specs=[tab_spec(lambda i: (i, 0, 0)),
                  tab_spec(lambda i: (jnp.minimum(i + 1, n - 1), 0, 0)),
                  pl.BlockSpec((tt, ROUTE_LANES), lambda i: (i, 0)),
                  pl.BlockSpec((tt, d), lambda i: (i, 0)),
                  pl.BlockSpec((1,) + mod.shape[1:], lambda i: (i // tiles_per_seq, 0, 0)),
                  pl.BlockSpec((1, d), lambda i: (0, 0)),
                  pl.BlockSpec(memory_space=pl.ANY)],
        out_specs=pl.BlockSpec((tt, d), lambda i: (i, 0)),
        scratch_shapes=[pltpu.VMEM((2, n_slots, d), jnp.bfloat16), pltpu.SemaphoreType.DMA((2,)),
                        pltpu.SMEM((1,), jnp.int32)],
        out_shape=jax.ShapeDtypeStruct((n * tt, d), jnp.float32),
        compiler_params=pltpu.CompilerParams(dimension_semantics=("arbitrary",), vmem_limit_bytes=VMEM_LIMIT),
        name="combine",
    )(tab, tab, route, x_mid, mod, fw, ys)


def _block_diag_halves(w):
    nb, bd, _ = w.shape
    per = nb // 2
    eye = jnp.eye(per, dtype=w.dtype)
    halves = [jnp.einsum('nij,nm->nimj', w[h * per:(h + 1) * per], eye).reshape(per * bd, per * bd) for h in range(2)]
    return jnp.stack(halves)


def _tile_slots(tt):
    return -(-_tile_slots_used(tt) // LANES) * LANES


def _tile_slots_used(tt):
    return 2 * tt + (GRANULE - 1) * min(N_EXPERTS, 2 * tt)


def _tile_rows(t):
    for cand in (512, 256, 128, 64, 32, 16, 8):
        if t % cand == 0:
            return cand
    raise ValueError(f"sequence length {t} is not a multiple of 8")


def kernel(x_prompt, x_sample, state_conv, state_lru, state_hgrn, c_prompt, c_sample, w_ada, b_ada, w_in, conv_w,
           conv_b, lru_wa, lru_ba, lru_wx, lru_bx, lru_lambda, hg_lower, hg_norm_w, w_out, w_grp, b_grp, w_rt, b_rt,
           w_gate, w_up, w_down, final_norm_w):
    depth = w_in.shape[0]
    bp, tp, d = x_prompt.shape
    bs, ts, _ = x_sample.shape
    f32 = jnp.float32

    c_all = jnp.concatenate([c_prompt, c_sample], axis=0)
    rows = -(-c_all.shape[0] // SUBLANES) * SUBLANES
    c_all = jnp.pad(c_all, ((0, rows - c_all.shape[0]), (0, 0)))
    mod_all = _mod_call(c_all, w_ada, b_ada).reshape(depth, rows, 6, d)

    p_low = jax.nn.softmax(hg_lower.astype(f32), axis=0)
    lbs = jnp.cumsum(p_low, axis=0) - p_low[0]
    w_in_bf, w_out_bf = _bf(w_in), _bf(w_out)
    wr = jnp.concatenate([w_grp, w_rt], axis=-1)
    wr = jnp.pad(wr, ((0, 0), (0, 0), (0, ROUTE_LANES - wr.shape[-1])))
    wr_hi = _bf(wr)
    wr_hl = jnp.concatenate([wr_hi, _bf(wr - wr_hi.astype(f32))], axis=-1)
    br = jnp.concatenate([b_grp, b_rt], axis=-1)
    br = jnp.pad(br, ((0, 0), (0, ROUTE_LANES - br.shape[-1])))
    fw = final_norm_w.reshape(1, d)

    trunks = [
        dict(x=x_prompt, b0=0, nb=bp, t=tp,
             conv=jnp.zeros((depth, bp, CONV_W - 1, LRU_WIDTH), f32), lru=jnp.zeros((depth, bp, LRU_WIDTH), f32),
             hg=jnp.zeros((depth, bp, HG_HEADS, HG_DK, HG_DK), f32)),
        dict(x=x_sample, b0=bp, nb=bs, t=ts, conv=state_conv, lru=state_lru, hg=state_hgrn),
    ]
    for tr in trunks:
        tr["tt"] = _tile_rows(tr["t"])
        tr["chunk"] = min(64, tr["tt"])
        tr.update(convs=[], lrus=[], hgs=[])

    n_rows = sum(tr["nb"] * (tr["t"] // tr["tt"]) * _tile_slots_used(tr["tt"]) for tr in trunks)
    max_tiles = (n_rows + N_EXPERTS * (EXPERT_TILE - 1)) // EXPERT_TILE
    hs_zero = jnp.zeros((max_tiles * EXPERT_TILE, d), jnp.bfloat16)
    tile_ids = jnp.arange(max_tiles, dtype=jnp.int32)
    expert_ids = jnp.arange(N_EXPERTS, dtype=jnp.int32)

    for l in range(depth):
        wts = (w_in_bf[l], conv_w[l], conv_b[l].reshape(1, -1), _bf(_block_diag_halves(lru_wa[l])),
               _bf(_block_diag_halves(lru_wx[l])), lru_ba[l].reshape(1, -1), lru_bx[l].reshape(1, -1),
               lru_lambda[l].reshape(1, -1), lbs[l].reshape(1, -1), hg_norm_w[l].reshape(1, -1), w_out_bf[l],
               wr_hi[l], wr_hl[l], br[l].reshape(1, -1))
        cnt = jnp.zeros((1, ROUTE_LANES), f32)
        for tr in trunks:
            mod = mod_all[l, tr["b0"]:tr["b0"] + tr["nb"]]
            tr["x"], tr["h2"], tr["route"], tr["tab"], cnt, conv_n, lru_n, hg_n = _mixer_call(
                tr["x"], mod, tr["conv"][l], tr["lru"][l], tr["hg"][l], wts, cnt, tt=tr["tt"], chunk=tr["chunk"])
            tr["mod"] = mod
            tr["convs"].append(conv_n)
            tr["lrus"].append(lru_n.reshape(tr["nb"], LRU_WIDTH))
            tr["hgs"].append(hg_n)

        counts = cnt[0, N_GROUPS:N_GROUPS + N_EXPERTS].astype(jnp.int32)
        tiles_e = (counts + (EXPERT_TILE - 1)) // EXPERT_TILE
        ends = jnp.cumsum(tiles_e)
        off = (ends - tiles_e) * EXPERT_TILE
        n_tiles = ends[-1:]
        tile_expert = jnp.sum(jnp.minimum(tile_ids, n_tiles - 1)[:, None] >= ends[None, :], axis=1).astype(jnp.int32)

        hs = hs_zero
        for tr in trunks:
            tab = tr["tab"].reshape((-1,) + tr["tab"].shape[2:])
            region = jnp.sum(jnp.where(tab[:, 0, :, None] == expert_ids, off, 0), axis=-1)
            tr["tab"] = tab.at[:, 1, :].add(region)
            tr["route"] = tr["route"].reshape(-1, ROUTE_LANES)
            hs = _dispatch_call(tr["tab"], tr["h2"].reshape(-1, d), tr["route"], hs, n_slots=_tile_slots(tr["tt"]))
        ys = _expert_call(tile_expert, n_tiles, hs, w_gate, w_up, w_down, layer=l, te_rows=EXPERT_TILE)
        for tr in trunks:
            x_new = _combine_call(tr["tab"], tr["route"], tr["x"].reshape(-1, d), tr["mod"], fw, ys,
                                  n_slots=_tile_slots(tr["tt"]), final=(l == depth - 1))
            tr["x"] = x_new.reshape(tr["nb"], tr["t"], d)

    (yp, cp, lp, hp), (ys_, cs, ls, hs_) = [
        (tr["x"], jnp.stack(tr["convs"]), jnp.stack(tr["lrus"]), jnp.stack(tr["hgs"])) for tr in trunks]
    return (yp, ys_, cp, lp, hp, cs, ls, hs_)
```

```python
import functools

import jax
import jax.numpy as jnp
from jax import lax
from jax.experimental import pallas as pl
from jax.experimental.pallas import tpu as pltpu

LRU_WIDTH = 512
LRU_C = 8.0
CONV_W = 4
HG_WIDTH = 512
HG_HEADS = 4
HG_DK = 128
N_GROUPS = 4
EXPERTS_PER_GROUP = 8
N_EXPERTS = 32
EXPERT_HIDDEN = 256
EPS = 1e-6

SUBLANES = 8
GRANULE = 16
LANES = 128
HG_CHUNK = 128
HG_SUB = 16
EXP_CLAMP = 80.0
ROUTE_LANES = LANES
EXPERT_TILE = 512
VMEM_LIMIT = 56 * 1024 * 1024

_NT = (((1,), (1,)), ((), ()))
_TN = (((0,), (0,)), ((), ()))


def _bf(x):
    return x.astype(jnp.bfloat16)


def _dot(a, b):
    return jnp.dot(a, b, preferred_element_type=jnp.float32)


def _sigmoid(x):
    return 1.0 / (1.0 + jnp.exp(-x))


def _silu(x):
    return x * _sigmoid(x)


def _gelu_tanh(x):
    return 0.5 * x * (1.0 + jnp.tanh(0.7978845608028654 * (x + 0.044715 * (x * x * x))))


def _softplus(x):
    return jnp.maximum(x, 0.0) + jnp.log(1.0 + jnp.exp(-jnp.abs(x)))


def _rms_rows(x):
    return x * lax.rsqrt(jnp.mean(x * x, axis=-1, keepdims=True) + EPS)


def _scan8(a, b):
    rows = lax.broadcasted_iota(jnp.int32, a.shape, 0)
    for d in (1, 2, 4):
        m = rows >= d
        a_sh = pltpu.roll(a, d, 0)
        b_sh = pltpu.roll(b, d, 0)
        b = jnp.where(m, a * b_sh + b, b)
        a = jnp.where(m, a * a_sh, a)
    return a, b


def _cumsum8(x):
    rows = lax.broadcasted_iota(jnp.int32, x.shape, 0)
    for d in (1, 2, 4):
        x = x + jnp.where(rows >= d, pltpu.roll(x, d, 0), 0.0)
    return x


def _block_refs(b, m):
    n, w = b.shape
    starts, ends = [], []
    for i in range(n // m):
        if i == 0:
            starts.append(jnp.zeros((m, w), jnp.float32))
        else:
            starts.append(jnp.broadcast_to(b[i * m - 1:i * m, :], (m, w)))
        ends.append(jnp.broadcast_to(b[(i + 1) * m - 1:(i + 1) * m, :], (m, w)))
    if len(starts) == 1:
        return starts[0], ends[0]
    return jnp.concatenate(starts, axis=0), jnp.concatenate(ends, axis=0)


def _mixer_kernel(*refs, tt, chunk):
    it = iter(refs)
    x_ref = next(it)
    (mod_ref, conv0_ref, lru0_ref, hg0_ref, w_in_ref, conv_w_ref, conv_b_ref, wa_ref, wx_ref, ba_ref, bx_ref,
     lam_ref, lbs_ref, hgn_ref, w_out_ref, wr_hi_ref, wr_hl_ref, br_ref, tri_ref, upper_ref, cnt_in_ref) = (
        next(it) for _ in range(21))
    (xmid_ref, h2_ref, route_ref, tab_ref, cnt_out_ref, conv_out_ref, lru_out_ref,
     hg_out_ref) = (next(it) for _ in range(8))
    conv_scr, lru_scr, st_scr, cnt_scr = (next(it) for _ in range(4))

    bi = pl.program_id(0)
    j = pl.program_id(1)
    last_b = pl.num_programs(0) - 1
    last_j = pl.num_programs(1) - 1
    pad = SUBLANES - (CONV_W - 1)

    @pl.when((bi == 0) & (j == 0))
    def _():
        cnt_scr[...] = cnt_in_ref[...]

    @pl.when(j == 0)
    def _():
        conv_scr[pad:SUBLANES, :] = conv0_ref[0]
        lru_scr[...] = lru0_ref[0]
        for hd in range(HG_HEADS):
            st_scr[hd] = hg0_ref[0, hd].T

    x = x_ref[0]
    mod = mod_ref[0]
    sh1, sc1, g1, sh2, sc2 = (mod[i:i + 1, :] for i in range(5))

    h = _rms_rows(x) * (1.0 + sc1) + sh1
    h_bf = _bf(h)
    lw, hw = LRU_WIDTH, HG_WIDTH
    proj = lambda lo, width: _dot(h_bf, w_in_ref[:, lo:lo + width])
    xb = proj(0, lw)
    gb = proj(lw, lw)

    conv_scr[SUBLANES:SUBLANES + tt, :] = xb
    xc = conv_b_ref[...]
    for k in range(CONV_W):
        xc = xc + conv_w_ref[k:k + 1, :] * conv_scr[pl.ds(pad + k, tt), :]
    new_hist = conv_scr[pl.ds(tt + pad, CONV_W - 1), :]
    conv_scr[pad:SUBLANES, :] = new_hist
    qh = proj(2 * lw, hw)

    xc_bf = _bf(xc)
    half = lw // 2
    r_pre = jnp.concatenate([_dot(xc_bf[:, i * half:(i + 1) * half], wa_ref[i]) for i in range(2)], axis=1)
    i_pre = jnp.concatenate([_dot(xc_bf[:, i * half:(i + 1) * half], wx_ref[i]) for i in range(2)], axis=1)
    r = _sigmoid(r_pre + ba_ref[...])
    ig = _sigmoid(i_pre + bx_ref[...])
    fh = proj(2 * lw + hw, hw)
    log_a = (-LRU_C) * r * _softplus(-lam_ref[...])
    a = jnp.exp(log_a)
    th = jnp.tanh(log_a)
    one_minus_a2 = (-2.0) * th / (1.0 - th)
    bt = jnp.sqrt(one_minus_a2) * ig * xc
    ih = proj(2 * lw + 2 * hw, hw)

    carry = lru_scr[...]
    hs = []
    for g in range(tt // SUBLANES):
        sl = slice(g * SUBLANES, (g + 1) * SUBLANES)
        acum, hloc = _scan8(a[sl], bt[sl])
        hg = acum * carry + hloc
        hs.append(hg)
        carry = hg[SUBLANES - 1:SUBLANES, :]
    lru_scr[...] = carry
    gh = proj(2 * lw + 3 * hw, hw)
    hl = jnp.concatenate(hs, axis=0)
    y_lru = hl * _gelu_tanh(gb)
    mix_lru = _dot(_bf(y_lru), w_out_ref[:lw, :])

    lbs = lbs_ref[...]
    q = _silu(qh) * (HG_DK ** -0.5)
    f = lbs + (1.0 - lbs) * _sigmoid(fh)
    kk = 1.0 - f
    glog = jnp.log(f)

    ti = lax.broadcasted_iota(jnp.int32, (chunk, chunk), 0)
    si = lax.broadcasted_iota(jnp.int32, (chunk, chunk), 1)
    sub_shift = HG_SUB.bit_length() - 1
    mask_diag = ((ti >> sub_shift) == (si >> sub_shift)) & (ti >= si)
    levels = []
    m = HG_SUB
    while m < chunk:
        sh = m.bit_length() - 1
        levels.append((m, ((ti >> (sh + 1)) == (si >> (sh + 1))) & (((ti >> sh) & 1) == 1) & (((si >> sh) & 1) == 0)))
        m *= 2

    o_chunks = []
    for c in range(tt // chunk):
        rs = slice(c * chunk, (c + 1) * chunk)
        gc = glog[rs]
        bs, bcarry = [], None
        for g in range(chunk // SUBLANES):
            cs = _cumsum8(gc[g * SUBLANES:(g + 1) * SUBLANES])
            if bcarry is not None:
                cs = cs + bcarry
            bs.append(cs)
            bcarry = cs[SUBLANES - 1:SUBLANES, :]
        b = jnp.concatenate(bs, axis=0)
        b_last = bcarry
        qc, kc, vc = q[rs], kk[rs], ih[rs]

        st16, en16 = _block_refs(b, HG_SUB)
        q_lv = {HG_SUB: _bf(qc * jnp.exp(b - st16))}
        k_lv = {HG_SUB: _bf(kc * jnp.exp(en16 - b))}
        k_diag = _bf(kc * jnp.exp(jnp.minimum(st16 - b, EXP_CLAMP)))
        for m, _ in levels:
            if m == HG_SUB:
                continue
            st, en = _block_refs(b, m)
            q_lv[m] = _bf(qc * jnp.exp(b - st))
            k_lv[m] = _bf(kc * jnp.exp(en - b))
        q_all = _bf(qc * jnp.exp(b))
        k_all = _bf(kc * jnp.exp(b_last - b))
        v_bf = _bf(vc)
        eb_last = jnp.exp(b_last)

        heads = [slice(hd * HG_DK, (hd + 1) * HG_DK) for hd in range(HG_HEADS)]
        nt_dot = lambda lhs, rhs: lax.dot_general(lhs, rhs, _NT, preferred_element_type=jnp.float32)
        a_diag = [nt_dot(q_lv[HG_SUB][:, hs_], k_diag[:, hs_]) for hs_ in heads]
        a_lvl = [[nt_dot(q_lv[m][:, hs_], k_lv[m][:, hs_]) for m, _ in levels] for hs_ in heads]
        st_old = [st_scr[hd] for hd in range(HG_HEADS)]
        o_inter = [nt_dot(q_all[:, hs_], _bf(st_t)) for hs_, st_t in zip(heads, st_old)]
        kv = [lax.dot_general(v_bf[:, hs_], k_all[:, hs_], _TN, preferred_element_type=jnp.float32)
              for hs_ in heads]
        amats = []
        for hd in range(HG_HEADS):
            amat = jnp.where(mask_diag, a_diag[hd], 0.0)
            for (m, msk), a_m in zip(levels, a_lvl[hd]):
                amat = amat + jnp.where(msk, a_m, 0.0)
            amats.append(_bf(amat))
        o_intra = [_dot(amat, v_bf[:, hs_]) for amat, hs_ in zip(amats, heads)]
        o_heads = []
        for hd, hs_ in enumerate(heads):
            st_scr[hd] = st_old[hd] * eb_last[:, hs_] + kv[hd]
            o_heads.append(_rms_rows(o_inter[hd] + o_intra[hd]) * hgn_ref[...])
        o_chunks.append(jnp.concatenate(o_heads, axis=1))
    o_all = o_chunks[0] if len(o_chunks) == 1 else jnp.concatenate(o_chunks, axis=0)
    y_hg = o_all * _silu(gh)

    mix = mix_lru + _dot(_bf(y_hg), w_out_ref[lw:, :])
    x_mid = x + g1 * mix
    xmid_ref[0] = x_mid

    h2 = _rms_rows(x_mid) * (1.0 + sc2) + sh2
    h2_hi = _bf(h2)
    h2_lo = _bf(h2 - h2_hi.astype(jnp.float32))
    h2_ref[0] = h2_hi
    hi_terms = _dot(h2_hi, wr_hl_ref[...])
    logits = (hi_terms[:, :ROUTE_LANES] + hi_terms[:, ROUTE_LANES:] + _dot(h2_lo, wr_hi_ref[...])
              + br_ref[...])

    lane = lax.broadcasted_iota(jnp.int32, (tt, ROUTE_LANES), 1)
    lane_f = lane.astype(jnp.float32)
    neg, big = -1e30, 1e6
    is_grp = lane < N_GROUPS
    gl = jnp.where(is_grp, logits, neg)
    gmax = jnp.max(gl, axis=-1, keepdims=True)
    gsel = jnp.min(jnp.where(gl == gmax, lane_f, big), axis=-1, keepdims=True)
    gsum = jnp.sum(jnp.where(is_grp, jnp.exp(gl - gmax), 0.0), axis=-1, keepdims=True)
    p_grp = 1.0 / gsum
    lo = N_GROUPS + gsel * EXPERTS_PER_GROUP
    emask = (lane_f >= lo) & (lane_f < lo + EXPERTS_PER_GROUP)
    el = jnp.where(emask, logits, neg)
    tv1 = jnp.max(el, axis=-1, keepdims=True)
    ti1 = jnp.min(jnp.where(emask & (el == tv1), lane_f, big), axis=-1, keepdims=True)
    emask2 = emask & (lane_f != ti1)
    el2 = jnp.where(emask2, logits, neg)
    tv2 = jnp.max(el2, axis=-1, keepdims=True)
    ti2 = jnp.min(jnp.where(emask2 & (el2 == tv2), lane_f, big), axis=-1, keepdims=True)
    e21 = jnp.exp(tv2 - tv1)
    w1 = p_grp / (1.0 + e21)
    w2 = w1 * e21
    hot1 = lane_f == ti1
    hot2 = lane_f == ti2
    hot12 = jnp.where(hot1 | hot2, 1.0, 0.0)
    before = _dot(tri_ref[...], _bf(hot12))
    c_tile = before[tt - 1:tt, :] + hot12[tt - 1:tt, :]
    c_pad = jnp.floor((c_tile + (GRANULE - 1.0)) * (1.0 / GRANULE)) * GRANULE
    lstart = _dot(_bf(jnp.broadcast_to(c_pad, (SUBLANES, ROUTE_LANES))), upper_ref[...])[0:1, :]
    pos = before + lstart
    slot1 = jnp.sum(jnp.where(hot1, pos, 0.0), axis=-1, keepdims=True)
    slot2 = jnp.sum(jnp.where(hot2, pos, 0.0), axis=-1, keepdims=True)
    gbase = cnt_scr[...]
    cnt_scr[...] = gbase + c_pad

    per_expert = jnp.concatenate([lstart, c_pad, gbase, jnp.zeros((LANES - 3, ROUTE_LANES), jnp.float32)], axis=0).T
    lstart_c, c_pad_c, gbase_c = per_expert[:, 0:1], per_expert[:, 1:2], per_expert[:, 2:3]
    erow = lax.broadcasted_iota(jnp.int32, (LANES, ROUTE_LANES), 0)
    is_expert = (erow >= N_GROUPS) & (erow < N_GROUPS + N_EXPERTS)
    ended = is_expert & ((lstart_c + c_pad_c) * (1.0 / GRANULE) <= lane_f[0:1, :])
    owner = jnp.sum(jnp.where(ended, 1.0, 0.0), axis=0, keepdims=True)
    mine = (erow - N_GROUPS).astype(jnp.float32) == owner
    dst_row = jnp.sum(jnp.where(mine, gbase_c - lstart_c, 0.0), axis=0, keepdims=True) + GRANULE * lane_f[0:1, :]
    n_gran = jnp.broadcast_to(jnp.sum(c_pad, axis=-1, keepdims=True) * (1.0 / GRANULE), (1, ROUTE_LANES))
    tab_ref[0, 0] = jnp.concatenate(
        [owner, dst_row, n_gran, jnp.zeros((SUBLANES - 3, ROUTE_LANES), jnp.float32)], axis=0).astype(jnp.int32)
    route_ref[0] = jnp.where(lane == 0, w1, jnp.where(lane == 1, w2, jnp.where(lane == 2, slot1,
                                                                                jnp.where(lane == 3, slot2, 0.0))))

    @pl.when(j == last_j)
    def _():
        conv_out_ref[0] = new_hist
        lru_out_ref[0] = carry
        for hd in range(HG_HEADS):
            hg_out_ref[0, hd] = st_scr[hd].T

    @pl.when((bi == last_b) & (j == last_j))
    def _():
        cnt_out_ref[...] = cnt_scr[...]


def _mixer_call(x, mod, conv0, lru0, hg0, wts, cnt_in, *, tt, chunk):
    bsz, t, d = x.shape
    grid = (bsz, t // tt)
    tile = lambda last: pl.BlockSpec((1, tt, last), lambda b, j: (b, j, 0))
    per_b = lambda *shape: pl.BlockSpec((1,) + shape, lambda b, j: (b,) + (0,) * len(shape))
    full = lambda a: pl.BlockSpec(a.shape, lambda b, j: (0,) * a.ndim)
    tri = _bf(jnp.tril(jnp.ones((tt, tt), jnp.float32), -1))
    upper = _bf(jnp.triu(jnp.ones((ROUTE_LANES, ROUTE_LANES), jnp.float32), 1))
    per_tile = lambda *shape: pl.BlockSpec((1, 1) + shape, lambda b, j: (b, j) + (0,) * len(shape))

    args, in_specs = [x], [tile(d)]
    args += [mod, conv0, lru0.reshape(bsz, 1, LRU_WIDTH), hg0]
    in_specs += [per_b(6, d), per_b(CONV_W - 1, LRU_WIDTH), per_b(1, LRU_WIDTH), per_b(HG_HEADS, HG_DK, HG_DK)]
    args += list(wts) + [tri, upper, cnt_in]
    in_specs += [full(w) for w in wts] + [full(tri), full(upper), full(cnt_in)]

    out_shape = (
        jax.ShapeDtypeStruct((bsz, t, d), jnp.float32),
        jax.ShapeDtypeStruct((bsz, t, d), jnp.bfloat16),
        jax.ShapeDtypeStruct((bsz, t, ROUTE_LANES), jnp.float32),
        jax.ShapeDtypeStruct((bsz, t // tt, SUBLANES, ROUTE_LANES), jnp.int32),
        jax.ShapeDtypeStruct((1, ROUTE_LANES), jnp.float32),
        jax.ShapeDtypeStruct((bsz, CONV_W - 1, LRU_WIDTH), jnp.float32),
        jax.ShapeDtypeStruct((bsz, 1, LRU_WIDTH), jnp.float32),
        jax.ShapeDtypeStruct((bsz, HG_HEADS, HG_DK, HG_DK), jnp.float32),
    )
    out_specs = (tile(d), tile(d), tile(ROUTE_LANES), per_tile(SUBLANES, ROUTE_LANES),
                 pl.BlockSpec((1, ROUTE_LANES), lambda b, j: (0, 0)),
                 per_b(CONV_W - 1, LRU_WIDTH), per_b(1, LRU_WIDTH), per_b(HG_HEADS, HG_DK, HG_DK))
    scratch = [
        pltpu.VMEM((tt + SUBLANES, LRU_WIDTH), jnp.float32),
        pltpu.VMEM((1, LRU_WIDTH), jnp.float32),
        pltpu.VMEM((HG_HEADS, HG_DK, HG_DK), jnp.float32),
        pltpu.VMEM((1, ROUTE_LANES), jnp.float32),
    ]
    return pl.pallas_call(
        functools.partial(_mixer_kernel, tt=tt, chunk=chunk),
        grid=grid, in_specs=in_specs, out_specs=out_specs, out_shape=out_shape, scratch_shapes=scratch,
        compiler_params=pltpu.CompilerParams(dimension_semantics=("arbitrary", "arbitrary"),
                                             vmem_limit_bytes=VMEM_LIMIT),
        name="mixer",
    )(*args)


def _mod_kernel(c_ref, w_ref, b_ref, o_ref):
    s = _silu(c_ref[...])
    o_ref[0] = jnp.dot(s, w_ref[0], preferred_element_type=jnp.float32,
                       precision=lax.Precision.HIGHEST) + b_ref[0]


def _mod_call(c_all, w_ada, b_ada):
    depth, d, n = w_ada.shape
    rows = c_all.shape[0]
    tn = d
    return pl.pallas_call(
        _mod_kernel,
        grid=(depth, n // tn),
        in_specs=[pl.BlockSpec((rows, d), lambda l, i: (0, 0)),
                  pl.BlockSpec((1, d, tn), lambda l, i: (l, 0, i)),
                  pl.BlockSpec((1, 1, tn), lambda l, i: (l, 0, i))],
        out_specs=pl.BlockSpec((1, rows, tn), lambda l, i: (l, 0, i)),
        out_shape=jax.ShapeDtypeStruct((depth, rows, n), jnp.float32),
        compiler_params=pltpu.CompilerParams(dimension_semantics=("arbitrary", "arbitrary"),
                                             vmem_limit_bytes=VMEM_LIMIT),
        name="modulation",
    )(c_all, w_ada, b_ada.reshape(depth, 1, n))


def _for_each_granule(tab_ref, fn):
    n_gran = tab_ref[0, 2, 0]

    def per_granule(g, c):
        fn(pl.multiple_of(g * GRANULE, GRANULE), pl.multiple_of(tab_ref[0, 1, g], GRANULE))
        return c

    lax.fori_loop(0, n_gran, per_granule, 0)
    return n_gran


def _wait_granules(n, n_max, wait_fn):
    bit = 1 << (n_max.bit_length() - 1)
    while bit:
        pl.when((n & bit) != 0)(functools.partial(wait_fn, bit))
        bit >>= 1


def _dispatch_kernel(tab_ref, h_ref, route_ref, hs_in_ref, hs_ref, srt, sem, pend):
    del hs_in_ref
    i = pl.program_id(0)
    cur = lax.rem(i, 2)
    tt, n_slots = h_ref.shape[0], srt.shape[1]

    def granule_copy(buf, src, dst, rows=GRANULE):
        return pltpu.make_async_copy(srt.at[buf, pl.ds(src, rows), :],
                                     hs_ref.at[pl.ds(dst, rows), :], sem.at[buf])

    def drain(buf, n):
        _wait_granules(n, n_slots // GRANULE, lambda count: granule_copy(buf, 0, 0, count * GRANULE).wait())

    @pl.when(i == 0)
    def _():
        pend[0] = 0

    rec = route_ref[...]
    if tt < LANES:
        rec = jnp.concatenate([rec, jnp.zeros((LANES - tt, ROUTE_LANES), jnp.float32)], axis=0)
    rec_t = rec.T
    s1_row, s2_row = rec_t[2:3, :tt], rec_t[3:4, :tt]
    slot_iota = lax.broadcasted_iota(jnp.int32, (n_slots, tt), 0).astype(jnp.float32)
    perm = _bf(jnp.where((slot_iota == s1_row) | (slot_iota == s2_row), 1.0, 0.0))
    srt[cur] = _bf(_dot(perm, h_ref[...]))

    earlier = pend[0]
    pend[0] = _for_each_granule(tab_ref, lambda s, d: granule_copy(cur, s, d).start())
    drain(1 - cur, earlier)

    @pl.when(i == pl.num_programs(0) - 1)
    def _():
        drain(cur, pend[0])


def _dispatch_call(tab, h2, route, hs, *, n_slots):
    n = tab.shape[0]
    tt = h2.shape[0] // n
    d = h2.shape[1]
    return pl.pallas_call(
        _dispatch_kernel,
        grid=(n,),
        in_specs=[pl.BlockSpec((1, SUBLANES, ROUTE_LANES), lambda i: (i, 0, 0), memory_space=pltpu.SMEM),
                  pl.BlockSpec((tt, d), lambda i: (i, 0)),
                  pl.BlockSpec((tt, ROUTE_LANES), lambda i: (i, 0)),
                  pl.BlockSpec(memory_space=pl.ANY)],
        out_specs=pl.BlockSpec(memory_space=pl.ANY),
        scratch_shapes=[pltpu.VMEM((2, n_slots, d), jnp.bfloat16), pltpu.SemaphoreType.DMA((2,)),
                        pltpu.SMEM((1,), jnp.int32)],
        out_shape=jax.ShapeDtypeStruct(hs.shape, hs.dtype),
        input_output_aliases={3: 0},
        compiler_params=pltpu.CompilerParams(dimension_semantics=("arbitrary",), vmem_limit_bytes=VMEM_LIMIT),
        name="dispatch",
    )(tab, h2, route, hs)


def _expert_kernel(te_ref, nt_ref, hs_ref, wg_ref, wu_ref, wd_ref, ys_ref, wgu_bf, wd_bf):
    i = pl.program_id(0)
    active = i < nt_ref[0]

    @pl.when(active & ((i == 0) | (te_ref[i] != te_ref[jnp.maximum(i - 1, 0)])))
    def _():
        wgu_bf[:, :EXPERT_HIDDEN] = _bf(wg_ref[0, 0])
        wgu_bf[:, EXPERT_HIDDEN:] = _bf(wu_ref[0, 0])
        wd_bf[...] = _bf(wd_ref[0, 0])

    @pl.when(active)
    def _():
        gu = _dot(hs_ref[...], wgu_bf[...])
        hid = _silu(gu[:, :EXPERT_HIDDEN]) * gu[:, EXPERT_HIDDEN:]
        ys_ref[...] = _bf(_dot(_bf(hid), wd_bf[...]))


def _expert_call(tile_expert, n_tiles, hs, w_gate, w_up, w_down, *, layer, te_rows):
    npad, d = hs.shape
    wspec = lambda a: pl.BlockSpec((1, 1) + a.shape[2:], lambda i, te, nt: (layer, te[i], 0, 0))
    rows = pl.BlockSpec((te_rows, d), lambda i, te, nt: (jnp.minimum(i, nt[0] - 1), 0))
    grid_spec = pltpu.PrefetchScalarGridSpec(
        num_scalar_prefetch=2, grid=(npad // te_rows,),
        in_specs=[rows, wspec(w_gate), wspec(w_up), wspec(w_down)],
        out_specs=rows,
        scratch_shapes=[pltpu.VMEM((d, 2 * EXPERT_HIDDEN), jnp.bfloat16),
                        pltpu.VMEM((EXPERT_HIDDEN, d), jnp.bfloat16)])
    return pl.pallas_call(
        _expert_kernel,
        grid_spec=grid_spec,
        out_shape=jax.ShapeDtypeStruct((npad, d), jnp.bfloat16),
        input_output_aliases={2: 0},
        compiler_params=pltpu.CompilerParams(dimension_semantics=("arbitrary",), vmem_limit_bytes=VMEM_LIMIT),
        name="experts",
    )(tile_expert, n_tiles, hs, w_gate, w_up, w_down)


def _combine_kernel(tab_ref, tab_next_ref, route_ref, x_ref, mod_ref, fw_ref, ys_ref, o_ref, yloc, sem, pend, *,
                    final):
    i = pl.program_id(0)
    last = pl.num_programs(0) - 1
    cur = lax.rem(i, 2)

    def granule_copy(buf, loc, srt, rows=GRANULE):
        return pltpu.make_async_copy(ys_ref.at[pl.ds(srt, rows), :],
                                     yloc.at[buf, pl.ds(loc, rows), :], sem.at[buf])

    def gather(tab, buf):
        return _for_each_granule(tab, lambda s, d: granule_copy(buf, s, d).start())

    @pl.when(i == 0)
    def _():
        yloc[...] = jnp.zeros_like(yloc)
        pend[0] = gather(tab_ref, 0)

    n_cur = pend[0]

    @pl.when(i < last)
    def _():
        pend[0] = gather(tab_next_ref, 1 - cur)

    _wait_granules(n_cur, yloc.shape[1] // GRANULE, lambda count: granule_copy(cur, 0, 0, count * GRANULE).wait())

    rt = route_ref[...]
    tt, n_slots = rt.shape[0], yloc.shape[1]
    slot = lax.broadcasted_iota(jnp.int32, (tt, n_slots), 1).astype(jnp.float32)
    wc = jnp.where(slot == rt[:, 2:3], rt[:, 0:1], 0.0) + jnp.where(slot == rt[:, 3:4], rt[:, 1:2], 0.0)
    x_new = x_ref[...] + mod_ref[0, 5:6, :] * _dot(_bf(wc), yloc[cur])
    o_ref[...] = _rms_rows(x_new) * fw_ref[...] if final else x_new


def _combine_call(tab, route, x_mid, mod, fw, ys, *, n_slots, final):
    n = tab.shape[0]
    tt = route.shape[0] // n
    tiles_per_seq = n // mod.shape[0]
    d = ys.shape[-1]
    tab_spec = lambda imap: pl.BlockSpec((1, SUBLANES, ROUTE_LANES), imap, memory_space=pltpu.SMEM)
    return pl.pallas_call(
        functools.partial(_combine_kernel, final=final),
        grid=(n,),
        in_specs=[tab_spec(lambda i: (i, 0, 0)),
                  tab_spec(lambda i: (jnp.minimum(i + 1, n - 1), 0, 0)),
                  pl.BlockSpec((tt, ROUTE_LANES), lambda i: (i, 0)),
                  pl.BlockSpec((tt, d), lambda i: (i, 0)),
                  pl.BlockSpec((1,) + mod.shape[1:], lambda i: (i // tiles_per_seq, 0, 0)),
                  pl.BlockSpec((1, d), lambda i: (0, 0)),
                  pl.BlockSpec(memory_space=pl.ANY)],
        out_specs=pl.BlockSpec((tt, d), lambda i: (i, 0)),
        scratch_shapes=[pltpu.VMEM((2, n_slots, d), jnp.bfloat16), pltpu.SemaphoreType.DMA((2,)),
                        pltpu.SMEM((1,), jnp.int32)],
        out_shape=jax.ShapeDtypeStruct((n * tt, d), jnp.float32),
        compiler_params=pltpu.CompilerParams(dimension_semantics=("arbitrary",), vmem_limit_bytes=VMEM_LIMIT),
        name="combine",
    )(tab, tab, route, x_mid, mod, fw, ys)


def _block_diag_halves(w):
    nb, bd, _ = w.shape
    per = nb // 2
    eye = jnp.eye(per, dtype=w.dtype)
    halves = [jnp.einsum('nij,nm->nimj', w[h * per:(h + 1) * per], eye).reshape(per * bd, per * bd) for h in range(2)]
    return jnp.stack(halves)


def _tile_slots(tt):
    return -(-_tile_slots_used(tt) // LANES) * LANES


def _tile_slots_used(tt):
    return 2 * tt + (GRANULE - 1) * min(N_EXPERTS, 2 * tt)


def _tile_rows(t):
    for cand in (512, 256, 128, 64, 32, 16, 8):
        if t % cand == 0:
            return cand
    raise ValueError(f"sequence length {t} is not a multiple of 8")


def kernel(x_prompt, x_sample, state_conv, state_lru, state_hgrn, c_prompt, c_sample, w_ada, b_ada, w_in, conv_w,
           conv_b, lru_wa, lru_ba, lru_wx, lru_bx, lru_lambda, hg_lower, hg_norm_w, w_out, w_grp, b_grp, w_rt, b_rt,
           w_gate, w_up, w_down, final_norm_w):
    depth = w_in.shape[0]
    bp, tp, d = x_prompt.shape
    bs, ts, _ = x_sample.shape
    f32 = jnp.float32

    c_all = jnp.concatenate([c_prompt, c_sample], axis=0)
    rows = -(-c_all.shape[0] // SUBLANES) * SUBLANES
    c_all = jnp.pad(c_all, ((0, rows - c_all.shape[0]), (0, 0)))
    mod_all = _mod_call(c_all, w_ada, b_ada).reshape(depth, rows, 6, d)

    p_low = jax.nn.softmax(hg_lower.astype(f32), axis=0)
    lbs = jnp.cumsum(p_low, axis=0) - p_low[0]
    w_in_bf, w_out_bf = _bf(w_in), _bf(w_out)
    wr = jnp.concatenate([w_grp, w_rt], axis=-1)
    wr = jnp.pad(wr, ((0, 0), (0, 0), (0, ROUTE_LANES - wr.shape[-1])))
    wr_hi = _bf(wr)
    wr_hl = jnp.concatenate([wr_hi, _bf(wr - wr_hi.astype(f32))], axis=-1)
    br = jnp.concatenate([b_grp, b_rt], axis=-1)
    br = jnp.pad(br, ((0, 0), (0, ROUTE_LANES - br.shape[-1])))
    fw = final_norm_w.reshape(1, d)

    trunks = [
        dict(x=x_prompt, b0=0, nb=bp, t=tp,
             conv=jnp.zeros((depth, bp, CONV_W - 1, LRU_WIDTH), f32), lru=jnp.zeros((depth, bp, LRU_WIDTH), f32),
             hg=jnp.zeros((depth, bp, HG_HEADS, HG_DK, HG_DK), f32)),
        dict(x=x_sample, b0=bp, nb=bs, t=ts, conv=state_conv, lru=state_lru, hg=state_hgrn),
    ]
    for tr in trunks:
        tr["tt"] = _tile_rows(tr["t"])
        tr["chunk"] = min(HG_CHUNK, tr["tt"])
        tr.update(convs=[], lrus=[], hgs=[])

    n_rows = sum(tr["nb"] * (tr["t"] // tr["tt"]) * _tile_slots_used(tr["tt"]) for tr in trunks)
    max_tiles = (n_rows + N_EXPERTS * (EXPERT_TILE - 1)) // EXPERT_TILE
    hs_zero = jnp.zeros((max_tiles * EXPERT_TILE, d), jnp.bfloat16)
    tile_ids = jnp.arange(max_tiles, dtype=jnp.int32)
    expert_ids = jnp.arange(N_EXPERTS, dtype=jnp.int32)

    for l in range(depth):
        wts = (w_in_bf[l], conv_w[l], conv_b[l].reshape(1, -1), _bf(_block_diag_halves(lru_wa[l])),
               _bf(_block_diag_halves(lru_wx[l])), lru_ba[l].reshape(1, -1), lru_bx[l].reshape(1, -1),
               lru_lambda[l].reshape(1, -1), lbs[l].reshape(1, -1), hg_norm_w[l].reshape(1, -1), w_out_bf[l],
               wr_hi[l], wr_hl[l], br[l].reshape(1, -1))
        cnt = jnp.zeros((1, ROUTE_LANES), f32)
        for tr in trunks:
            mod = mod_all[l, tr["b0"]:tr["b0"] + tr["nb"]]
            tr["x"], tr["h2"], tr["route"], tr["tab"], cnt, conv_n, lru_n, hg_n = _mixer_call(
                tr["x"], mod, tr["conv"][l], tr["lru"][l], tr["hg"][l], wts, cnt, tt=tr["tt"], chunk=tr["chunk"])
            tr["mod"] = mod
            tr["convs"].append(conv_n)
            tr["lrus"].append(lru_n.reshape(tr["nb"], LRU_WIDTH))
            tr["hgs"].append(hg_n)

        counts = cnt[0, N_GROUPS:N_GROUPS + N_EXPERTS].astype(jnp.int32)
        tiles_e = (counts + (EXPERT_TILE - 1)) // EXPERT_TILE
        ends = jnp.cumsum(tiles_e)
        off = (ends - tiles_e) * EXPERT_TILE
        n_tiles = ends[-1:]
        tile_expert = jnp.sum(jnp.minimum(tile_ids, n_tiles - 1)[:, None] >= ends[None, :], axis=1).astype(jnp.int32)

        hs = hs_zero
        for tr in trunks:
            tab = tr["tab"].reshape((-1,) + tr["tab"].shape[2:])
            region = jnp.sum(jnp.where(tab[:, 0, :, None] == expert_ids, off, 0), axis=-1)
            tr["tab"] = tab.at[:, 1, :].add(region)
            tr["route"] = tr["route"].reshape(-1, ROUTE_LANES)
            hs = _dispatch_call(tr["tab"], tr["h2"].reshape(-1, d), tr["route"], hs, n_slots=_tile_slots(tr["tt"]))
        ys = _expert_call(tile_expert, n_tiles, hs, w_gate, w_up, w_down, layer=l, te_rows=EXPERT_TILE)
        for tr in trunks:
            x_new = _combine_call(tr["tab"], tr["route"], tr["x"].reshape(-1, d), tr["mod"], fw, ys,
                                  n_slots=_tile_slots(tr["tt"]), final=(l == depth - 1))
            tr["x"] = x_new.reshape(tr["nb"], tr["t"], d)

    (yp, cp, lp, hp), (ys_, cs, ls, hs_) = [
        (tr["x"], jnp.stack(tr["convs"]), jnp.stack(tr["lrus"]), jnp.stack(tr["hgs"])) for tr in trunks]
    return (yp, ys_, cp, lp, hp, cs, ls, hs_)
```
